```python
import math
import jax, jax.numpy as jnp
from jax import lax
import numpy as np

D_MODEL = 1024
BATCH = 2
SEQ = 8192
DEPTH = 2

GRID_W = 64
CTX_LEN = 256
EPS = 1e-6
F32 = jnp.float32

SSD_HEADS = 8
SSD_HEAD_DIM = 64
SSD_INNER = SSD_HEADS * SSD_HEAD_DIM
SSD_GROUPS = 2
SSD_STATE = 128
SSD_CHUNK = 128
SSD_CONV = 3
SSD_GN = SSD_GROUPS * SSD_STATE
SSD_XB = SSD_INNER + SSD_GN
SSD_CONV_DIM = SSD_XB + SSD_GN

ATTN_HEADS = 8
ATTN_KV_HEADS = 2
ATTN_HEAD_DIM = 64
ATTN_Q_PER_KV = ATTN_HEADS // ATTN_KV_HEADS
ATTN_INNER = ATTN_HEADS * ATTN_HEAD_DIM
ATTN_KV_INNER = ATTN_KV_HEADS * ATTN_HEAD_DIM
ATTN_BLOCK = 128
ATTN_SCALE = ATTN_HEAD_DIM ** -0.5
ROPE_THETA = 10000.0
ROPE_AXIS_DIM = ATTN_HEAD_DIM // 2

HY_WIDTH = 512
HY_CONV = 3
HY_BANDS = 16
HY_EMB = 1 + 2 * HY_BANDS
HY_HIDDEN = 64
HY_FAST_DECAY_PCT = 0.3
HY_SLOW_DECAY_PCT = 1.5
HY_TARGET = 1e-2

D_MIX = SSD_INNER + ATTN_INNER + HY_WIDTH
D_FF = ((8 * D_MODEL // 3 + 255) // 256) * 256
N_MOD = 6

OFF_K = 0
OFF_V = OFF_K + ATTN_KV_INNER
OFF_XB = OFF_V + ATTN_KV_INNER
OFF_DT = OFF_XB + SSD_XB
CTX_STATE_COLS = OFF_DT + 2 * SSD_HEADS
OFF_C = CTX_STATE_COLS
OFF_Q = OFF_C + SSD_GN
OFF_Z = OFF_Q + ATTN_INNER
OFF_HY = OFF_Z + SSD_INNER
N_IN = OFF_HY + 3 * HY_WIDTH

kernel_name = 'hybrid_ssd_gqa_hyena_prefix_dit'


def _rmsnorm(t, g):
    tf = t.astype(F32)
    y = tf * lax.rsqrt(jnp.mean(tf * tf, axis=-1, keepdims=True) + EPS)
    return (y * g.astype(F32)).astype(t.dtype)


def _modulate(h, shift, scale):
    return h * (1.0 + scale) + shift


def _flip(t):
    return jnp.flip(t, axis=1)


def _dwconv(u, w, b):
    width, ch = w.shape
    pad = (width - 1) // 2
    y = lax.conv_general_dilated(u, w[:, None, :], window_strides=(1,), padding=[(pad, pad)],
                                 dimension_numbers=('NWC', 'WIO', 'NWC'), feature_group_count=ch)
    return y + b


def _segsum(a):
    T = a.shape[-1]
    cs = jnp.cumsum(a, axis=-1)
    diff = cs[..., :, None] - cs[..., None, :]
    mask = jnp.tril(jnp.ones((T, T), dtype=bool))
    return jnp.where(mask, diff, -jnp.inf)


def _heads_from_groups(t):
    b, L, _ = t.shape
    t = t.reshape(b, L, SSD_GROUPS, SSD_STATE)
    return jnp.repeat(t, SSD_HEADS // SSD_GROUPS, axis=2)


def _ssd_prep(proj, p):
    xb = jax.nn.silu(_dwconv(proj[..., OFF_XB:OFF_XB + SSD_XB],
                             p['ssd_conv_w'][:, :SSD_XB], p['ssd_conv_b'][:SSD_XB]))
    b, L, _ = xb.shape
    xs = xb[..., :SSD_INNER].reshape(b, L, SSD_HEADS, SSD_HEAD_DIM).astype(F32)
    bm = _heads_from_groups(xb[..., SSD_INNER:]).astype(F32)
    dt_raw = proj[..., OFF_DT:OFF_DT + 2 * SSD_HEADS].reshape(b, L, 2, SSD_HEADS).astype(F32)
    dt = jax.nn.softplus(dt_raw + p['ssd_dt_bias'].astype(F32))
    return xs, bm, dt


def _ssd_cproj(proj, p):
    cm = jax.nn.silu(_dwconv(proj[..., OFF_C:OFF_C + SSD_GN],
                             p['ssd_conv_w'][:, SSD_XB:], p['ssd_conv_b'][SSD_XB:]))
    return _heads_from_groups(cm).astype(F32)


def _ssd_scan(xs, dt, a, bm, cm, init):
    b, L, H, P = xs.shape
    nc = L // SSD_CHUNK
    X = (xs * dt[..., None]).reshape(b, nc, SSD_CHUNK, H, P)
    Bc = bm.reshape(b, nc, SSD_CHUNK, H, SSD_STATE)
    Cc = cm.reshape(b, nc, SSD_CHUNK, H, SSD_STATE)
    a_dt = (dt * a).reshape(b, nc, SSD_CHUNK, H).transpose(0, 3, 1, 2)
    a_cs = jnp.cumsum(a_dt, axis=-1)
    scores = jnp.einsum('bclhn,bcshn->bhcls', Cc, Bc) * jnp.exp(_segsum(a_dt))
    y_diag = jnp.einsum('bhcls,bcshp->bclhp', scores, X)
    decay_states = jnp.exp(a_cs[..., -1:] - a_cs)
    states = jnp.einsum('bclhn,bhcl,bclhp->bchpn', Bc, decay_states, X)
    states = jnp.concatenate([init[:, None], states], axis=1)
    chunk_decay = jnp.exp(_segsum(jnp.pad(a_cs[..., -1], ((0, 0), (0, 0), (1, 0)))))
    states = jnp.einsum('bhzc,bchpn->bzhpn', chunk_decay, states)
    y_off = jnp.einsum('bclhn,bchpn,bhcl->bclhp', Cc, states[:, :-1], jnp.exp(a_cs))
    return (y_diag + y_off).reshape(b, L, H, P), states[:, -1]


def _ssd_final_state(xs, dt, a, bm):
    cs = jnp.cumsum(dt * a, axis=1)
    w = jnp.exp(cs[:, -1:] - cs) * dt
    return jnp.einsum('blhn,blh,blhp->bhpn', bm, w, xs)


def _ssd_out(y, xs, z, p):
    b, L = y.shape[:2]
    y = (y + p['ssd_d'].astype(F32)[:, None] * xs).reshape(b, L, SSD_INNER)
    y = y * jax.nn.silu(z.astype(F32))
    return _rmsnorm(y, p['ssd_norm_g']).astype(z.dtype)


def _ssd_mixer(proj, projc, p, last):
    a = -jnp.exp(p['ssd_a_log'].astype(F32))
    xs, bm, dt = _ssd_prep(proj, p)
    cm = _ssd_cproj(proj, p)
    xs_c, bm_c, dt_c = _ssd_prep(projc, p)
    if last:
        s_f = _ssd_final_state(xs_c, dt_c[:, :, 0], a[0], bm_c)
        s_b = _ssd_final_state(_flip(xs_c), _flip(dt_c[:, :, 1]), a[1], _flip(bm_c))
        y_c = None
    else:
        cm_c = _ssd_cproj(projc, p)
        zeros = jnp.zeros((xs_c.shape[0], SSD_HEADS, SSD_HEAD_DIM, SSD_STATE), F32)
        yc_f, s_f = _ssd_scan(xs_c, dt_c[:, :, 0], a[0], bm_c, cm_c, zeros)
        yc_b, s_b = _ssd_scan(_flip(xs_c), _flip(dt_c[:, :, 1]), a[1], _flip(bm_c), _flip(cm_c), zeros)
        y_c = _ssd_out(yc_f + _flip(yc_b), xs_c, projc[..., OFF_Z:OFF_Z + SSD_INNER], p)
    y_f, _ = _ssd_scan(xs, dt[:, :, 0], a[0], bm, cm, s_f)
    y_b, _ = _ssd_scan(_flip(xs), _flip(dt[:, :, 1]), a[1], _flip(bm), _flip(cm), s_b)
    y = _ssd_out(y_f + _flip(y_b), xs, proj[..., OFF_Z:OFF_Z + SSD_INNER], p)
    return y, y_c


def _rope_half(t, cos, sin):
    t1, t2 = jnp.split(t, 2, axis=-1)
    return jnp.concatenate([t1 * cos - t2 * sin, t2 * cos + t1 * sin], axis=-1)


def _rope2d(t, rope):
    cos_r, sin_r, cos_c, sin_c = rope
    tr, tc = jnp.split(t.astype(F32), 2, axis=-1)
    out = jnp.concatenate([_rope_half(tr, cos_r[:, None], sin_r[:, None]),
                           _rope_half(tc, cos_c[:, None], sin_c[:, None])], axis=-1)
    return out.astype(t.dtype)


def _gqa(q, k, v):
    s = jnp.einsum('bqgrd,bkgd->bgrqk', q, k).astype(F32)
    pr = jax.nn.softmax(s, axis=-1).astype(v.dtype)
    return jnp.einsum('bgrqk,bkgd->bqgrd', pr, v)


def _split_heads(t, n):
    return t.reshape(t.shape[0], t.shape[1], n, ATTN_HEAD_DIM)


def _attn_mixer(proj, projc, p, rope, last):
    b, L, _ = proj.shape
    q = _rope2d(_rmsnorm(_split_heads(proj[..., OFF_Q:OFF_Q + ATTN_INNER], ATTN_HEADS), p['q_norm_g']), rope) * ATTN_SCALE
    k = _rope2d(_rmsnorm(_split_heads(proj[..., OFF_K:OFF_K + ATTN_KV_INNER], ATTN_KV_HEADS), p['k_norm_g']), rope)
    v = _split_heads(proj[..., OFF_V:OFF_V + ATTN_KV_INNER], ATTN_KV_HEADS)
    kc = _rmsnorm(_split_heads(projc[..., OFF_K:OFF_K + ATTN_KV_INNER], ATTN_KV_HEADS), p['k_norm_g'])
    vc = _split_heads(projc[..., OFF_V:OFF_V + ATTN_KV_INNER], ATTN_KV_HEADS)
    k_all = jnp.concatenate([kc, k], axis=1)
    v_all = jnp.concatenate([vc, v], axis=1)
    nb = L // ATTN_BLOCK
    q_blocks = q.reshape(b, nb, ATTN_BLOCK, ATTN_KV_HEADS, ATTN_Q_PER_KV, ATTN_HEAD_DIM).transpose(1, 0, 2, 3, 4, 5)
    o = lax.map(lambda qb: _gqa(qb, k_all, v_all), q_blocks)
    y = _rmsnorm(o.transpose(1, 0, 2, 3, 4, 5).reshape(b, L, ATTN_INNER), p['attn_norm_g'])
    if last:
        return y, None
    bc, lc, _ = projc.shape
    qc = _rmsnorm(_split_heads(projc[..., OFF_Q:OFF_Q + ATTN_INNER], ATTN_HEADS), p['q_norm_g']) * ATTN_SCALE
    oc = _gqa(qc.reshape(bc, lc, ATTN_KV_HEADS, ATTN_Q_PER_KV, ATTN_HEAD_DIM), kc, vc)
    y_c = _rmsnorm(oc.reshape(bc, lc, ATTN_INNER), p['attn_norm_g'])
    return y, y_c


def _hyena_filter(L, p):
    t = jnp.linspace(0.0, 1.0, L, dtype=F32)[:, None]
    w = 2.0 * math.pi * jnp.arange(L, dtype=F32)[:, None] / L
    f = jnp.linspace(1e-4, HY_BANDS - 1, HY_BANDS, dtype=F32)
    feats = jnp.concatenate([t, jnp.cos(f * w), -jnp.sin(f * w)], axis=-1)
    freq = p['hy_freq'].astype(F32)
    hdn = jnp.sin(freq * (feats @ p['hy_w1'].astype(F32) + p['hy_b1'].astype(F32)))
    hdn = jnp.sin(freq * (hdn @ p['hy_w2'].astype(F32) + p['hy_b2'].astype(F32)))
    filt = (hdn @ p['hy_w3'].astype(F32)).reshape(L, 2, HY_WIDTH)
    max_decay = math.log(HY_TARGET) / HY_FAST_DECAY_PCT
    min_decay = math.log(HY_TARGET) / HY_SLOW_DECAY_PCT
    deltas = jnp.linspace(min_decay, max_decay, HY_WIDTH, dtype=F32)
    decay = jnp.exp(-t * jnp.abs(deltas))
    return filt * decay[:, None, :]


def _long_conv(u, filt, bias):
    L = u.shape[1]
    taps = jnp.concatenate([filt[:, 0], jnp.zeros((1, HY_WIDTH), F32), filt[:0:-1, 1]], axis=0)
    kf = jnp.fft.rfft(taps, n=2 * L, axis=0)
    uf = jnp.fft.rfft(u, n=2 * L, axis=1)
    y = jnp.fft.irfft(uf * kf[None], n=2 * L, axis=1)[:, :L]
    return y + u * bias


def _hyena_mixer(u_raw, p):
    u = _dwconv(u_raw, p['hy_conv_w'], p['hy_conv_b']).astype(F32)
    v, x1, x0 = jnp.split(u, 3, axis=-1)
    filt = _hyena_filter(u.shape[1], p)
    y = x0 * _long_conv(v * x1, filt, p['hy_bias'].astype(F32))
    return _rmsnorm(y, p['hy_norm_g']).astype(u_raw.dtype)


def _swiglu(h, p):
    gate, up = jnp.split(h @ p['w_gu'], 2, axis=-1)
    return (jax.nn.silu(gate) * up) @ p['w_down']


def _layer(x, xc, c_act, cctx_act, p, rope, last):
    mod = (c_act @ p['w_mod'] + p['b_mod'])[:, None, :]
    sh1, sc1, g1, sh2, sc2, g2 = jnp.split(mod, N_MOD, axis=-1)
    n_ctx_mod = 2 if last else N_MOD
    mod_c = cctx_act @ p['w_mod'][:, :n_ctx_mod * D_MODEL] + p['b_mod'][:n_ctx_mod * D_MODEL]
    mod_c = jnp.split(mod_c, n_ctx_mod)
    h = _modulate(_rmsnorm(x, p['norm1_g']), sh1, sc1)
    hc = _modulate(_rmsnorm(xc, p['norm1_g']), mod_c[0], mod_c[1])
    proj = h @ p['w_in']
    n_ctx_cols = CTX_STATE_COLS if last else N_IN
    projc = hc @ p['w_in'][:, :n_ctx_cols]
    y_ssd, y_ssd_c = _ssd_mixer(proj, projc, p, last)
    y_att, y_att_c = _attn_mixer(proj, projc, p, rope, last)
    y_hy = _hyena_mixer(proj[..., OFF_HY:], p)
    x = x + g1 * (jnp.concatenate([y_ssd, y_att, y_hy], axis=-1) @ p['w_out'])
    x = x + g2 * _swiglu(_modulate(_rmsnorm(x, p['norm2_g']), sh2, sc2), p)
    if last:
        return x, None
    y_hy_c = _hyena_mixer(projc[..., OFF_HY:], p)
    xc = xc + mod_c[2] * (jnp.concatenate([y_ssd_c, y_att_c, y_hy_c], axis=-1) @ p['w_out'])
    xc = xc + mod_c[5] * _swiglu(_modulate(_rmsnorm(xc, p['norm2_g']), mod_c[3], mod_c[4]), p)
    return x, xc


def setup_inputs(seed: int = 0) -> dict:
    key = jax.random.key(seed)
    ks = jax.random.split(key, 40)

    def nrm(k, shape, scale):
        return jax.random.normal(k, shape, F32) * scale

    def gain(k, shape):
        return 1.0 + 0.05 * jax.random.normal(k, shape, F32)

    dt0 = jnp.exp(jax.random.uniform(ks[10], (DEPTH, 2, SSD_HEADS), F32, math.log(1e-3), math.log(1e-1)))
    return {
        'x': nrm(ks[0], (BATCH, SEQ, D_MODEL), 1.0),
        'c': nrm(ks[1], (BATCH, D_MODEL), 1.0),
        'ctx': nrm(ks[2], (BATCH, CTX_LEN, D_MODEL), 1.0),
        'c_ctx': nrm(ks[3], (D_MODEL,), 1.0),
        'w_mod': nrm(ks[4], (DEPTH, D_MODEL, N_MOD * D_MODEL), 0.5 * D_MODEL ** -0.5),
        'b_mod': nrm(ks[5], (DEPTH, N_MOD * D_MODEL), 0.02),
        'norm1_g': gain(ks[6], (DEPTH, D_MODEL)),
        'w_in': nrm(ks[7], (DEPTH, D_MODEL, N_IN), D_MODEL ** -0.5),
        'ssd_conv_w': nrm(ks[8], (DEPTH, SSD_CONV, SSD_CONV_DIM), SSD_CONV ** -0.5),
        'ssd_conv_b': nrm(ks[9], (DEPTH, SSD_CONV_DIM), 0.02),
        'ssd_a_log': jnp.log(jax.random.uniform(ks[11], (DEPTH, 2, SSD_HEADS), F32, 1.0, 16.0)),
        'ssd_dt_bias': dt0 + jnp.log(-jnp.expm1(-dt0)),
        'ssd_d': 1.0 + 0.1 * jax.random.normal(ks[12], (DEPTH, SSD_HEADS), F32),
        'ssd_norm_g': gain(ks[13], (DEPTH, SSD_INNER)),
        'q_norm_g': gain(ks[14], (DEPTH, ATTN_HEAD_DIM)),
        'k_norm_g': gain(ks[15], (DEPTH, ATTN_HEAD_DIM)),
        'attn_norm_g': gain(ks[16], (DEPTH, ATTN_INNER)),
        'hy_conv_w': nrm(ks[17], (DEPTH, HY_CONV, 3 * HY_WIDTH), HY_CONV ** -0.5),
        'hy_conv_b': nrm(ks[18], (DEPTH, 3 * HY_WIDTH), 0.02),
        'hy_w1': nrm(ks[19], (DEPTH, HY_EMB, HY_HIDDEN), HY_EMB ** -0.5),
        'hy_b1': nrm(ks[20], (DEPTH, HY_HIDDEN), 0.02),
        'hy_freq': gain(ks[21], (DEPTH, HY_HIDDEN)),
        'hy_w2': nrm(ks[22], (DEPTH, HY_HIDDEN, HY_HIDDEN), HY_HIDDEN ** -0.5),
        'hy_b2': nrm(ks[23], (DEPTH, HY_HIDDEN), 0.02),
        'hy_w3': nrm(ks[24], (DEPTH, HY_HIDDEN, 2 * HY_WIDTH), HY_HIDDEN ** -0.5),
        'hy_bias': nrm(ks[25], (DEPTH, HY_WIDTH), 1.0),
        'hy_norm_g': gain(ks[26], (DEPTH, HY_WIDTH)),
        'w_out': nrm(ks[27], (DEPTH, D_MIX, D_MODEL), D_MIX ** -0.5),
        'norm2_g': gain(ks[28], (DEPTH, D_MODEL)),
        'w_gu': nrm(ks[29], (DEPTH, D_MODEL, 2 * D_FF), D_MODEL ** -0.5),
        'w_down': nrm(ks[30], (DEPTH, D_FF, D_MODEL), D_FF ** -0.5),
        'final_g': gain(ks[31], (D_MODEL,)),
    }


def reference(x, c, ctx, c_ctx, w_mod, b_mod, norm1_g, w_in, ssd_conv_w, ssd_conv_b, ssd_a_log,
              ssd_dt_bias, ssd_d, ssd_norm_g, q_norm_g, k_norm_g, attn_norm_g, hy_conv_w, hy_conv_b,
              hy_w1, hy_b1, hy_freq, hy_w2, hy_b2, hy_w3, hy_bias, hy_norm_g, w_out, norm2_g, w_gu,
              w_down, final_g):
    L = x.shape[1]
    ROWS = L // GRID_W
    row = jnp.repeat(jnp.arange(ROWS, dtype=F32), GRID_W)
    col = jnp.tile(jnp.arange(GRID_W, dtype=F32), ROWS)
    inv_freq = ROPE_THETA ** (-jnp.arange(0, ROPE_AXIS_DIM, 2, dtype=F32) / ROPE_AXIS_DIM)
    ang_r = row[:, None] * inv_freq
    ang_c = col[:, None] * inv_freq
    rope = (jnp.cos(ang_r), jnp.sin(ang_r), jnp.cos(ang_c), jnp.sin(ang_c))
    c_act = jax.nn.silu(c)
    cctx_act = jax.nn.silu(c_ctx)
    xc = ctx
    for l in range(DEPTH):
        p = {
            'w_mod': w_mod[l], 'b_mod': b_mod[l], 'norm1_g': norm1_g[l], 'w_in': w_in[l],
            'ssd_conv_w': ssd_conv_w[l], 'ssd_conv_b': ssd_conv_b[l], 'ssd_a_log': ssd_a_log[l],
            'ssd_dt_bias': ssd_dt_bias[l], 'ssd_d': ssd_d[l], 'ssd_norm_g': ssd_norm_g[l],
            'q_norm_g': q_norm_g[l], 'k_norm_g': k_norm_g[l], 'attn_norm_g': attn_norm_g[l],
            'hy_conv_w': hy_conv_w[l], 'hy_conv_b': hy_conv_b[l], 'hy_w1': hy_w1[l], 'hy_b1': hy_b1[l],
            'hy_freq': hy_freq[l], 'hy_w2': hy_w2[l], 'hy_b2': hy_b2[l], 'hy_w3': hy_w3[l],
            'hy_bias': hy_bias[l], 'hy_norm_g': hy_norm_g[l], 'w_out': w_out[l], 'norm2_g': norm2_g[l],
            'w_gu': w_gu[l], 'w_down': w_down[l],
        }
        x, xc = _layer(x, xc, c_act, cctx_act, p, rope, l == DEPTH - 1)
    return _rmsnorm(x, final_g)
```

```python
import functools
import math

import numpy as np
import jax
import jax.numpy as jnp
from jax import lax
from jax.experimental import pallas as pl
from jax.experimental.pallas import tpu as pltpu

F32 = jnp.float32
BF16 = jnp.bfloat16
HI = lax.Precision.HIGHEST

D_MODEL = 1024
GRID_W = 64
EPS = 1e-6
SSD_HEADS = 8
SSD_HEAD_DIM = 64
SSD_INNER = 512
SSD_STATE = 128
SSD_CHUNK = 128
SSD_GN = 256
SSD_XB = SSD_INNER + SSD_GN
ATTN_HEADS = 8
ATTN_KV_HEADS = 2
ATTN_HEAD_DIM = 64
ATTN_INNER = 512
ATTN_KV_INNER = 128
ATTN_SCALE = ATTN_HEAD_DIM ** -0.5
ROPE_THETA = 10000.0
ROPE_AXIS_DIM = ATTN_HEAD_DIM // 2
HY_WIDTH = 512
HY_BANDS = 16
HY_EMB = 1 + 2 * HY_BANDS
HY_HIDDEN = 64
HY_FAST_DECAY_PCT = 0.3
HY_SLOW_DECAY_PCT = 1.5
HY_TARGET = 1e-2
D_MIX = 1536
D_FF = 2816
N_MOD = 6

OFF_K = 0
OFF_V = OFF_K + ATTN_KV_INNER
OFF_XB = OFF_V + ATTN_KV_INNER
OFF_DT = OFF_XB + SSD_XB
OFF_C = OFF_DT + 2 * SSD_HEADS
OFF_Q = OFF_C + SSD_GN
OFF_Z = OFF_Q + ATTN_INNER
OFF_HY = OFF_Z + SSD_INNER

P_HY = 0
P_Q = 1536
P_Z = 2048
P_X = 2560
P_B = 3072
P_C = 3328
P_K = 3584
P_V = 3712
P_DT = 3840
NP = 4096

LANE = 128
SUBLANE = 8
VMEM_LIMIT = 48 * 1024 * 1024

FFT_R = 128
FFT_G = 8


def _cp(*sem):
    return pltpu.CompilerParams(dimension_semantics=sem, vmem_limit_bytes=VMEM_LIMIT)


def _silu(x):
    return x * (1.0 / (1.0 + jnp.exp(-x)))


def _softplus(x):
    return jnp.maximum(x, 0.0) + jnp.log(1.0 + jnp.exp(-jnp.abs(x)))


def _rms(x, g):
    return x * lax.rsqrt(jnp.mean(x * x, axis=-1, keepdims=True) + EPS) * g


def _dot(a, b):
    return jnp.dot(a, b, preferred_element_type=F32)


def _dot_hi(a, b):
    return jnp.dot(a, b, precision=HI, preferred_element_type=F32)


def _mod_kernel(c_ref, w_ref, b_ref, o_ref):
    o_ref[...] = _dot_hi(_silu(c_ref[...]), w_ref[...]) + b_ref[...]


def _mod_call(c_rows, w, b):
    n = w.shape[1]
    tn = 1024
    return pl.pallas_call(
        _mod_kernel,
        grid=(n // tn,),
        in_specs=[pl.BlockSpec((SUBLANE, D_MODEL), lambda j: (0, 0)),
                  pl.BlockSpec((D_MODEL, tn), lambda j: (0, j)),
                  pl.BlockSpec((1, tn), lambda j: (0, j))],
        out_specs=pl.BlockSpec((SUBLANE, tn), lambda j: (0, j)),
        out_shape=jax.ShapeDtypeStruct((SUBLANE, n), F32),
        compiler_params=_cp("parallel"),
        name="adaln_mod",
    )(c_rows, w, b)


def _proj_kernel(x_ref, sh_ref, sc_ref, g_ref, w_ref, o_ref, h_ref):
    @pl.when(pl.program_id(2) == 0)
    def _():
        y = _rms(x_ref[0], g_ref[...])
        h_ref[...] = (y * (1.0 + sc_ref[0]) + sh_ref[0]).astype(BF16)

    o_ref[0] = _dot(h_ref[...], w_ref[...])


def _proj_in(x, sh, sc, g, w):
    b, l, _ = x.shape
    tm = min(l, 1024)
    tn = 512
    return pl.pallas_call(
        _proj_kernel,
        grid=(b, l // tm, NP // tn),
        in_specs=[pl.BlockSpec((1, tm, D_MODEL), lambda bi, i, j: (bi, i, 0)),
                  pl.BlockSpec((1, 1, D_MODEL), lambda bi, i, j: (bi, 0, 0)),
                  pl.BlockSpec((1, 1, D_MODEL), lambda bi, i, j: (bi, 0, 0)),
                  pl.BlockSpec((1, D_MODEL), lambda bi, i, j: (0, 0)),
                  pl.BlockSpec((D_MODEL, tn), lambda bi, i, j: (0, j))],
        out_specs=pl.BlockSpec((1, tm, tn), lambda bi, i, j: (bi, i, j)),
        out_shape=jax.ShapeDtypeStruct((b, l, NP), F32),
        scratch_shapes=[pltpu.VMEM((tm, D_MODEL), BF16)],
        compiler_params=_cp("parallel", "parallel", "arbitrary"),
        name="proj_in",
    )(x, sh, sc, g, w)


def _dwconv3(u, prev_row, next_row, w, b):
    tm = u.shape[0]
    ri = lax.broadcasted_iota(jnp.int32, u.shape, 0)
    um = jnp.where(ri == 0, prev_row, pltpu.roll(u, 1, 0))
    up = jnp.where(ri == tm - 1, next_row, pltpu.roll(u, tm - 1, 0))
    return um * w[0:1] + u * w[1:2] + up * w[2:3] + b


def _conv_group(refs, i, n_i):
    u_ref, p_ref, n_ref, w_ref, b_ref = refs
    prev_row = jnp.where(i > 0, p_ref[0, SUBLANE - 1:SUBLANE, :], 0.0)
    next_row = jnp.where(i < n_i - 1, n_ref[0, 0:1, :], 0.0)
    return _dwconv3(u_ref[0], prev_row, next_row, w_ref[...], b_ref[...])


def _ssd_conv_kernel(u_ref, p_ref, n_ref, w_ref, b_ref, o_ref):
    y = _conv_group((u_ref, p_ref, n_ref, w_ref, b_ref), pl.program_id(1), pl.num_programs(1))
    o_ref[0] = _silu(y)


def _hy_conv_kernel(*refs):
    i, n_i = pl.program_id(1), pl.num_programs(1)
    v = _conv_group(refs[0:5], i, n_i)
    x1 = _conv_group(refs[5:10], i, n_i)
    x0 = _conv_group(refs[10:15], i, n_i)
    vx_ref, x0_ref = refs[15], refs[16]
    vx_ref[0] = v * x1
    x0_ref[0] = x0


def _conv_specs(tm, tc, l, col_block, w_block, grid_rank):
    nrb = l // SUBLANE
    per = tm // SUBLANE
    if grid_rank == 3:
        def wrap(f):
            return lambda bi, i, j: f(bi, i, j)
    else:
        def wrap(f):
            return lambda bi, i: f(bi, i, 0)
    return [
        pl.BlockSpec((1, tm, tc), wrap(lambda bi, i, j: (bi, i, col_block + j))),
        pl.BlockSpec((1, SUBLANE, tc), wrap(lambda bi, i, j: (bi, jnp.maximum(i * per - 1, 0), col_block + j))),
        pl.BlockSpec((1, SUBLANE, tc), wrap(lambda bi, i, j: (bi, jnp.minimum((i + 1) * per, nrb - 1), col_block + j))),
        pl.BlockSpec((SUBLANE, tc), wrap(lambda bi, i, j: (0, w_block + j))),
        pl.BlockSpec((1, tc), wrap(lambda bi, i, j: (0, w_block + j))),
    ]


def _pad_taps(w):
    return jnp.pad(w, ((0, SUBLANE - w.shape[0]), (0, 0)))


def _ssd_conv(proj, conv_w, conv_b):
    b, l, _ = proj.shape
    tm = min(l, 1024)
    tc = 512
    specs = _conv_specs(tm, tc, l, P_X // tc, 0, 3)
    return pl.pallas_call(
        _ssd_conv_kernel,
        grid=(b, l // tm, 2),
        in_specs=specs,
        out_specs=pl.BlockSpec((1, tm, tc), lambda bi, i, j: (bi, i, j)),
        out_shape=jax.ShapeDtypeStruct((b, l, 2 * tc), F32),
        compiler_params=_cp("parallel", "parallel", "parallel"),
        name="ssd_conv",
    )(proj, proj, proj, _pad_taps(conv_w), conv_b[None])


def _hy_conv(proj, conv_w, conv_b):
    b, l, _ = proj.shape
    tm = min(l, 1024)
    tc = HY_WIDTH
    wp = _pad_taps(conv_w)
    bp = conv_b[None]
    specs, args = [], []
    for grp in range(3):
        specs += _conv_specs(tm, tc, l, P_HY // tc + grp, grp, 2)
        args += [proj, proj, proj, wp, bp]
    out_spec = pl.BlockSpec((1, tm, tc), lambda bi, i: (bi, i, 0))
    shp = jax.ShapeDtypeStruct((b, l, tc), F32)
    return pl.pallas_call(
        _hy_conv_kernel,
        grid=(b, l // tm),
        in_specs=specs,
        out_specs=[out_spec, out_spec],
        out_shape=[shp, shp],
        compiler_params=_cp("parallel", "parallel"),
        name="hy_conv",
    )(*args)


def _ssd_chunk(xbc, dt_raw, a, bias, d_row, e_d, st, lane0, fwd):
    q = SSD_CHUNK
    xs = xbc[:, :SSD_INNER]
    bm = xbc[:, SSD_INNER:SSD_XB]
    cm = xbc[:, SSD_XB:]
    ri = lax.broadcasted_iota(jnp.int32, (q, q), 0)
    ci = lax.broadcasted_iota(jnp.int32, (q, q), 1)
    dt = _softplus(dt_raw + bias)
    adt = dt * a
    cs = _dot_hi((ci <= ri).astype(F32), adt)
    key = cs if fwd else cs - adt
    key_t = key.T
    key_e = _dot_hi(key, e_d)
    dt_e = _dot_hi(dt, e_d)
    tot_e = _dot_hi(cs, e_d)[q - 1:q, :]
    if fwd:
        w_state = dt_e * jnp.exp(tot_e - key_e)
        e_off = jnp.exp(key_e)
        mask = ci <= ri
    else:
        w_state = dt_e * jnp.exp(key_e)
        e_off = jnp.exp(tot_e - key_e)
        mask = ci >= ri
    x_dt = (xs * dt_e).astype(BF16)
    x_w = (xs * w_state).astype(BF16)
    st_b = st.astype(BF16)
    y_parts, st_parts, off_parts = [], [], []
    hg = SSD_HEADS // 2
    for g in range(2):
        bg = bm[:, g * SSD_STATE:(g + 1) * SSD_STATE]
        cg = cm[:, g * SSD_STATE:(g + 1) * SSD_STATE].astype(BF16)
        bg_t = bg.T.astype(BF16)
        gmat = _dot(cg, bg_t)
        gs = slice(g * hg * SSD_HEAD_DIM, (g + 1) * hg * SSD_HEAD_DIM)
        off_parts.append(_dot(cg, st_b[:, gs]))
        st_parts.append(_dot(bg_t, x_w[:, gs]))
        for hh in range(hg):
            h = g * hg + hh
            col = key[:, lane0 + h:lane0 + h + 1]
            row = key_t[lane0 + h:lane0 + h + 1, :]
            diff = (col - row) if fwd else (row - col)
            lm = jnp.exp(jnp.where(mask, diff, -1e30))
            s = (gmat * lm).astype(BF16)
            y_parts.append(_dot(s, x_dt[:, h * SSD_HEAD_DIM:(h + 1) * SSD_HEAD_DIM]))
    y = jnp.concatenate(y_parts, axis=1) + jnp.concatenate(off_parts, axis=1) * e_off
    if fwd:
        y = y + d_row * xs
    st_new = st * jnp.exp(tot_e) + jnp.concatenate(st_parts, axis=1)
    return y, st_new


def _ssd_kernel(xf_ref, xb_ref, dtf_ref, dtb_ref, bias_ref, alog_ref, d_ref, e_ref, sf0_ref, sb0_ref,
                yf_ref, yb_ref, sf_ref, sb_ref, stf, stb):
    c = pl.program_id(1)

    @pl.when(c == 0)
    def _():
        stf[...] = sf0_ref[0]
        stb[...] = sb0_ref[0]

    a = -jnp.exp(alog_ref[...])
    bias = bias_ref[...]
    yf, sf = _ssd_chunk(xf_ref[0], dtf_ref[0], a, bias, d_ref[...], e_ref[:, :SSD_INNER], stf[...], 0, True)
    yb, sb = _ssd_chunk(xb_ref[0], dtb_ref[0], a, bias, d_ref[...], e_ref[:, SSD_INNER:], stb[...],
                        SSD_HEADS, False)
    yf_ref[0] = yf
    yb_ref[0] = yb
    stf[...] = sf
    stb[...] = sb
    sf_ref[0] = sf
    sb_ref[0] = sb


def _ssd_scan(xbc, proj, sf0, sb0, p):
    b, l, _ = xbc.shape
    nc = l // SSD_CHUNK
    q = SSD_CHUNK
    dtb = P_DT // LANE
    st_spec = pl.BlockSpec((1, SSD_STATE, SSD_INNER), lambda bi, c: (bi, 0, 0))
    y_shape = jax.ShapeDtypeStruct((b, l, SSD_INNER), F32)
    st_shape = jax.ShapeDtypeStruct((b, SSD_STATE, SSD_INNER), F32)
    row = lambda n: pl.BlockSpec((1, n), lambda bi, c: (0, 0))
    return pl.pallas_call(
        _ssd_kernel,
        grid=(b, nc),
        in_specs=[pl.BlockSpec((1, q, 2 * SSD_INNER), lambda bi, c: (bi, c, 0)),
                  pl.BlockSpec((1, q, 2 * SSD_INNER), lambda bi, c: (bi, nc - 1 - c, 0)),
                  pl.BlockSpec((1, q, LANE), lambda bi, c: (bi, c, dtb)),
                  pl.BlockSpec((1, q, LANE), lambda bi, c: (bi, nc - 1 - c, dtb)),
                  row(LANE), row(LANE), row(SSD_INNER),
                  pl.BlockSpec((LANE, 2 * SSD_INNER), lambda bi, c: (0, 0)),
                  st_spec, st_spec],
        out_specs=[pl.BlockSpec((1, q, SSD_INNER), lambda bi, c: (bi, c, 0)),
                   pl.BlockSpec((1, q, SSD_INNER), lambda bi, c: (bi, nc - 1 - c, 0)),
                   st_spec, st_spec],
        out_shape=[y_shape, y_shape, st_shape, st_shape],
        scratch_shapes=[pltpu.VMEM((SSD_STATE, SSD_INNER), F32), pltpu.VMEM((SSD_STATE, SSD_INNER), F32)],
        compiler_params=_cp("parallel", "arbitrary"),
        name="ssd_scan",
    )(xbc, xbc, proj, proj, p["dt_bias_row"], p["a_log_row"], p["d_row"], p["head_expand"], sf0, sb0)


def _attn_prep_kernel(q_ref, k_ref, v_ref, cos_ref, sin_ref, gq_ref, gk_ref, gm_ref, qo_ref, ko_ref, vo_ref):
    cos = cos_ref[...]
    sin = sin_ref[...]
    gm = gm_ref[...]
    lane = lax.broadcasted_iota(jnp.int32, cos.shape, 1)
    first = jnp.bitwise_and(lane, 31) < 16
    hd = ATTN_HEAD_DIM

    def norm_rope(t, g):
        ms = _dot_hi(t * t, gm)
        y = t * lax.rsqrt(ms + EPS) * g
        partner = jnp.where(first, pltpu.roll(y, LANE - 16, 1), pltpu.roll(y, 16, 1))
        return y * cos + partner * sin

    for s in range(ATTN_INNER // LANE):
        qs = norm_rope(q_ref[0, :, s * LANE:(s + 1) * LANE], gq_ref[...]) * ATTN_SCALE
        qo_ref[0, 2 * s] = qs[:, :hd].astype(BF16)
        qo_ref[0, 2 * s + 1] = qs[:, hd:].astype(BF16)
    ks = norm_rope(k_ref[0], gk_ref[...])
    ko_ref[0, 0] = ks[:, :hd].astype(BF16)
    ko_ref[0, 1] = ks[:, hd:].astype(BF16)
    vv = v_ref[0]
    vo_ref[0, 0] = vv[:, :hd].astype(BF16)
    vo_ref[0, 1] = vv[:, hd:].astype(BF16)


def _attn_prep(proj, cos, sin, gq, gk, gm):
    b, l, _ = proj.shape
    tm = min(l, 1024)
    hd = ATTN_HEAD_DIM
    const = lambda r, c: pl.BlockSpec((r, c), lambda bi, i: (0, 0))
    return pl.pallas_call(
        _attn_prep_kernel,
        grid=(b, l // tm),
        in_specs=[pl.BlockSpec((1, tm, ATTN_INNER), lambda bi, i: (bi, i, P_Q // ATTN_INNER)),
                  pl.BlockSpec((1, tm, LANE), lambda bi, i: (bi, i, P_K // LANE)),
                  pl.BlockSpec((1, tm, LANE), lambda bi, i: (bi, i, P_V // LANE)),
                  pl.BlockSpec((tm, LANE), lambda bi, i: (i, 0)),
                  pl.BlockSpec((tm, LANE), lambda bi, i: (i, 0)),
                  const(1, LANE), const(1, LANE), const(LANE, LANE)],
        out_specs=[pl.BlockSpec((1, ATTN_HEADS, tm, hd), lambda bi, i: (bi, 0, i, 0)),
                   pl.BlockSpec((1, ATTN_KV_HEADS, tm, hd), lambda bi, i: (bi, 0, i, 0)),
                   pl.BlockSpec((1, ATTN_KV_HEADS, tm, hd), lambda bi, i: (bi, 0, i, 0))],
        out_shape=[jax.ShapeDtypeStruct((b, ATTN_HEADS, l, hd), BF16),
                   jax.ShapeDtypeStruct((b, ATTN_KV_HEADS, l, hd), BF16),
                   jax.ShapeDtypeStruct((b, ATTN_KV_HEADS, l, hd), BF16)],
        compiler_params=_cp("parallel", "parallel"),
        name="attn_prep",
    )(proj, proj, proj, cos, sin, gq, gk, gm)


def _flash_kernel(q_ref, k_ref, v_ref, o_ref, *, tk, nk):
    r = ATTN_HEADS // ATTN_KV_HEADS
    tq = q_ref.shape[2]
    hd = ATTN_HEAD_DIM
    q = q_ref[0].reshape(r * tq, hd)

    def body(j, carry):
        m, l, acc = carry
        start = pl.multiple_of(j * tk, tk)
        ks = k_ref[0, 0, pl.ds(start, tk), :]
        vs = v_ref[0, 0, pl.ds(start, tk), :]
        s = lax.dot_general(q, ks, (((1,), (1,)), ((), ())), preferred_element_type=F32)
        mn = jnp.maximum(m, jnp.max(s, axis=-1, keepdims=True))
        alpha = jnp.exp(m - mn)
        p = jnp.exp(s - mn)
        l = alpha * l + jnp.sum(p, axis=-1, keepdims=True)
        acc = alpha * acc + _dot(p.astype(BF16), vs)
        return mn, l, acc

    m0 = jnp.full((r * tq, 1), -1e30, F32)
    l0 = jnp.zeros((r * tq, 1), F32)
    a0 = jnp.zeros((r * tq, hd), F32)
    m, l, acc = lax.fori_loop(0, nk, body, (m0, l0, a0))
    o = acc * (1.0 / l)
    o_ref[0] = jnp.concatenate([o[h * tq:(h + 1) * tq] for h in range(r)], axis=1)


def _flash(q, k, v):
    b, _, l, hd = q.shape
    lk = k.shape[2]
    r = ATTN_HEADS // ATTN_KV_HEADS
    tq = 256
    tk = 768 if lk % 768 == 0 else 256
    return pl.pallas_call(
        functools.partial(_flash_kernel, tk=tk, nk=lk // tk),
        grid=(b, ATTN_KV_HEADS, l // tq),
        in_specs=[pl.BlockSpec((1, r, tq, hd), lambda bi, g, i: (bi, g, i, 0)),
                  pl.BlockSpec((1, 1, lk, hd), lambda bi, g, i: (bi, g, 0, 0)),
                  pl.BlockSpec((1, 1, lk, hd), lambda bi, g, i: (bi, g, 0, 0))],
        out_specs=pl.BlockSpec((1, tq, r * hd), lambda bi, g, i: (bi, i, g)),
        out_shape=jax.ShapeDtypeStruct((b, l, ATTN_INNER), F32),
        compiler_params=_cp("parallel", "parallel", "parallel"),
        name="flash_gqa",
    )(q, k, v)


def _taps_kernel(f_ref, w1_ref, b1_ref, fr_ref, w2_ref, b2_ref, w3_ref, adel_ref, o_ref, *, seq):
    tm = f_ref.shape[0]
    f = f_ref[...]
    fr = fr_ref[...]
    h = jnp.sin(fr * (_dot_hi(f, w1_ref[...]) + b1_ref[...]))
    h = jnp.sin(fr * (_dot_hi(h, w2_ref[...]) + b2_ref[...]))
    y = _dot_hi(h, w3_ref[...]) * jnp.exp(-f[:, 0:1] * adel_ref[...])
    row = pl.program_id(0) * tm + lax.broadcasted_iota(jnp.int32, y.shape, 0)
    o_ref[...] = jnp.where(row == seq, 0.0, y)


def _hy_taps(feats, p, adel, seq):
    n = 2 * seq
    tm = min(seq, 1024)
    nh = seq // tm
    const = lambda r, c: pl.BlockSpec((r, c), lambda i: (0, 0))
    return pl.pallas_call(
        functools.partial(_taps_kernel, seq=seq),
        grid=(n // tm,),
        in_specs=[pl.BlockSpec((tm, LANE), lambda i: (i, 0)),
                  const(LANE, HY_HIDDEN), const(1, HY_HIDDEN), const(1, HY_HIDDEN),
                  const(HY_HIDDEN, HY_HIDDEN), const(1, HY_HIDDEN),
                  pl.BlockSpec((HY_HIDDEN, HY_WIDTH), lambda i: (0, i // nh)),
                  const(1, HY_WIDTH)],
        out_specs=pl.BlockSpec((tm, HY_WIDTH), lambda i: (i, 0)),
        out_shape=jax.ShapeDtypeStruct((n, HY_WIDTH), F32),
        compiler_params=_cp("parallel"),
        name="hy_taps",
    )(feats, p["hy_w1p"], p["hy_b1"][None], p["hy_freq"][None], p["hy_w2"], p["hy_b2"][None], p["hy_w3"], adel)


def _fft1_kernel(x_ref, t_ref, are_ref, aim_ref, *, full):
    g = pl.program_id(2)
    r_ = FFT_R
    half = r_ // 2
    if full:
        k1 = lax.broadcasted_iota(jnp.int32, (2 * r_, LANE), 0)
        sgn = jnp.where(jnp.bitwise_and(k1, 1) == 0, 1.0, -1.0)
    for r in range(FFT_G):
        n2 = g * FFT_G + r
        t = t_ref[r]
        a = _dot_hi(t, x_ref[0, pl.ds(n2, half, stride=r_), :])
        if full:
            a = a + sgn * _dot_hi(t, x_ref[0, pl.ds(half * r_ + n2, half, stride=r_), :])
        are_ref[0, r] = a[:r_]
        aim_ref[0, r] = a[r_:]


def _fft1(x, table, full):
    nb, ln, c = x.shape
    r_ = FFT_R
    spec_o = pl.BlockSpec((1, FFT_G, r_, LANE), lambda bi, j, g: (bi, g, 0, j))
    shp = jax.ShapeDtypeStruct((nb, r_, r_, c), F32)
    return pl.pallas_call(
        functools.partial(_fft1_kernel, full=full),
        grid=(nb, c // LANE, r_ // FFT_G),
        in_specs=[pl.BlockSpec((1, ln, LANE), lambda bi, j, g: (bi, 0, j)),
                  pl.BlockSpec((FFT_G, 2 * r_, r_ // 2), lambda bi, j, g: (g, 0, 0))],
        out_specs=[spec_o, spec_o],
        out_shape=[shp, shp],
        compiler_params=_cp("parallel", "parallel", "arbitrary"),
        name="hy_fft1",
    )(x, table)


def _fft2_filter_kernel(are_ref, aim_ref, fs_ref, kre_ref, kim_ref):
    r_ = FFT_R
    for r in range(SUBLANE):
        a = jnp.concatenate([are_ref[0, :, r, :], aim_ref[0, :, r, :]], axis=0)
        b = _dot_hi(fs_ref[...], a)
        kre_ref[r] = b[:r_]
        kim_ref[r] = b[r_:]


def _fft2_filter(are, aim, fs):
    r_ = FFT_R
    c = are.shape[-1]
    spec_i = pl.BlockSpec((1, r_, SUBLANE, c), lambda kg: (0, 0, kg, 0))
    spec_o = pl.BlockSpec((SUBLANE, r_, c), lambda kg: (kg, 0, 0))
    shp = jax.ShapeDtypeStruct((r_, r_, c), F32)
    return pl.pallas_call(
        _fft2_filter_kernel,
        grid=(r_ // SUBLANE,),
        in_specs=[spec_i, spec_i, pl.BlockSpec((2 * r_, 2 * r_), lambda kg: (0, 0))],
        out_specs=[spec_o, spec_o],
        out_shape=[shp, shp],
        compiler_params=_cp("parallel"),
        name="hy_fft2_filter",
    )(are, aim, fs)


def _fft2_kernel(are_ref, aim_ref, kre_ref, kim_ref, fs_ref, fc_ref, zre_ref, zim_ref):
    r_ = FFT_R
    for r in range(SUBLANE):
        a = jnp.concatenate([are_ref[0, :, r, :], aim_ref[0, :, r, :]], axis=0)
        b = _dot_hi(fs_ref[...], a)
        br, bi = b[:r_], b[r_:]
        kr, ki = kre_ref[r], kim_ref[r]
        y = jnp.concatenate([br * kr - bi * ki, br * ki + bi * kr], axis=0)
        z = _dot_hi(fc_ref[...], y)
        zre_ref[0, :, r, :] = z[:r_]
        zim_ref[0, :, r, :] = z[r_:]


def _fft2(are, aim, kre, kim, fs, fc):
    nb, r_, _, c = are.shape
    spec_a = pl.BlockSpec((1, r_, SUBLANE, c), lambda bi, kg: (bi, 0, kg, 0))
    spec_k = pl.BlockSpec((SUBLANE, r_, c), lambda bi, kg: (kg, 0, 0))
    spec_f = pl.BlockSpec((2 * r_, 2 * r_), lambda bi, kg: (0, 0))
    shp = jax.ShapeDtypeStruct(are.shape, F32)
    return pl.pallas_call(
        _fft2_kernel,
        grid=(nb, r_ // SUBLANE),
        in_specs=[spec_a, spec_a, spec_k, spec_k, spec_f, spec_f],
        out_specs=[spec_a, spec_a],
        out_shape=[shp, shp],
        compiler_params=_cp("parallel", "parallel"),
        name="hy_fft2",
    )(are, aim, kre, kim, fs, fc)


def _ifft1_kernel(zre_ref, zim_ref, t_ref, vx_ref, x0_ref, bias_ref, o_ref):
    g = pl.program_id(2)
    r_ = FFT_R
    half = r_ // 2
    for r in range(FFT_G):
        n2 = g * FFT_G + r
        z = jnp.concatenate([zre_ref[0, r], zim_ref[0, r]], axis=0)
        conv = _dot_hi(t_ref[r], z)
        rows = pl.ds(n2, half, stride=r_)
        o_ref[0, rows, :] = x0_ref[0, rows, :] * (conv + bias_ref[...] * vx_ref[0, rows, :])


def _ifft1(zre, zim, table, vx, x0, bias):
    nb, seq, c = vx.shape
    r_ = FFT_R
    spec_z = pl.BlockSpec((1, FFT_G, r_, LANE), lambda bi, j, g: (bi, g, 0, j))
    spec_x = pl.BlockSpec((1, seq, LANE), lambda bi, j, g: (bi, 0, j))
    return pl.pallas_call(
        _ifft1_kernel,
        grid=(nb, c // LANE, r_ // FFT_G),
        in_specs=[spec_z, spec_z,
                  pl.BlockSpec((FFT_G, r_ // 2, 2 * r_), lambda bi, j, g: (g, 0, 0)),
                  spec_x, spec_x,
                  pl.BlockSpec((1, LANE), lambda bi, j, g: (0, j))],
        out_specs=spec_x,
        out_shape=jax.ShapeDtypeStruct((nb, seq, c), F32),
        compiler_params=_cp("parallel", "parallel", "arbitrary"),
        name="hy_ifft1",
    )(zre, zim, table, vx, x0, bias)


def _hy_ctx_kernel(vx_ref, x0_ref, taps_ref, ff_ref, fi_ref, bias_ref, o_ref):
    vx = vx_ref[0]
    seq = vx.shape[0]
    n = 2 * seq
    u = _dot_hi(ff_ref[:, :seq], vx)
    k = _dot_hi(ff_ref[...], taps_ref[...])
    ur, ui = u[:n], u[n:]
    kr, ki = k[:n], k[n:]
    y = jnp.concatenate([ur * kr - ui * ki, ur * ki + ui * kr], axis=0)
    conv = _dot_hi(fi_ref[...], y)
    o_ref[0] = x0_ref[0] * (conv + bias_ref[...] * vx)


def _hy_ctx(vx, x0, taps, ff, fi, bias):
    b, seq, c = vx.shape
    n = 2 * seq
    spec_x = pl.BlockSpec((1, seq, c), lambda bi: (bi, 0, 0))
    const = lambda r, cc: pl.BlockSpec((r, cc), lambda bi: (0, 0))
    return pl.pallas_call(
        _hy_ctx_kernel,
        grid=(b,),
        in_specs=[spec_x, spec_x, const(n, c), const(2 * n, n), const(seq, 2 * n), const(1, c)],
        out_specs=spec_x,
        out_shape=jax.ShapeDtypeStruct((b, seq, c), F32),
        compiler_params=_cp("parallel"),
        name="hy_ctx",
    )(vx, x0, taps, ff, fi, bias)


def _mix_kernel(x_ref, yf_ref, yb_ref, z_ref, at_ref, hy_ref, g1_ref, gs_ref, ga_ref, gh_ref, w_ref, o_ref):
    ys = _rms((yf_ref[0] + yb_ref[0]) * _silu(z_ref[0]), gs_ref[...]).astype(BF16)
    ya = _rms(at_ref[0], ga_ref[...]).astype(BF16)
    yh = _rms(hy_ref[0], gh_ref[...]).astype(BF16)
    r = (_dot(ys, w_ref[0:SSD_INNER, :])
         + _dot(ya, w_ref[SSD_INNER:SSD_INNER + ATTN_INNER, :])
         + _dot(yh, w_ref[SSD_INNER + ATTN_INNER:, :]))
    o_ref[0] = x_ref[0] + g1_ref[0] * r


def _mix_out(x, yf, yb, proj, at, hy, g1, gs, ga, gh, w):
    b, l, _ = x.shape
    tm = min(l, 512)
    c = SSD_INNER
    t512 = lambda col: pl.BlockSpec((1, tm, c), lambda bi, i: (bi, i, col))
    const = lambda r, cc: pl.BlockSpec((r, cc), lambda bi, i: (0, 0))
    xs = pl.BlockSpec((1, tm, D_MODEL), lambda bi, i: (bi, i, 0))
    return pl.pallas_call(
        _mix_kernel,
        grid=(b, l // tm),
        in_specs=[xs, t512(0), t512(0), t512(P_Z // c), t512(0), t512(0),
                  pl.BlockSpec((1, 1, D_MODEL), lambda bi, i: (bi, 0, 0)),
                  const(1, c), const(1, c), const(1, c), const(D_MIX, D_MODEL)],
        out_specs=xs,
        out_shape=jax.ShapeDtypeStruct(x.shape, F32),
        compiler_params=_cp("parallel", "parallel"),
        name="mix_out",
    )(x, yf, yb, proj, at, hy, g1, gs, ga, gh, w)


def _ffn_kernel(x_ref, sh_ref, sc_ref, g2_ref, ng_ref, wg_ref, wu_ref, wd_ref, fg_ref, o_ref, h_ref, acc_ref,
                *, final):
    j = pl.program_id(2)

    @pl.when(j == 0)
    def _():
        y = _rms(x_ref[0], ng_ref[...])
        h_ref[...] = (y * (1.0 + sc_ref[0]) + sh_ref[0]).astype(BF16)
        acc_ref[...] = jnp.zeros_like(acc_ref)

    h = h_ref[...]
    act = (_silu(_dot(h, wg_ref[...])) * _dot(h, wu_ref[...])).astype(BF16)
    acc_ref[...] += _dot(act, wd_ref[...])

    @pl.when(j == pl.num_programs(2) - 1)
    def _():
        y = x_ref[0] + g2_ref[0] * acc_ref[...]
        if final:
            y = _rms(y, fg_ref[...])
        o_ref[0] = y


def _ffn(x, sh, sc, g2, ng, w_gu, w_down, fg, final):
    b, l, _ = x.shape
    tm = min(l, 1024)
    tf = 256
    nf = D_FF // tf
    xs = pl.BlockSpec((1, tm, D_MODEL), lambda bi, i, j: (bi, i, 0))
    mod = pl.BlockSpec((1, 1, D_MODEL), lambda bi, i, j: (bi, 0, 0))
    row = pl.BlockSpec((1, D_MODEL), lambda bi, i, j: (0, 0))
    return pl.pallas_call(
        functools.partial(_ffn_kernel, final=final),
        grid=(b, l // tm, nf),
        in_specs=[xs, mod, mod, mod, row,
                  pl.BlockSpec((D_MODEL, tf), lambda bi, i, j: (0, j)),
                  pl.BlockSpec((D_MODEL, tf), lambda bi, i, j: (0, nf + j)),
                  pl.BlockSpec((tf, D_MODEL), lambda bi, i, j: (j, 0)),
                  row],
        out_specs=xs,
        out_shape=jax.ShapeDtypeStruct(x.shape, F32),
        scratch_shapes=[pltpu.VMEM((tm, D_MODEL), BF16), pltpu.VMEM((tm, D_MODEL), F32)],
        compiler_params=_cp("parallel", "parallel", "arbitrary"),
        name="ffn",
    )(x, sh, sc, g2, ng, w_gu, w_gu, w_down, fg)


def _rope_tables(seq):
    t = np.arange(seq)
    inv = ROPE_THETA ** (-np.arange(0, ROPE_AXIS_DIM, 2, dtype=np.float64) / ROPE_AXIS_DIM)
    ang_r = (t // GRID_W)[:, None] * inv
    ang_c = (t % GRID_W)[:, None] * inv
    cos = np.concatenate([np.cos(ang_r), np.cos(ang_r), np.cos(ang_c), np.cos(ang_c)], axis=1)
    sin = np.concatenate([-np.sin(ang_r), np.sin(ang_r), -np.sin(ang_c), np.sin(ang_c)], axis=1)
    return (jnp.asarray(np.tile(cos, (1, 2)), F32), jnp.asarray(np.tile(sin, (1, 2)), F32))


def _hy_feats(seq):
    t = np.linspace(0.0, 1.0, seq)[:, None]
    w = 2.0 * math.pi * np.arange(seq)[:, None] / seq
    f = np.linspace(1e-4, HY_BANDS - 1, HY_BANDS)
    feats = np.concatenate([t, np.cos(f * w), -np.sin(f * w)], axis=1)
    full = np.concatenate([feats, feats[:1], feats[:0:-1]], axis=0)
    return jnp.asarray(np.pad(full, ((0, 0), (0, LANE - HY_EMB))), F32)


def _hy_abs_deltas():
    lo = math.log(HY_TARGET) / HY_SLOW_DECAY_PCT
    hi = math.log(HY_TARGET) / HY_FAST_DECAY_PCT
    return jnp.asarray(np.abs(np.linspace(lo, hi, HY_WIDTH))[None], F32)


def _fft_tables():
    r_ = FFT_R
    n = r_ * r_
    n2 = np.arange(r_)[:, None, None]
    k1 = np.arange(r_)[None, :, None]
    n1 = np.arange(r_ // 2)[None, None, :]
    th = 2.0 * math.pi * ((k1 * (r_ * n1 + n2)) % n) / n
    fwd = np.concatenate([np.cos(th), -np.sin(th)], axis=1)
    inv = np.transpose(fwd, (0, 2, 1)) / n
    kk = np.arange(r_)
    ph = 2.0 * math.pi * ((kk[:, None] * kk[None, :]) % r_) / r_
    fr, fi = np.cos(ph), -np.sin(ph)
    fs = np.block([[fr, -fi], [fi, fr]])
    fc = np.block([[fr, fi], [-fi, fr]])
    return tuple(jnp.asarray(a, F32) for a in (fwd, inv, fs, fc))


def _dense_dft_tables(seq):
    n = 2 * seq
    kk = np.arange(n)
    ph = 2.0 * math.pi * ((kk[:, None] * kk[None, :]) % n) / n
    ff = np.concatenate([np.cos(ph), -np.sin(ph)], axis=0)
    fi = np.concatenate([np.cos(ph[:seq]), -np.sin(ph[:seq])], axis=1) / n
    return jnp.asarray(ff, F32), jnp.asarray(fi, F32)


def _head_expand():
    e = np.zeros((LANE, 2 * SSD_INNER), np.float32)
    for h in range(2 * SSD_HEADS):
        e[h, h * SSD_HEAD_DIM:(h + 1) * SSD_HEAD_DIM] = 1.0
    return jnp.asarray(e)


def _group_mean():
    g = np.kron(np.eye(LANE // ATTN_HEAD_DIM), np.ones((ATTN_HEAD_DIM, ATTN_HEAD_DIM))) / ATTN_HEAD_DIM
    return jnp.asarray(g, F32)


def _relayout_w_in(w):
    cols = [w[:, OFF_HY:OFF_HY + 3 * HY_WIDTH], w[:, OFF_Q:OFF_Q + ATTN_INNER], w[:, OFF_Z:OFF_Z + SSD_INNER],
            w[:, OFF_XB:OFF_XB + SSD_XB], w[:, OFF_C:OFF_C + SSD_GN],
            w[:, OFF_K:OFF_K + ATTN_KV_INNER], w[:, OFF_V:OFF_V + ATTN_KV_INNER],
            w[:, OFF_DT:OFF_DT + 2 * SSD_HEADS]]
    wr = jnp.concatenate(cols, axis=1)
    return jnp.pad(wr, ((0, 0), (0, NP - wr.shape[1]))).astype(BF16)


def _pad_row(v):
    v = v.reshape(1, -1)
    return jnp.pad(v, ((0, 0), (0, LANE - v.shape[1])))


def _layer_params(l, raw, tables):
    p = {k: v[l] for k, v in raw.items()}
    p["w_in_r"] = _relayout_w_in(p["w_in"])
    p["dt_bias_row"] = _pad_row(p["ssd_dt_bias"])
    p["a_log_row"] = _pad_row(p["ssd_a_log"])
    p["d_row"] = jnp.repeat(p["ssd_d"], SSD_HEAD_DIM)[None]
    p["head_expand"] = tables["head_expand"]
    p["gq"] = jnp.tile(p["q_norm_g"], LANE // ATTN_HEAD_DIM)[None]
    p["gk"] = jnp.tile(p["k_norm_g"], LANE // ATTN_HEAD_DIM)[None]
    p["hy_w1p"] = jnp.pad(p["hy_w1"], ((0, LANE - HY_EMB), (0, 0)))
    p["w_out_b"] = p["w_out"].astype(BF16)
    p["w_gu_b"] = p["w_gu"].astype(BF16)
    p["w_down_b"] = p["w_down"].astype(BF16)
    return p


def _mixers(proj, p, tables, is_ctx, ssd_init):
    b, l, _ = proj.shape
    xbc = _ssd_conv(proj, p["ssd_conv_w"], p["ssd_conv_b"])
    yf, yb, sf, sb = _ssd_scan(xbc, proj, ssd_init[0], ssd_init[1], p)
    rope = tables["rope_ctx"] if is_ctx else tables["rope"]
    q, k, v = _attn_prep(proj, rope[0], rope[1], p["gq"], p["gk"], tables["group_mean"])
    return (yf, yb, sf, sb), (q, k, v)


def _hyena(proj, p, tables, is_ctx):
    vx, x0 = _hy_conv(proj, p["hy_conv_w"], p["hy_conv_b"])
    seq = proj.shape[1]
    bias = p["hy_bias"][None]
    if is_ctx:
        taps = _hy_taps(tables["feats_ctx"], p, tables["abs_deltas"], seq)
        return _hy_ctx(vx, x0, taps, tables["dft_ctx"][0], tables["dft_ctx"][1], bias)
    t_fwd, t_inv, fs, fc = tables["fft"]
    taps = _hy_taps(tables["feats"], p, tables["abs_deltas"], seq)
    kre, kim = _fft2_filter(*_fft1(taps[None], t_fwd, True), fs)
    are, aim = _fft1(vx, t_fwd, False)
    zre, zim = _fft2(are, aim, kre, kim, fs, fc)
    return _ifft1(zre, zim, t_inv, vx, x0, bias)


def _tail(x, proj, ssd, at, hy, mod, p, final_g, final):
    x = _mix_out(x, ssd[0], ssd[1], proj, at, hy, mod[2], p["ssd_norm_g"][None], p["attn_norm_g"][None],
                 p["hy_norm_g"][None], p["w_out_b"])
    return _ffn(x, mod[3], mod[4], mod[5], p["norm2_g"][None], p["w_gu_b"], p["w_down_b"], final_g[None], final)


def _layer(x, xc, mod_rows, p, tables, final_g, last):
    b = x.shape[0]
    mod_x = [mod_rows[:b, i * D_MODEL:(i + 1) * D_MODEL][:, None, :] for i in range(N_MOD)]
    mod_c = [jnp.broadcast_to(mod_rows[b:b + 1, i * D_MODEL:(i + 1) * D_MODEL][:, None, :], (b, 1, D_MODEL))
             for i in range(N_MOD)]
    g1 = p["norm1_g"][None]
    proj = _proj_in(x, mod_x[0], mod_x[1], g1, p["w_in_r"])
    projc = _proj_in(xc, mod_c[0], mod_c[1], g1, p["w_in_r"])
    zeros = jnp.zeros((b, SSD_STATE, SSD_INNER), F32)
    ssd_c, qkv_c = _mixers(projc, p, tables, True, (zeros, zeros))
    ssd_x, qkv_x = _mixers(proj, p, tables, False, (ssd_c[2], ssd_c[3]))
    k_all = jnp.concatenate([qkv_c[1], qkv_x[1]], axis=2)
    v_all = jnp.concatenate([qkv_c[2], qkv_x[2]], axis=2)
    at = _flash(qkv_x[0], k_all, v_all)
    hy = _hyena(proj, p, tables, False)
    x = _tail(x, proj, ssd_x, at, hy, mod_x, p, final_g, last)
    if last:
        return x, None
    at_c = _flash(qkv_c[0], qkv_c[1], qkv_c[2])
    hy_c = _hyena(projc, p, tables, True)
    xc = _tail(xc, projc, ssd_c, at_c, hy_c, mod_c, p, final_g, False)
    return x, xc


def kernel(x, c, ctx, c_ctx, w_mod, b_mod, norm1_g, w_in, ssd_conv_w, ssd_conv_b, ssd_a_log, ssd_dt_bias, ssd_d, ssd_norm_g, q_norm_g, k_norm_g, attn_norm_g, hy_conv_w, hy_conv_b, hy_w1, hy_b1, hy_freq, hy_w2, hy_b2, hy_w3, hy_bias, hy_norm_g, w_out, norm2_g, w_gu, w_down, final_g):
    b, seq, _ = x.shape
    ctx_len = ctx.shape[1]
    depth = w_mod.shape[0]
    assert 2 * seq == FFT_R * FFT_R and b + 1 <= SUBLANE
    raw = dict(w_in=w_in, ssd_conv_w=ssd_conv_w, ssd_conv_b=ssd_conv_b, ssd_a_log=ssd_a_log,
               ssd_dt_bias=ssd_dt_bias, ssd_d=ssd_d, ssd_norm_g=ssd_norm_g, q_norm_g=q_norm_g,
               k_norm_g=k_norm_g, attn_norm_g=attn_norm_g, hy_conv_w=hy_conv_w, hy_conv_b=hy_conv_b,
               hy_w1=hy_w1, hy_b1=hy_b1, hy_freq=hy_freq, hy_w2=hy_w2, hy_b2=hy_b2, hy_w3=hy_w3,
               hy_bias=hy_bias, hy_norm_g=hy_norm_g, w_out=w_out, norm2_g=norm2_g, w_gu=w_gu, w_down=w_down,
               norm1_g=norm1_g)
    ones = jnp.ones((ctx_len, LANE), F32)
    tables = dict(rope=_rope_tables(seq), rope_ctx=(ones, jnp.zeros_like(ones)),
                  feats=_hy_feats(seq), feats_ctx=_hy_feats(ctx_len), abs_deltas=_hy_abs_deltas(),
                  fft=_fft_tables(), dft_ctx=_dense_dft_tables(ctx_len),
                  head_expand=_head_expand(), group_mean=_group_mean())
    c_rows = jnp.concatenate([c, c_ctx[None], jnp.zeros((SUBLANE - b - 1, D_MODEL), F32)], axis=0)
    xc = ctx
    for l in range(depth):
        p = _layer_params(l, raw, tables)
        mod_rows = _mod_call(c_rows, w_mod[l], b_mod[l][None])
        x, xc = _layer(x, xc, mod_rows, p, tables, final_g, l == depth - 1)
    return x
```

```python
import functools
import math

import ml_dtypes
import numpy as np
import jax
import jax.numpy as jnp
from jax import lax
from jax.experimental import pallas as pl
from jax.experimental.pallas import tpu as pltpu

F32 = jnp.float32
BF16 = jnp.bfloat16
HI = lax.Precision.HIGHEST

D_MODEL = 1024
GRID_W = 64
EPS = 1e-6
SSD_HEADS = 8
SSD_HEAD_DIM = 64
SSD_INNER = 512
SSD_STATE = 128
SSD_CHUNK = 128
SSD_GN = 256
SSD_XB = SSD_INNER + SSD_GN
ATTN_HEADS = 8
ATTN_KV_HEADS = 2
ATTN_HEAD_DIM = 64
ATTN_INNER = 512
ATTN_KV_INNER = 128
ATTN_SCALE = ATTN_HEAD_DIM ** -0.5
LOG2E = math.log2(math.e)
ROPE_THETA = 10000.0
ROPE_AXIS_DIM = ATTN_HEAD_DIM // 2
HY_WIDTH = 512
HY_BANDS = 16
HY_EMB = 1 + 2 * HY_BANDS
HY_HIDDEN = 64
HY_FAST_DECAY_PCT = 0.3
HY_SLOW_DECAY_PCT = 1.5
HY_TARGET = 1e-2
D_MIX = 1536
D_FF = 2816
N_MOD = 6

OFF_K = 0
OFF_V = OFF_K + ATTN_KV_INNER
OFF_XB = OFF_V + ATTN_KV_INNER
OFF_DT = OFF_XB + SSD_XB
OFF_C = OFF_DT + 2 * SSD_HEADS
OFF_Q = OFF_C + SSD_GN
OFF_Z = OFF_Q + ATTN_INNER
OFF_HY = OFF_Z + SSD_INNER

P_HY = 0
P_Q = 1536
P_Z = 2048
P_X = 2560
P_B = 3072
P_C = 3328
P_K = 3584
P_V = 3712
P_DT = 3840
NP = 4096

LANE = 128
SUBLANE = 8
VMEM_LIMIT = 48 * 1024 * 1024

FFT_R = 128
FFT_G = 8


def _cp(*sem):
    return pltpu.CompilerParams(dimension_semantics=sem, vmem_limit_bytes=VMEM_LIMIT)


def _silu(x):
    return x * (1.0 / (1.0 + jnp.exp(-x)))


def _softplus(x):
    return jnp.maximum(x, 0.0) + jnp.log(1.0 + jnp.exp(-jnp.abs(x)))


def _rms(x, g):
    return x * lax.rsqrt(jnp.mean(x * x, axis=-1, keepdims=True) + EPS) * g


def _dot(a, b):
    return jnp.dot(a, b, preferred_element_type=F32)


def _dot_hi(a, b):
    return jnp.dot(a, b, precision=HI, preferred_element_type=F32)


def _split_bf16(x):
    hi = x.astype(BF16)
    return hi, (x - hi.astype(F32)).astype(BF16)


def _dot3_table(t_hl, x):
    m = t_hl.shape[0] // 2
    x_hi, x_lo = _split_bf16(x)
    a = _dot(t_hl, x_hi)
    return a[:m] + a[m:] + _dot(t_hl[:m], x_lo)


def _mod_kernel(c_ref, w_ref, b_ref, o_ref):
    o_ref[...] = _dot_hi(_silu(c_ref[...]), w_ref[...]) + b_ref[...]


def _mod_call(c_rows, w, b):
    n = w.shape[1]
    tn = 1024
    return pl.pallas_call(
        _mod_kernel,
        grid=(n // tn,),
        in_specs=[pl.BlockSpec((SUBLANE, D_MODEL), lambda j: (0, 0)),
                  pl.BlockSpec((D_MODEL, tn), lambda j: (0, j)),
                  pl.BlockSpec((1, tn), lambda j: (0, j))],
        out_specs=pl.BlockSpec((SUBLANE, tn), lambda j: (0, j)),
        out_shape=jax.ShapeDtypeStruct((SUBLANE, n), F32),
        compiler_params=_cp("parallel"),
        name="adaln_mod",
    )(c_rows, w, b)


def _proj_kernel(x_ref, sh_ref, sc_ref, g_ref, w_ref, o_ref, h_ref):
    @pl.when(pl.program_id(2) == 0)
    def _():
        y = _rms(x_ref[0], g_ref[...])
        h_ref[...] = (y * (1.0 + sc_ref[0]) + sh_ref[0]).astype(BF16)

    o_ref[0] = _dot(h_ref[...], w_ref[...])


def _proj_in(x, sh, sc, g, w):
    b, l, _ = x.shape
    tm = min(l, 1024)
    tn = 512
    return pl.pallas_call(
        _proj_kernel,
        grid=(b, l // tm, NP // tn),
        in_specs=[pl.BlockSpec((1, tm, D_MODEL), lambda bi, i, j: (bi, i, 0)),
                  pl.BlockSpec((1, 1, D_MODEL), lambda bi, i, j: (bi, 0, 0)),
                  pl.BlockSpec((1, 1, D_MODEL), lambda bi, i, j: (bi, 0, 0)),
                  pl.BlockSpec((1, D_MODEL), lambda bi, i, j: (0, 0)),
                  pl.BlockSpec((D_MODEL, tn), lambda bi, i, j: (0, j))],
        out_specs=pl.BlockSpec((1, tm, tn), lambda bi, i, j: (bi, i, j)),
        out_shape=jax.ShapeDtypeStruct((b, l, NP), F32),
        scratch_shapes=[pltpu.VMEM((tm, D_MODEL), BF16)],
        compiler_params=_cp("parallel", "parallel", "arbitrary"),
        name="proj_in",
    )(x, sh, sc, g, w)


def _dwconv3(u, prev_row, next_row, w, b):
    tm = u.shape[0]
    ri = lax.broadcasted_iota(jnp.int32, u.shape, 0)
    um = jnp.where(ri == 0, prev_row, pltpu.roll(u, 1, 0))
    up = jnp.where(ri == tm - 1, next_row, pltpu.roll(u, tm - 1, 0))
    return um * w[0:1] + u * w[1:2] + up * w[2:3] + b


def _conv_group(refs, i, n_i):
    u_ref, p_ref, n_ref, w_ref, b_ref = refs
    prev_row = jnp.where(i > 0, p_ref[0, SUBLANE - 1:SUBLANE, :], 0.0)
    next_row = jnp.where(i < n_i - 1, n_ref[0, 0:1, :], 0.0)
    return _dwconv3(u_ref[0], prev_row, next_row, w_ref[...], b_ref[...])


def _ssd_conv_kernel(u_ref, p_ref, n_ref, w_ref, b_ref, o_ref):
    y = _conv_group((u_ref, p_ref, n_ref, w_ref, b_ref), pl.program_id(1), pl.num_programs(1))
    o_ref[0] = _silu(y)


def _hy_conv_kernel(*refs):
    i, n_i = pl.program_id(1), pl.num_programs(1)
    v = _conv_group(refs[0:5], i, n_i)
    x1 = _conv_group(refs[5:10], i, n_i)
    x0 = _conv_group(refs[10:15], i, n_i)
    vx_ref, x0_ref = refs[15], refs[16]
    vx_ref[0] = v * x1
    x0_ref[0] = x0


def _conv_specs(tm, tc, l, col_block, w_block, grid_rank):
    nrb = l // SUBLANE
    per = tm // SUBLANE
    if grid_rank == 3:
        def wrap(f):
            return lambda bi, i, j: f(bi, i, j)
    else:
        def wrap(f):
            return lambda bi, i: f(bi, i, 0)
    return [
        pl.BlockSpec((1, tm, tc), wrap(lambda bi, i, j: (bi, i, col_block + j))),
        pl.BlockSpec((1, SUBLANE, tc), wrap(lambda bi, i, j: (bi, jnp.maximum(i * per - 1, 0), col_block + j))),
        pl.BlockSpec((1, SUBLANE, tc), wrap(lambda bi, i, j: (bi, jnp.minimum((i + 1) * per, nrb - 1), col_block + j))),
        pl.BlockSpec((SUBLANE, tc), wrap(lambda bi, i, j: (0, w_block + j))),
        pl.BlockSpec((1, tc), wrap(lambda bi, i, j: (0, w_block + j))),
    ]


def _pad_taps(w):
    return jnp.pad(w, ((0, SUBLANE - w.shape[0]), (0, 0)))


def _ssd_conv(proj, conv_w, conv_b):
    b, l, _ = proj.shape
    tm = min(l, 1024)
    tc = 512
    specs = _conv_specs(tm, tc, l, P_X // tc, 0, 3)
    return pl.pallas_call(
        _ssd_conv_kernel,
        grid=(b, l // tm, 2),
        in_specs=specs,
        out_specs=pl.BlockSpec((1, tm, tc), lambda bi, i, j: (bi, i, j)),
        out_shape=jax.ShapeDtypeStruct((b, l, 2 * tc), F32),
        compiler_params=_cp("parallel", "parallel", "parallel"),
        name="ssd_conv",
    )(proj, proj, proj, _pad_taps(conv_w), conv_b[None])


def _hy_conv(proj, conv_w, conv_b):
    b, l, _ = proj.shape
    tm = min(l, 1024)
    tc = HY_WIDTH
    wp = _pad_taps(conv_w)
    bp = conv_b[None]
    specs, args = [], []
    for grp in range(3):
        specs += _conv_specs(tm, tc, l, P_HY // tc + grp, grp, 2)
        args += [proj, proj, proj, wp, bp]
    out_spec = pl.BlockSpec((1, tm, tc), lambda bi, i: (bi, i, 0))
    shp = jax.ShapeDtypeStruct((b, l, tc), F32)
    return pl.pallas_call(
        _hy_conv_kernel,
        grid=(b, l // tm),
        in_specs=specs,
        out_specs=[out_spec, out_spec],
        out_shape=[shp, shp],
        compiler_params=_cp("parallel", "parallel"),
        name="hy_conv",
    )(*args)


def _ssd_chunk(xbc, dt_raw, a, bias, d_row, e_d, st, lane0, fwd):
    q = SSD_CHUNK
    xs = xbc[:, :SSD_INNER]
    bm = xbc[:, SSD_INNER:SSD_XB]
    cm = xbc[:, SSD_XB:]
    ri = lax.broadcasted_iota(jnp.int32, (q, q), 0)
    ci = lax.broadcasted_iota(jnp.int32, (q, q), 1)
    dt = _softplus(dt_raw + bias)
    adt = dt * a
    cs = _dot_hi((ci <= ri).astype(F32), adt)
    key = cs if fwd else cs - adt
    key_t = key.T
    key_e = _dot_hi(key, e_d)
    dt_e = _dot_hi(dt, e_d)
    tot_e = _dot_hi(cs, e_d)[q - 1:q, :]
    if fwd:
        w_state = dt_e * jnp.exp(tot_e - key_e)
        e_off = jnp.exp(key_e)
        mask = ci <= ri
    else:
        w_state = dt_e * jnp.exp(key_e)
        e_off = jnp.exp(tot_e - key_e)
        mask = ci >= ri
    x_dt = (xs * dt_e).astype(BF16)
    x_w = (xs * w_state).astype(BF16)
    st_b = st.astype(BF16)
    y_parts, st_parts, off_parts = [], [], []
    hg = SSD_HEADS // 2
    for g in range(2):
        bg = bm[:, g * SSD_STATE:(g + 1) * SSD_STATE]
        cg = cm[:, g * SSD_STATE:(g + 1) * SSD_STATE].astype(BF16)
        bg_t = bg.T.astype(BF16)
        gmat = _dot(cg, bg_t)
        gs = slice(g * hg * SSD_HEAD_DIM, (g + 1) * hg * SSD_HEAD_DIM)
        off_parts.append(_dot(cg, st_b[:, gs]))
        st_parts.append(_dot(bg_t, x_w[:, gs]))
        for hh in range(hg):
            h = g * hg + hh
            col = key[:, lane0 + h:lane0 + h + 1]
            row = key_t[lane0 + h:lane0 + h + 1, :]
            diff = (col - row) if fwd else (row - col)
            lm = jnp.exp(jnp.where(mask, diff, -1e30))
            s = (gmat * lm).astype(BF16)
            y_parts.append(_dot(s, x_dt[:, h * SSD_HEAD_DIM:(h + 1) * SSD_HEAD_DIM]))
    y = jnp.concatenate(y_parts, axis=1) + jnp.concatenate(off_parts, axis=1) * e_off
    if fwd:
        y = y + d_row * xs
    st_new = st * jnp.exp(tot_e) + jnp.concatenate(st_parts, axis=1)
    return y, st_new


def _ssd_kernel(xf_ref, xb_ref, dtf_ref, dtb_ref, bias_ref, alog_ref, d_ref, e_ref, sf0_ref, sb0_ref,
                yf_ref, yb_ref, sf_ref, sb_ref, stf, stb):
    c = pl.program_id(1)

    @pl.when(c == 0)
    def _():
        stf[...] = sf0_ref[0]
        stb[...] = sb0_ref[0]

    a = -jnp.exp(alog_ref[...])
    bias = bias_ref[...]
    yf, sf = _ssd_chunk(xf_ref[0], dtf_ref[0], a, bias, d_ref[...], e_ref[:, :SSD_INNER], stf[...], 0, True)
    yb, sb = _ssd_chunk(xb_ref[0], dtb_ref[0], a, bias, d_ref[...], e_ref[:, SSD_INNER:], stb[...],
                        SSD_HEADS, False)
    yf_ref[0] = yf
    yb_ref[0] = yb
    stf[...] = sf
    stb[...] = sb
    sf_ref[0] = sf
    sb_ref[0] = sb


def _ssd_scan(xbc, proj, sf0, sb0, p):
    b, l, _ = xbc.shape
    nc = l // SSD_CHUNK
    q = SSD_CHUNK
    dtb = P_DT // LANE
    st_spec = pl.BlockSpec((1, SSD_STATE, SSD_INNER), lambda bi, c: (bi, 0, 0))
    y_shape = jax.ShapeDtypeStruct((b, l, SSD_INNER), F32)
    st_shape = jax.ShapeDtypeStruct((b, SSD_STATE, SSD_INNER), F32)
    row = lambda n: pl.BlockSpec((1, n), lambda bi, c: (0, 0))
    return pl.pallas_call(
        _ssd_kernel,
        grid=(b, nc),
        in_specs=[pl.BlockSpec((1, q, 2 * SSD_INNER), lambda bi, c: (bi, c, 0)),
                  pl.BlockSpec((1, q, 2 * SSD_INNER), lambda bi, c: (bi, nc - 1 - c, 0)),
                  pl.BlockSpec((1, q, LANE), lambda bi, c: (bi, c, dtb)),
                  pl.BlockSpec((1, q, LANE), lambda bi, c: (bi, nc - 1 - c, dtb)),
                  row(LANE), row(LANE), row(SSD_INNER),
                  pl.BlockSpec((LANE, 2 * SSD_INNER), lambda bi, c: (0, 0)),
                  st_spec, st_spec],
        out_specs=[pl.BlockSpec((1, q, SSD_INNER), lambda bi, c: (bi, c, 0)),
                   pl.BlockSpec((1, q, SSD_INNER), lambda bi, c: (bi, nc - 1 - c, 0)),
                   st_spec, st_spec],
        out_shape=[y_shape, y_shape, st_shape, st_shape],
        scratch_shapes=[pltpu.VMEM((SSD_STATE, SSD_INNER), F32), pltpu.VMEM((SSD_STATE, SSD_INNER), F32)],
        compiler_params=_cp("parallel", "arbitrary"),
        name="ssd_scan",
    )(xbc, xbc, proj, proj, p["dt_bias_row"], p["a_log_row"], p["d_row"], p["head_expand"], sf0, sb0)


def _attn_prep_kernel(q_ref, k_ref, v_ref, cos_ref, sin_ref, gq_ref, gk_ref, gm_ref, qo_ref, ko_ref, vo_ref):
    cos = cos_ref[...]
    sin = sin_ref[...]
    gm = gm_ref[...]
    lane = lax.broadcasted_iota(jnp.int32, cos.shape, 1)
    first = jnp.bitwise_and(lane, 31) < 16
    hd = ATTN_HEAD_DIM

    def norm_rope(t, g):
        ms = _dot_hi(t * t, gm)
        y = t * lax.rsqrt(ms + EPS) * g
        partner = jnp.where(first, pltpu.roll(y, LANE - 16, 1), pltpu.roll(y, 16, 1))
        return y * cos + partner * sin

    for s in range(ATTN_INNER // LANE):
        qs = norm_rope(q_ref[0, :, s * LANE:(s + 1) * LANE], gq_ref[...]) * (ATTN_SCALE * LOG2E)
        qo_ref[0, 2 * s] = qs[:, :hd].astype(BF16)
        qo_ref[0, 2 * s + 1] = qs[:, hd:].astype(BF16)
    ks = norm_rope(k_ref[0], gk_ref[...])
    ko_ref[0, 0] = ks[:, :hd].astype(BF16)
    ko_ref[0, 1] = ks[:, hd:].astype(BF16)
    vv = v_ref[0]
    tail = jnp.where(lane == hd, 1.0, 0.0)
    vo_ref[0, 0] = jnp.where(lane < hd, vv, tail).astype(BF16)
    vo_ref[0, 1] = jnp.where(lane < hd, pltpu.roll(vv, hd, 1), tail).astype(BF16)


def _attn_prep(proj, cos, sin, gq, gk, gm):
    b, l, _ = proj.shape
    tm = min(l, 1024)
    hd = ATTN_HEAD_DIM
    const = lambda r, c: pl.BlockSpec((r, c), lambda bi, i: (0, 0))
    return pl.pallas_call(
        _attn_prep_kernel,
        grid=(b, l // tm),
        in_specs=[pl.BlockSpec((1, tm, ATTN_INNER), lambda bi, i: (bi, i, P_Q // ATTN_INNER)),
                  pl.BlockSpec((1, tm, LANE), lambda bi, i: (bi, i, P_K // LANE)),
                  pl.BlockSpec((1, tm, LANE), lambda bi, i: (bi, i, P_V // LANE)),
                  pl.BlockSpec((tm, LANE), lambda bi, i: (i, 0)),
                  pl.BlockSpec((tm, LANE), lambda bi, i: (i, 0)),
                  const(1, LANE), const(1, LANE), const(LANE, LANE)],
        out_specs=[pl.BlockSpec((1, ATTN_HEADS, tm, hd), lambda bi, i: (bi, 0, i, 0)),
                   pl.BlockSpec((1, ATTN_KV_HEADS, tm, hd), lambda bi, i: (bi, 0, i, 0)),
                   pl.BlockSpec((1, ATTN_KV_HEADS, tm, LANE), lambda bi, i: (bi, 0, i, 0))],
        out_shape=[jax.ShapeDtypeStruct((b, ATTN_HEADS, l, hd), BF16),
                   jax.ShapeDtypeStruct((b, ATTN_KV_HEADS, l, hd), BF16),
                   jax.ShapeDtypeStruct((b, ATTN_KV_HEADS, l, LANE), BF16)],
        compiler_params=_cp("parallel", "parallel"),
        name="attn_prep",
    )(proj, proj, proj, cos, sin, gq, gk, gm)


def _flash_kernel(q_ref, k_ref, v_ref, o_ref, *, tk, nk):
    r = ATTN_HEADS // ATTN_KV_HEADS
    tq = q_ref.shape[2]
    hd = ATTN_HEAD_DIM
    q = q_ref[0].reshape(r * tq, hd)

    def body(j, carry):
        m, acc = carry
        start = pl.multiple_of(j * tk, tk)
        ks = k_ref[0, 0, pl.ds(start, tk), :]
        vs = v_ref[0, 0, pl.ds(start, tk), :]
        s = lax.dot_general(q, ks, (((1,), (1,)), ((), ())), preferred_element_type=F32)
        mn = jnp.maximum(m, jnp.max(s, axis=-1, keepdims=True))
        alpha = jnp.exp2(m - mn)
        p = jnp.exp2(s - mn)
        acc = alpha * acc + _dot(p.astype(BF16), vs)
        return mn, acc

    m0 = jnp.full((r * tq, 1), -1e30, F32)
    a0 = jnp.zeros((r * tq, LANE), F32)
    _, acc = lax.fori_loop(0, nk, body, (m0, a0))
    o = acc[:, :hd] * (1.0 / acc[:, hd:hd + 1])
    o_ref[0] = jnp.concatenate([o[h * tq:(h + 1) * tq] for h in range(r)], axis=1)


def _flash(q, k, v):
    b, _, l, hd = q.shape
    lk = k.shape[2]
    r = ATTN_HEADS // ATTN_KV_HEADS
    tq = min(l, 1024)
    tk = 768 if lk % 768 == 0 else 256
    return pl.pallas_call(
        functools.partial(_flash_kernel, tk=tk, nk=lk // tk),
        grid=(b, ATTN_KV_HEADS, l // tq),
        in_specs=[pl.BlockSpec((1, r, tq, hd), lambda bi, g, i: (bi, g, i, 0)),
                  pl.BlockSpec((1, 1, lk, hd), lambda bi, g, i: (bi, g, 0, 0)),
                  pl.BlockSpec((1, 1, lk, LANE), lambda bi, g, i: (bi, g, 0, 0))],
        out_specs=pl.BlockSpec((1, tq, r * hd), lambda bi, g, i: (bi, i, g)),
        out_shape=jax.ShapeDtypeStruct((b, l, ATTN_INNER), F32),
        compiler_params=_cp("parallel", "parallel", "parallel"),
        name="flash_gqa",
    )(q, k, v)


def _taps_kernel(f_ref, w1_ref, b1_ref, fr_ref, w2_ref, b2_ref, w3_ref, adel_ref, o_ref, *, seq):
    tm = f_ref.shape[0]
    f = f_ref[...]
    fr = fr_ref[...]
    h = jnp.sin(fr * (_dot_hi(f, w1_ref[...]) + b1_ref[...]))
    h = jnp.sin(fr * (_dot_hi(h, w2_ref[...]) + b2_ref[...]))
    y = _dot_hi(h, w3_ref[...]) * jnp.exp(-f[:, 0:1] * adel_ref[...])
    row = pl.program_id(0) * tm + lax.broadcasted_iota(jnp.int32, y.shape, 0)
    o_ref[...] = jnp.where(row == seq, 0.0, y)


def _hy_taps(feats, p, adel, seq):
    n = 2 * seq
    tm = min(seq, 1024)
    nh = seq // tm
    const = lambda r, c: pl.BlockSpec((r, c), lambda i: (0, 0))
    return pl.pallas_call(
        functools.partial(_taps_kernel, seq=seq),
        grid=(n // tm,),
        in_specs=[pl.BlockSpec((tm, LANE), lambda i: (i, 0)),
                  const(LANE, HY_HIDDEN), const(1, HY_HIDDEN), const(1, HY_HIDDEN),
                  const(HY_HIDDEN, HY_HIDDEN), const(1, HY_HIDDEN),
                  pl.BlockSpec((HY_HIDDEN, HY_WIDTH), lambda i: (0, i // nh)),
                  const(1, HY_WIDTH)],
        out_specs=pl.BlockSpec((tm, HY_WIDTH), lambda i: (i, 0)),
        out_shape=jax.ShapeDtypeStruct((n, HY_WIDTH), F32),
        compiler_params=_cp("parallel"),
        name="hy_taps",
    )(feats, p["hy_w1p"], p["hy_b1"][None], p["hy_freq"][None], p["hy_w2"], p["hy_b2"][None], p["hy_w3"], adel)


def _fft1_kernel(x_ref, t_ref, are_ref, aim_ref, *, full):
    g = pl.program_id(2)
    r_ = FFT_R
    half = r_ // 2
    if full:
        k1 = lax.broadcasted_iota(jnp.int32, (2 * r_, LANE), 0)
        sgn = jnp.where(jnp.bitwise_and(k1, 1) == 0, 1.0, -1.0)
    for r in range(FFT_G):
        n2 = g * FFT_G + r
        t = t_ref[r]
        a = _dot3_table(t, x_ref[0, pl.ds(n2, half, stride=r_), :])
        if full:
            a = a + sgn * _dot3_table(t, x_ref[0, pl.ds(half * r_ + n2, half, stride=r_), :])
        are_ref[0, r] = a[:r_]
        aim_ref[0, r] = a[r_:]


def _fft1(x, table, full):
    nb, ln, c = x.shape
    r_ = FFT_R
    spec_o = pl.BlockSpec((1, FFT_G, r_, LANE), lambda bi, j, g: (bi, g, 0, j))
    shp = jax.ShapeDtypeStruct((nb, r_, r_, c), F32)
    return pl.pallas_call(
        functools.partial(_fft1_kernel, full=full),
        grid=(nb, c // LANE, r_ // FFT_G),
        in_specs=[pl.BlockSpec((1, ln, LANE), lambda bi, j, g: (bi, 0, j)),
                  pl.BlockSpec((FFT_G, 4 * r_, r_ // 2), lambda bi, j, g: (g, 0, 0))],
        out_specs=[spec_o, spec_o],
        out_shape=[shp, shp],
        compiler_params=_cp("parallel", "parallel", "arbitrary"),
        name="hy_fft1",
    )(x, table)


def _fft2_filter_kernel(are_ref, aim_ref, fs_ref, kre_ref, kim_ref):
    r_ = FFT_R
    for r in range(SUBLANE):
        a = jnp.concatenate([are_ref[0, :, r, :], aim_ref[0, :, r, :]], axis=0)
        b = _dot3_table(fs_ref[...], a)
        kre_ref[r] = b[:r_]
        kim_ref[r] = b[r_:]


def _fft2_filter(are, aim, fs):
    r_ = FFT_R
    c = are.shape[-1]
    spec_i = pl.BlockSpec((1, r_, SUBLANE, c), lambda kg: (0, 0, kg, 0))
    spec_o = pl.BlockSpec((SUBLANE, r_, c), lambda kg: (kg, 0, 0))
    shp = jax.ShapeDtypeStruct((r_, r_, c), F32)
    return pl.pallas_call(
        _fft2_filter_kernel,
        grid=(r_ // SUBLANE,),
        in_specs=[spec_i, spec_i, pl.BlockSpec((4 * r_, 2 * r_), lambda kg: (0, 0))],
        out_specs=[spec_o, spec_o],
        out_shape=[shp, shp],
        compiler_params=_cp("parallel"),
        name="hy_fft2_filter",
    )(are, aim, fs)


def _fft2_kernel(are_ref, aim_ref, kre_ref, kim_ref, fs_ref, fc_ref, zre_ref, zim_ref):
    r_ = FFT_R
    for r in range(SUBLANE):
        a = jnp.concatenate([are_ref[0, :, r, :], aim_ref[0, :, r, :]], axis=0)
        b = _dot3_table(fs_ref[...], a)
        br, bi = b[:r_], b[r_:]
        kr, ki = kre_ref[r], kim_ref[r]
        y = jnp.concatenate([br * kr - bi * ki, br * ki + bi * kr], axis=0)
        z = _dot3_table(fc_ref[...], y)
        zre_ref[0, :, r, :] = z[:r_]
        zim_ref[0, :, r, :] = z[r_:]


def _fft2(are, aim, kre, kim, fs, fc):
    nb, r_, _, c = are.shape
    spec_a = pl.BlockSpec((1, r_, SUBLANE, c), lambda bi, kg: (bi, 0, kg, 0))
    spec_k = pl.BlockSpec((SUBLANE, r_, c), lambda bi, kg: (kg, 0, 0))
    spec_f = pl.BlockSpec((4 * r_, 2 * r_), lambda bi, kg: (0, 0))
    shp = jax.ShapeDtypeStruct(are.shape, F32)
    return pl.pallas_call(
        _fft2_kernel,
        grid=(nb, r_ // SUBLANE),
        in_specs=[spec_a, spec_a, spec_k, spec_k, spec_f, spec_f],
        out_specs=[spec_a, spec_a],
        out_shape=[shp, shp],
        compiler_params=_cp("parallel", "parallel"),
        name="hy_fft2",
    )(are, aim, kre, kim, fs, fc)


def _ifft1_kernel(zre_ref, zim_ref, t_ref, vx_ref, x0_ref, bias_ref, o_ref):
    g = pl.program_id(2)
    r_ = FFT_R
    half = r_ // 2
    for r in range(FFT_G):
        n2 = g * FFT_G + r
        z = jnp.concatenate([zre_ref[0, r], zim_ref[0, r]], axis=0)
        conv = _dot3_table(t_ref[r], z)
        rows = pl.ds(n2, half, stride=r_)
        o_ref[0, rows, :] = x0_ref[0, rows, :] * (conv + bias_ref[...] * vx_ref[0, rows, :])


def _ifft1(zre, zim, table, vx, x0, bias):
    nb, seq, c = vx.shape
    r_ = FFT_R
    spec_z = pl.BlockSpec((1, FFT_G, r_, LANE), lambda bi, j, g: (bi, g, 0, j))
    spec_x = pl.BlockSpec((1, seq, LANE), lambda bi, j, g: (bi, 0, j))
    return pl.pallas_call(
        _ifft1_kernel,
        grid=(nb, c // LANE, r_ // FFT_G),
        in_specs=[spec_z, spec_z,
                  pl.BlockSpec((FFT_G, r_, 2 * r_), lambda bi, j, g: (g, 0, 0)),
                  spec_x, spec_x,
                  pl.BlockSpec((1, LANE), lambda bi, j, g: (0, j))],
        out_specs=spec_x,
        out_shape=jax.ShapeDtypeStruct((nb, seq, c), F32),
        compiler_params=_cp("parallel", "parallel", "arbitrary"),
        name="hy_ifft1",
    )(zre, zim, table, vx, x0, bias)


def _hy_ctx_kernel(vx_ref, x0_ref, taps_ref, ff_ref, fi_ref, bias_ref, o_ref):
    vx = vx_ref[0]
    seq = vx.shape[0]
    n = 2 * seq
    u = _dot_hi(ff_ref[:, :seq], vx)
    k = _dot_hi(ff_ref[...], taps_ref[...])
    ur, ui = u[:n], u[n:]
    kr, ki = k[:n], k[n:]
    y = jnp.concatenate([ur * kr - ui * ki, ur * ki + ui * kr], axis=0)
    conv = _dot_hi(fi_ref[...], y)
    o_ref[0] = x0_ref[0] * (conv + bias_ref[...] * vx)


def _hy_ctx(vx, x0, taps, ff, fi, bias):
    b, seq, c = vx.shape
    n = 2 * seq
    spec_x = pl.BlockSpec((1, seq, c), lambda bi: (bi, 0, 0))
    const = lambda r, cc: pl.BlockSpec((r, cc), lambda bi: (0, 0))
    return pl.pallas_call(
        _hy_ctx_kernel,
        grid=(b,),
        in_specs=[spec_x, spec_x, const(n, c), const(2 * n, n), const(seq, 2 * n), const(1, c)],
        out_specs=spec_x,
        out_shape=jax.ShapeDtypeStruct((b, seq, c), F32),
        compiler_params=_cp("parallel"),
        name="hy_ctx",
    )(vx, x0, taps, ff, fi, bias)


def _mix_kernel(x_ref, yf_ref, yb_ref, z_ref, at_ref, hy_ref, g1_ref, gs_ref, ga_ref, gh_ref, w_ref, o_ref):
    ys = _rms((yf_ref[0] + yb_ref[0]) * _silu(z_ref[0]), gs_ref[...]).astype(BF16)
    ya = _rms(at_ref[0], ga_ref[...]).astype(BF16)
    yh = _rms(hy_ref[0], gh_ref[...]).astype(BF16)
    r = (_dot(ys, w_ref[0:SSD_INNER, :])
         + _dot(ya, w_ref[SSD_INNER:SSD_INNER + ATTN_INNER, :])
         + _dot(yh, w_ref[SSD_INNER + ATTN_INNER:, :]))
    o_ref[0] = x_ref[0] + g1_ref[0] * r


def _mix_out(x, yf, yb, proj, at, hy, g1, gs, ga, gh, w):
    b, l, _ = x.shape
    tm = min(l, 512)
    c = SSD_INNER
    t512 = lambda col: pl.BlockSpec((1, tm, c), lambda bi, i: (bi, i, col))
    const = lambda r, cc: pl.BlockSpec((r, cc), lambda bi, i: (0, 0))
    xs = pl.BlockSpec((1, tm, D_MODEL), lambda bi, i: (bi, i, 0))
    return pl.pallas_call(
        _mix_kernel,
        grid=(b, l // tm),
        in_specs=[xs, t512(0), t512(0), t512(P_Z // c), t512(0), t512(0),
                  pl.BlockSpec((1, 1, D_MODEL), lambda bi, i: (bi, 0, 0)),
                  const(1, c), const(1, c), const(1, c), const(D_MIX, D_MODEL)],
        out_specs=xs,
        out_shape=jax.ShapeDtypeStruct(x.shape, F32),
        compiler_params=_cp("parallel", "parallel"),
        name="mix_out",
    )(x, yf, yb, proj, at, hy, g1, gs, ga, gh, w)


def _ffn_kernel(x_ref, sh_ref, sc_ref, g2_ref, ng_ref, wg_ref, wu_ref, wd_ref, fg_ref, o_ref, h_ref, acc_ref,
                *, final):
    j = pl.program_id(2)

    @pl.when(j == 0)
    def _():
        y = _rms(x_ref[0], ng_ref[...])
        h_ref[...] = (y * (1.0 + sc_ref[0]) + sh_ref[0]).astype(BF16)
        acc_ref[...] = jnp.zeros_like(acc_ref)

    h = h_ref[...]
    act = (_silu(_dot(h, wg_ref[...])) * _dot(h, wu_ref[...])).astype(BF16)
    acc_ref[...] += _dot(act, wd_ref[...])

    @pl.when(j == pl.num_programs(2) - 1)
    def _():
        y = x_ref[0] + g2_ref[0] * acc_ref[...]
        if final:
            y = _rms(y, fg_ref[...])
        o_ref[0] = y


def _ffn(x, sh, sc, g2, ng, w_gu, w_down, fg, final):
    b, l, _ = x.shape
    tm = min(l, 1024)
    tf = 256
    nf = D_FF // tf
    xs = pl.BlockSpec((1, tm, D_MODEL), lambda bi, i, j: (bi, i, 0))
    mod = pl.BlockSpec((1, 1, D_MODEL), lambda bi, i, j: (bi, 0, 0))
    row = pl.BlockSpec((1, D_MODEL), lambda bi, i, j: (0, 0))
    return pl.pallas_call(
        functools.partial(_ffn_kernel, final=final),
        grid=(b, l // tm, nf),
        in_specs=[xs, mod, mod, mod, row,
                  pl.BlockSpec((D_MODEL, tf), lambda bi, i, j: (0, j)),
                  pl.BlockSpec((D_MODEL, tf), lambda bi, i, j: (0, nf + j)),
                  pl.BlockSpec((tf, D_MODEL), lambda bi, i, j: (j, 0)),
                  row],
        out_specs=xs,
        out_shape=jax.ShapeDtypeStruct(x.shape, F32),
        scratch_shapes=[pltpu.VMEM((tm, D_MODEL), BF16), pltpu.VMEM((tm, D_MODEL), F32)],
        compiler_params=_cp("parallel", "parallel", "arbitrary"),
        name="ffn",
    )(x, sh, sc, g2, ng, w_gu, w_gu, w_down, fg)


def _rope_tables(seq):
    t = np.arange(seq)
    inv = ROPE_THETA ** (-np.arange(0, ROPE_AXIS_DIM, 2, dtype=np.float64) / ROPE_AXIS_DIM)
    ang_r = (t // GRID_W)[:, None] * inv
    ang_c = (t % GRID_W)[:, None] * inv
    cos = np.concatenate([np.cos(ang_r), np.cos(ang_r), np.cos(ang_c), np.cos(ang_c)], axis=1)
    sin = np.concatenate([-np.sin(ang_r), np.sin(ang_r), -np.sin(ang_c), np.sin(ang_c)], axis=1)
    return (jnp.asarray(np.tile(cos, (1, 2)), F32), jnp.asarray(np.tile(sin, (1, 2)), F32))


def _hy_feats(seq):
    t = np.linspace(0.0, 1.0, seq)[:, None]
    w = 2.0 * math.pi * np.arange(seq)[:, None] / seq
    f = np.linspace(1e-4, HY_BANDS - 1, HY_BANDS)
    feats = np.concatenate([t, np.cos(f * w), -np.sin(f * w)], axis=1)
    full = np.concatenate([feats, feats[:1], feats[:0:-1]], axis=0)
    return jnp.asarray(np.pad(full, ((0, 0), (0, LANE - HY_EMB))), F32)


def _hy_abs_deltas():
    lo = math.log(HY_TARGET) / HY_SLOW_DECAY_PCT
    hi = math.log(HY_TARGET) / HY_FAST_DECAY_PCT
    return jnp.asarray(np.abs(np.linspace(lo, hi, HY_WIDTH))[None], F32)


def _fft_tables():
    r_ = FFT_R
    n = r_ * r_
    n2 = np.arange(r_)[:, None, None]
    k1 = np.arange(r_)[None, :, None]
    n1 = np.arange(r_ // 2)[None, None, :]
    th = 2.0 * math.pi * ((k1 * (r_ * n1 + n2)) % n) / n
    fwd = np.concatenate([np.cos(th), -np.sin(th)], axis=1)
    inv = np.transpose(fwd, (0, 2, 1)) / n
    kk = np.arange(r_)
    ph = 2.0 * math.pi * ((kk[:, None] * kk[None, :]) % r_) / r_
    fr, fi = np.cos(ph), -np.sin(ph)
    fs = np.block([[fr, -fi], [fi, fr]])
    fc = np.block([[fr, fi], [-fi, fr]])
    return tuple(_hi_lo_rows(a) for a in (fwd, inv, fs, fc))


def _hi_lo_rows(t):
    hi = t.astype(np.float32).astype(ml_dtypes.bfloat16)
    lo = (t - hi.astype(np.float64)).astype(np.float32).astype(ml_dtypes.bfloat16)
    return jnp.asarray(np.concatenate([hi, lo], axis=-2))


def _dense_dft_tables(seq):
    n = 2 * seq
    kk = np.arange(n)
    ph = 2.0 * math.pi * ((kk[:, None] * kk[None, :]) % n) / n
    ff = np.concatenate([np.cos(ph), -np.sin(ph)], axis=0)
    fi = np.concatenate([np.cos(ph[:seq]), -np.sin(ph[:seq])], axis=1) / n
    return jnp.asarray(ff, F32), jnp.asarray(fi, F32)


def _head_expand():
    e = np.zeros((LANE, 2 * SSD_INNER), np.float32)
    for h in range(2 * SSD_HEADS):
        e[h, h * SSD_HEAD_DIM:(h + 1) * SSD_HEAD_DIM] = 1.0
    return jnp.asarray(e)


def _group_mean():
    g = np.kron(np.eye(LANE // ATTN_HEAD_DIM), np.ones((ATTN_HEAD_DIM, ATTN_HEAD_DIM))) / ATTN_HEAD_DIM
    return jnp.asarray(g, F32)


def _relayout_w_in(w):
    cols = [w[:, OFF_HY:OFF_HY + 3 * HY_WIDTH], w[:, OFF_Q:OFF_Q + ATTN_INNER], w[:, OFF_Z:OFF_Z + SSD_INNER],
            w[:, OFF_XB:OFF_XB + SSD_XB], w[:, OFF_C:OFF_C + SSD_GN],
            w[:, OFF_K:OFF_K + ATTN_KV_INNER], w[:, OFF_V:OFF_V + ATTN_KV_INNER],
            w[:, OFF_DT:OFF_DT + 2 * SSD_HEADS]]
    wr = jnp.concatenate(cols, axis=1)
    return jnp.pad(wr, ((0, 0), (0, NP - wr.shape[1]))).astype(BF16)


def _pad_row(v):
    v = v.reshape(1, -1)
    return jnp.pad(v, ((0, 0), (0, LANE - v.shape[1])))


def _layer_params(l, raw, tables):
    p = {k: v[l] for k, v in raw.items()}
    p["w_in_r"] = _relayout_w_in(p["w_in"])
    p["dt_bias_row"] = _pad_row(p["ssd_dt_bias"])
    p["a_log_row"] = _pad_row(p["ssd_a_log"])
    p["d_row"] = jnp.repeat(p["ssd_d"], SSD_HEAD_DIM)[None]
    p["head_expand"] = tables["head_expand"]
    p["gq"] = jnp.tile(p["q_norm_g"], LANE // ATTN_HEAD_DIM)[None]
    p["gk"] = jnp.tile(p["k_norm_g"], LANE // ATTN_HEAD_DIM)[None]
    p["hy_w1p"] = jnp.pad(p["hy_w1"], ((0, LANE - HY_EMB), (0, 0)))
    p["w_out_b"] = p["w_out"].astype(BF16)
    p["w_gu_b"] = p["w_gu"].astype(BF16)
    p["w_down_b"] = p["w_down"].astype(BF16)
    return p


def _mixers(proj, p, tables, is_ctx, ssd_init):
    b, l, _ = proj.shape
    xbc = _ssd_conv(proj, p["ssd_conv_w"], p["ssd_conv_b"])
    yf, yb, sf, sb = _ssd_scan(xbc, proj, ssd_init[0], ssd_init[1], p)
    rope = tables["rope_ctx"] if is_ctx else tables["rope"]
    q, k, v = _attn_prep(proj, rope[0], rope[1], p["gq"], p["gk"], tables["group_mean"])
    return (yf, yb, sf, sb), (q, k, v)


def _hyena(proj, p, tables, is_ctx):
    vx, x0 = _hy_conv(proj, p["hy_conv_w"], p["hy_conv_b"])
    seq = proj.shape[1]
    bias = p["hy_bias"][None]
    if is_ctx:
        taps = _hy_taps(tables["feats_ctx"], p, tables["abs_deltas"], seq)
        return _hy_ctx(vx, x0, taps, tables["dft_ctx"][0], tables["dft_ctx"][1], bias)
    t_fwd, t_inv, fs, fc = tables["fft"]
    taps = _hy_taps(tables["feats"], p, tables["abs_deltas"], seq)
    kre, kim = _fft2_filter(*_fft1(taps[None], t_fwd, True), fs)
    are, aim = _fft1(vx, t_fwd, False)
    zre, zim = _fft2(are, aim, kre, kim, fs, fc)
    return _ifft1(zre, zim, t_inv, vx, x0, bias)


def _tail(x, proj, ssd, at, hy, mod, p, final_g, final):
    x = _mix_out(x, ssd[0], ssd[1], proj, at, hy, mod[2], p["ssd_norm_g"][None], p["attn_norm_g"][None],
                 p["hy_norm_g"][None], p["w_out_b"])
    return _ffn(x, mod[3], mod[4], mod[5], p["norm2_g"][None], p["w_gu_b"], p["w_down_b"], final_g[None], final)


def _layer(x, xc, mod_rows, p, tables, final_g, last):
    b = x.shape[0]
    mod_x = [mod_rows[:b, i * D_MODEL:(i + 1) * D_MODEL][:, None, :] for i in range(N_MOD)]
    mod_c = [jnp.broadcast_to(mod_rows[b:b + 1, i * D_MODEL:(i + 1) * D_MODEL][:, None, :], (b, 1, D_MODEL))
             for i in range(N_MOD)]
    g1 = p["norm1_g"][None]
    proj = _proj_in(x, mod_x[0], mod_x[1], g1, p["w_in_r"])
    projc = _proj_in(xc, mod_c[0], mod_c[1], g1, p["w_in_r"])
    zeros = jnp.zeros((b, SSD_STATE, SSD_INNER), F32)
    ssd_c, qkv_c = _mixers(projc, p, tables, True, (zeros, zeros))
    ssd_x, qkv_x = _mixers(proj, p, tables, False, (ssd_c[2], ssd_c[3]))
    k_all = jnp.concatenate([qkv_c[1], qkv_x[1]], axis=2)
    v_all = jnp.concatenate([qkv_c[2], qkv_x[2]], axis=2)
    at = _flash(qkv_x[0], k_all, v_all)
    hy = _hyena(proj, p, tables, False)
    x = _tail(x, proj, ssd_x, at, hy, mod_x, p, final_g, last)
    if last:
        return x, None
    at_c = _flash(qkv_c[0], qkv_c[1], qkv_c[2])
    hy_c = _hyena(projc, p, tables, True)
    xc = _tail(xc, projc, ssd_c, at_c, hy_c, mod_c, p, final_g, False)
    return x, xc


def kernel(x, c, ctx, c_ctx, w_mod, b_mod, norm1_g, w_in, ssd_conv_w, ssd_conv_b, ssd_a_log, ssd_dt_bias, ssd_d, ssd_norm_g, q_norm_g, k_norm_g, attn_norm_g, hy_conv_w, hy_conv_b, hy_w1, hy_b1, hy_freq, hy_w2, hy_b2, hy_w3, hy_bias, hy_norm_g, w_out, norm2_g, w_gu, w_down, final_g):
    b, seq, _ = x.shape
    ctx_len = ctx.shape[1]
    depth = w_mod.shape[0]
    assert 2 * seq == FFT_R * FFT_R and b + 1 <= SUBLANE
    raw = dict(w_in=w_in, ssd_conv_w=ssd_conv_w, ssd_conv_b=ssd_conv_b, ssd_a_log=ssd_a_log,
               ssd_dt_bias=ssd_dt_bias, ssd_d=ssd_d, ssd_norm_g=ssd_norm_g, q_norm_g=q_norm_g,
               k_norm_g=k_norm_g, attn_norm_g=attn_norm_g, hy_conv_w=hy_conv_w, hy_conv_b=hy_conv_b,
               hy_w1=hy_w1, hy_b1=hy_b1, hy_freq=hy_freq, hy_w2=hy_w2, hy_b2=hy_b2, hy_w3=hy_w3,
               hy_bias=hy_bias, hy_norm_g=hy_norm_g, w_out=w_out, norm2_g=norm2_g, w_gu=w_gu, w_down=w_down,
               norm1_g=norm1_g)
    ones = jnp.ones((ctx_len, LANE), F32)
    tables = dict(rope=_rope_tables(seq), rope_ctx=(ones, jnp.zeros_like(ones)),
                  feats=_hy_feats(seq), feats_ctx=_hy_feats(ctx_len), abs_deltas=_hy_abs_deltas(),
                  fft=_fft_tables(), dft_ctx=_dense_dft_tables(ctx_len),
                  head_expand=_head_expand(), group_mean=_group_mean())
    c_rows = jnp.concatenate([c, c_ctx[None], jnp.zeros((SUBLANE - b - 1, D_MODEL), F32)], axis=0)
    xc = ctx
    for l in range(depth):
        p = _layer_params(l, raw, tables)
        mod_rows = _mod_call(c_rows, w_mod[l], b_mod[l][None])
        x, xc = _layer(x, xc, mod_rows, p, tables, final_g, l == depth - 1)
    return x
```

```python
import functools
import math

import numpy as np
import jax
import jax.numpy as jnp
from jax import lax
from jax.experimental import pallas as pl
from jax.experimental.pallas import tpu as pltpu

F32 = jnp.float32
BF16 = jnp.bfloat16
HI = lax.Precision.HIGHEST

D_MODEL = 1024
GRID_W = 64
EPS = 1e-6
SSD_HEADS = 8
SSD_HEAD_DIM = 64
SSD_INNER = 512
SSD_STATE = 128
SSD_CHUNK = 128
SSD_GN = 256
SSD_XB = SSD_INNER + SSD_GN
ATTN_HEADS = 8
ATTN_KV_HEADS = 2
ATTN_HEAD_DIM = 64
ATTN_INNER = 512
ATTN_KV_INNER = 128
ATTN_SCALE = ATTN_HEAD_DIM ** -0.5
LOG2E = math.log2(math.e)
ATTN_V_ROWS = 80
ROPE_THETA = 10000.0
ROPE_AXIS_DIM = ATTN_HEAD_DIM // 2
HY_WIDTH = 512
HY_BANDS = 16
HY_EMB = 1 + 2 * HY_BANDS
HY_HIDDEN = 64
HY_FAST_DECAY_PCT = 0.3
HY_SLOW_DECAY_PCT = 1.5
HY_TARGET = 1e-2
D_MIX = 1536
D_FF = 2816
N_MOD = 6

OFF_K = 0
OFF_V = OFF_K + ATTN_KV_INNER
OFF_XB = OFF_V + ATTN_KV_INNER
OFF_DT = OFF_XB + SSD_XB
OFF_C = OFF_DT + 2 * SSD_HEADS
OFF_Q = OFF_C + SSD_GN
OFF_Z = OFF_Q + ATTN_INNER
OFF_HY = OFF_Z + SSD_INNER

P_HY = 0
P_Q = 1536
P_Z = 2048
P_X = 2560
P_B = 3072
P_C = 3328
P_K = 3584
P_V = 3712
P_DT = 3840
NP = 4096

LANE = 128
SUBLANE = 8
VMEM_LIMIT = 48 * 1024 * 1024

FFT_R = 128
FFT_G = 8
FFT_K1 = 72


def _cp(*sem):
    return pltpu.CompilerParams(dimension_semantics=sem, vmem_limit_bytes=VMEM_LIMIT)


def _silu(x):
    return x * (1.0 / (1.0 + jnp.exp(-x)))


def _softplus(x):
    return jnp.maximum(x, 0.0) + jnp.log(1.0 + jnp.exp(-jnp.abs(x)))


def _rms(x, g):
    return x * lax.rsqrt(jnp.mean(x * x, axis=-1, keepdims=True) + EPS) * g


def _dot(a, b):
    return jnp.dot(a, b, preferred_element_type=F32)


def _dot_hi(a, b):
    return jnp.dot(a, b, precision=HI, preferred_element_type=F32)


def _split_bf16(x):
    hi = x.astype(BF16)
    return hi, (x - hi.astype(F32)).astype(BF16)


def _dot3_table(t_hl, x):
    m = t_hl.shape[0] // 2
    x_hi, x_lo = _split_bf16(x)
    a = _dot(t_hl, x_hi)
    return a[:m] + a[m:] + _dot(t_hl[:m], x_lo)


def _dot1_table(t_hl, x):
    return _dot(t_hl[:t_hl.shape[0] // 2], x.astype(BF16))


def _mod_kernel(c_ref, w_ref, b_ref, o_ref):
    o_ref[...] = _dot_hi(_silu(c_ref[...]), w_ref[...]) + b_ref[...]


def _mod_call(c_rows, w, b):
    n = w.shape[1]
    tn = 1024
    return pl.pallas_call(
        _mod_kernel,
        grid=(n // tn,),
        in_specs=[pl.BlockSpec((SUBLANE, D_MODEL), lambda j: (0, 0)),
                  pl.BlockSpec((D_MODEL, tn), lambda j: (0, j)),
                  pl.BlockSpec((1, tn), lambda j: (0, j))],
        out_specs=pl.BlockSpec((SUBLANE, tn), lambda j: (0, j)),
        out_shape=jax.ShapeDtypeStruct((SUBLANE, n), F32),
        compiler_params=_cp("parallel"),
        name="adaln_mod",
    )(c_rows, w, b)


def _proj_kernel(x_ref, sh_ref, sc_ref, g_ref, w_ref, o_ref, h_ref):
    @pl.when(pl.program_id(2) == 0)
    def _():
        y = _rms(x_ref[0], g_ref[...])
        h_ref[...] = (y * (1.0 + sc_ref[0]) + sh_ref[0]).astype(BF16)

    o_ref[0] = _dot(h_ref[...], w_ref[...])


def _proj_in(x, sh, sc, g, w):
    b, l, _ = x.shape
    tm = min(l, 1024)
    tn = 512
    return pl.pallas_call(
        _proj_kernel,
        grid=(b, l // tm, NP // tn),
        in_specs=[pl.BlockSpec((1, tm, D_MODEL), lambda bi, i, j: (bi, i, 0)),
                  pl.BlockSpec((1, 1, D_MODEL), lambda bi, i, j: (bi, 0, 0)),
                  pl.BlockSpec((1, 1, D_MODEL), lambda bi, i, j: (bi, 0, 0)),
                  pl.BlockSpec((1, D_MODEL), lambda bi, i, j: (0, 0)),
                  pl.BlockSpec((D_MODEL, tn), lambda bi, i, j: (0, j))],
        out_specs=pl.BlockSpec((1, tm, tn), lambda bi, i, j: (bi, i, j)),
        out_shape=jax.ShapeDtypeStruct((b, l, NP), F32),
        scratch_shapes=[pltpu.VMEM((tm, D_MODEL), BF16)],
        compiler_params=_cp("parallel", "parallel", "arbitrary"),
        name="proj_in",
    )(x, sh, sc, g, w)


def _dwconv3(u, prev_row, next_row, w, b):
    tm = u.shape[0]
    ri = lax.broadcasted_iota(jnp.int32, u.shape, 0)
    um = jnp.where(ri == 0, prev_row, pltpu.roll(u, 1, 0))
    up = jnp.where(ri == tm - 1, next_row, pltpu.roll(u, tm - 1, 0))
    return um * w[0:1] + u * w[1:2] + up * w[2:3] + b


def _conv_group(refs, i, n_i):
    u_ref, p_ref, n_ref, w_ref, b_ref = refs
    prev_row = jnp.where(i > 0, p_ref[0, SUBLANE - 1:SUBLANE, :], 0.0)
    next_row = jnp.where(i < n_i - 1, n_ref[0, 0:1, :], 0.0)
    return _dwconv3(u_ref[0], prev_row, next_row, w_ref[...], b_ref[...])


def _ssd_conv_kernel(u_ref, p_ref, n_ref, w_ref, b_ref, o_ref):
    y = _conv_group((u_ref, p_ref, n_ref, w_ref, b_ref), pl.program_id(1), pl.num_programs(1))
    o_ref[0] = _silu(y)


def _hy_conv_kernel(*refs):
    i, n_i = pl.program_id(1), pl.num_programs(1)
    v = _conv_group(refs[0:5], i, n_i)
    x1 = _conv_group(refs[5:10], i, n_i)
    x0 = _conv_group(refs[10:15], i, n_i)
    vx_ref, x0_ref = refs[15], refs[16]
    vx_ref[0] = v * x1
    x0_ref[0] = x0


def _conv_specs(tm, tc, l, col_block, w_block, grid_rank):
    nrb = l // SUBLANE
    per = tm // SUBLANE
    if grid_rank == 3:
        def wrap(f):
            return lambda bi, i, j: f(bi, i, j)
    else:
        def wrap(f):
            return lambda bi, i: f(bi, i, 0)
    return [
        pl.BlockSpec((1, tm, tc), wrap(lambda bi, i, j: (bi, i, col_block + j))),
        pl.BlockSpec((1, SUBLANE, tc), wrap(lambda bi, i, j: (bi, jnp.maximum(i * per - 1, 0), col_block + j))),
        pl.BlockSpec((1, SUBLANE, tc), wrap(lambda bi, i, j: (bi, jnp.minimum((i + 1) * per, nrb - 1), col_block + j))),
        pl.BlockSpec((SUBLANE, tc), wrap(lambda bi, i, j: (0, w_block + j))),
        pl.BlockSpec((1, tc), wrap(lambda bi, i, j: (0, w_block + j))),
    ]


def _pad_taps(w):
    return jnp.pad(w, ((0, SUBLANE - w.shape[0]), (0, 0)))


def _ssd_conv(proj, conv_w, conv_b):
    b, l, _ = proj.shape
    tm = min(l, 1024)
    tc = 512
    specs = _conv_specs(tm, tc, l, P_X // tc, 0, 3)
    return pl.pallas_call(
        _ssd_conv_kernel,
        grid=(b, l // tm, 2),
        in_specs=specs,
        out_specs=pl.BlockSpec((1, tm, tc), lambda bi, i, j: (bi, i, j)),
        out_shape=jax.ShapeDtypeStruct((b, l, 2 * tc), F32),
        compiler_params=_cp("parallel", "parallel", "parallel"),
        name="ssd_conv",
    )(proj, proj, proj, _pad_taps(conv_w), conv_b[None])


def _hy_conv(proj, conv_w, conv_b):
    b, l, _ = proj.shape
    tm = min(l, 1024)
    tc = HY_WIDTH
    wp = _pad_taps(conv_w)
    bp = conv_b[None]
    specs, args = [], []
    for grp in range(3):
        specs += _conv_specs(tm, tc, l, P_HY // tc + grp, grp, 2)
        args += [proj, proj, proj, wp, bp]
    out_spec = pl.BlockSpec((1, tm, tc), lambda bi, i: (bi, i, 0))
    shp = jax.ShapeDtypeStruct((b, l, tc), F32)
    return pl.pallas_call(
        _hy_conv_kernel,
        grid=(b, l // tm),
        in_specs=specs,
        out_specs=[out_spec, out_spec],
        out_shape=[shp, shp],
        compiler_params=_cp("parallel", "parallel"),
        name="hy_conv",
    )(*args)


def _split_pieces(x, n):
    pieces, r = [], x
    for k in range(n):
        pc = r.astype(BF16)
        pieces.append(pc)
        if k + 1 < n:
            r = r - pc.astype(F32)
    return pieces


def _select_left(sel, x, n):
    w = x.shape[1]
    r = _dot(sel, jnp.concatenate(_split_pieces(x, n), axis=1))
    return sum(r[:, k * w:(k + 1) * w] for k in range(n))


def _select_right(x, sel, n):
    m = x.shape[0]
    r = _dot(jnp.concatenate(_split_pieces(x, n), axis=0), sel)
    return sum(r[k * m:(k + 1) * m] for k in range(n))


def _ssd_chunk(xbc, dt_raw, a, bias, d_row, e_d, st, lane0, fwd):
    q = SSD_CHUNK
    xs = xbc[:, :SSD_INNER]
    bm = xbc[:, SSD_INNER:SSD_XB]
    cm = xbc[:, SSD_XB:]
    ri = lax.broadcasted_iota(jnp.int32, (q, q), 0)
    ci = lax.broadcasted_iota(jnp.int32, (q, q), 1)
    dt = _softplus(dt_raw + bias)
    adt = dt * a
    cs = _select_left((ci <= ri).astype(BF16), adt, 3)
    tot = cs[q - 1:q, :]
    if fwd:
        key = cs
        w_c = dt * jnp.exp(tot - key)
        e_c = jnp.exp(key)
        mask = ci <= ri
    else:
        key = cs - adt
        w_c = dt * jnp.exp(key)
        e_c = jnp.exp(tot - key)
        mask = ci >= ri
    key_t = key.T
    dt_t = dt.T
    dec_c = jnp.broadcast_to(jnp.exp(tot), (2 * SUBLANE, LANE))
    spread = _select_right(jnp.concatenate([w_c, e_c, dec_c], axis=0), e_d, 2)
    w_e, e_off, decay = spread[:q], spread[q:2 * q], spread[2 * q:2 * q + 1]
    x_b = xs.astype(BF16)
    x_w = (xs * w_e).astype(BF16)
    st_b = st.astype(BF16)
    y_parts, st_parts, off_parts = [], [], []
    hg = SSD_HEADS // 2
    for g in range(2):
        bg = bm[:, g * SSD_STATE:(g + 1) * SSD_STATE]
        cg = cm[:, g * SSD_STATE:(g + 1) * SSD_STATE].astype(BF16)
        bg_t = bg.T.astype(BF16)
        gmat = _dot(cg, bg_t)
        gs = slice(g * hg * SSD_HEAD_DIM, (g + 1) * hg * SSD_HEAD_DIM)
        off_parts.append(_dot(cg, st_b[:, gs]))
        st_parts.append(_dot(bg_t, x_w[:, gs]))
        for hh in range(hg):
            h = g * hg + hh
            col = key[:, lane0 + h:lane0 + h + 1]
            row = key_t[lane0 + h:lane0 + h + 1, :]
            diff = (col - row) if fwd else (row - col)
            lm = jnp.exp(jnp.where(mask, diff, -1e30)) * dt_t[lane0 + h:lane0 + h + 1, :]
            s = (gmat * lm).astype(BF16)
            y_parts.append(_dot(s, x_b[:, h * SSD_HEAD_DIM:(h + 1) * SSD_HEAD_DIM]))
    y = jnp.concatenate(y_parts, axis=1) + jnp.concatenate(off_parts, axis=1) * e_off
    if fwd:
        y = y + d_row * xs
    st_new = st * decay + jnp.concatenate(st_parts, axis=1)
    return y, st_new


def _ssd_kernel(xf_ref, xb_ref, dtf_ref, dtb_ref, bias_ref, alog_ref, d_ref, e_ref, sf0_ref, sb0_ref,
                yf_ref, yb_ref, sf_ref, sb_ref, stf, stb):
    c = pl.program_id(1)

    @pl.when(c == 0)
    def _():
        stf[...] = sf0_ref[0]
        stb[...] = sb0_ref[0]

    a = -jnp.exp(alog_ref[...])
    bias = bias_ref[...]
    yf, sf = _ssd_chunk(xf_ref[0], dtf_ref[0], a, bias, d_ref[...], e_ref[:, :SSD_INNER], stf[...], 0, True)
    yb, sb = _ssd_chunk(xb_ref[0], dtb_ref[0], a, bias, d_ref[...], e_ref[:, SSD_INNER:], stb[...],
                        SSD_HEADS, False)
    yf_ref[0] = yf
    yb_ref[0] = yb
    stf[...] = sf
    stb[...] = sb
    sf_ref[0] = sf
    sb_ref[0] = sb


def _ssd_scan(xbc, proj, sf0, sb0, p):
    b, l, _ = xbc.shape
    nc = l // SSD_CHUNK
    q = SSD_CHUNK
    dtb = P_DT // LANE
    st_spec = pl.BlockSpec((1, SSD_STATE, SSD_INNER), lambda bi, c: (bi, 0, 0))
    y_shape = jax.ShapeDtypeStruct((b, l, SSD_INNER), F32)
    st_shape = jax.ShapeDtypeStruct((b, SSD_STATE, SSD_INNER), F32)
    row = lambda n: pl.BlockSpec((1, n), lambda bi, c: (0, 0))
    return pl.pallas_call(
        _ssd_kernel,
        grid=(b, nc),
        in_specs=[pl.BlockSpec((1, q, 2 * SSD_INNER), lambda bi, c: (bi, c, 0)),
                  pl.BlockSpec((1, q, 2 * SSD_INNER), lambda bi, c: (bi, nc - 1 - c, 0)),
                  pl.BlockSpec((1, q, LANE), lambda bi, c: (bi, c, dtb)),
                  pl.BlockSpec((1, q, LANE), lambda bi, c: (bi, nc - 1 - c, dtb)),
                  row(LANE), row(LANE), row(SSD_INNER),
                  pl.BlockSpec((LANE, 2 * SSD_INNER), lambda bi, c: (0, 0)),
                  st_spec, st_spec],
        out_specs=[pl.BlockSpec((1, q, SSD_INNER), lambda bi, c: (bi, c, 0)),
                   pl.BlockSpec((1, q, SSD_INNER), lambda bi, c: (bi, nc - 1 - c, 0)),
                   st_spec, st_spec],
        out_shape=[y_shape, y_shape, st_shape, st_shape],
        scratch_shapes=[pltpu.VMEM((SSD_STATE, SSD_INNER), F32), pltpu.VMEM((SSD_STATE, SSD_INNER), F32)],
        compiler_params=_cp("parallel", "arbitrary"),
        name="ssd_scan",
    )(xbc, xbc, proj, proj, p["dt_bias_row"], p["a_log_row"], p["d_row"], p["head_expand"], sf0, sb0)


def _attn_prep_kernel(q_ref, k_ref, v_ref, cos_ref, sin_ref, gq_ref, gk_ref, gm_ref, qo_ref, ko_ref, vo_ref):
    cos = cos_ref[...]
    sin = sin_ref[...]
    gm = gm_ref[...]
    lane = lax.broadcasted_iota(jnp.int32, cos.shape, 1)
    first = jnp.bitwise_and(lane, 31) < 16
    hd = ATTN_HEAD_DIM

    def norm_rope(t, g):
        ms = _dot_hi(t * t, gm)
        y = t * lax.rsqrt(ms + EPS) * g
        partner = jnp.where(first, pltpu.roll(y, LANE - 16, 1), pltpu.roll(y, 16, 1))
        return y * cos + partner * sin

    for s in range(ATTN_INNER // LANE):
        qs = norm_rope(q_ref[0, :, s * LANE:(s + 1) * LANE], gq_ref[...]) * (ATTN_SCALE * LOG2E)
        qt = qs.T.astype(BF16)
        qo_ref[0, 2 * s] = qt[:hd]
        qo_ref[0, 2 * s + 1] = qt[hd:]
    ks = norm_rope(k_ref[0], gk_ref[...])
    ko_ref[0, 0] = ks[:, :hd].astype(BF16)
    ko_ref[0, 1] = ks[:, hd:].astype(BF16)
    vt = v_ref[0].T.astype(BF16)
    tm = vt.shape[1]
    pad_rows = lax.broadcasted_iota(jnp.int32, (ATTN_V_ROWS - hd, tm), 0)
    tail = jnp.where(pad_rows == 0, 1.0, 0.0).astype(BF16)
    for g in range(ATTN_KV_HEADS):
        vo_ref[0, g, 0:hd, :] = vt[g * hd:(g + 1) * hd]
        vo_ref[0, g, hd:ATTN_V_ROWS, :] = tail


def _attn_prep(proj, cos, sin, gq, gk, gm):
    b, l, _ = proj.shape
    tm = min(l, 1024)
    hd = ATTN_HEAD_DIM
    const = lambda r, c: pl.BlockSpec((r, c), lambda bi, i: (0, 0))
    return pl.pallas_call(
        _attn_prep_kernel,
        grid=(b, l // tm),
        in_specs=[pl.BlockSpec((1, tm, ATTN_INNER), lambda bi, i: (bi, i, P_Q // ATTN_INNER)),
                  pl.BlockSpec((1, tm, LANE), lambda bi, i: (bi, i, P_K // LANE)),
                  pl.BlockSpec((1, tm, LANE), lambda bi, i: (bi, i, P_V // LANE)),
                  pl.BlockSpec((tm, LANE), lambda bi, i: (i, 0)),
                  pl.BlockSpec((tm, LANE), lambda bi, i: (i, 0)),
                  const(1, LANE), const(1, LANE), const(LANE, LANE)],
        out_specs=[pl.BlockSpec((1, ATTN_HEADS, hd, tm), lambda bi, i: (bi, 0, 0, i)),
                   pl.BlockSpec((1, ATTN_KV_HEADS, tm, hd), lambda bi, i: (bi, 0, i, 0)),
                   pl.BlockSpec((1, ATTN_KV_HEADS, ATTN_V_ROWS, tm), lambda bi, i: (bi, 0, 0, i))],
        out_shape=[jax.ShapeDtypeStruct((b, ATTN_HEADS, hd, l), BF16),
                   jax.ShapeDtypeStruct((b, ATTN_KV_HEADS, l, hd), BF16),
                   jax.ShapeDtypeStruct((b, ATTN_KV_HEADS, ATTN_V_ROWS, l), BF16)],
        compiler_params=_cp("parallel", "parallel"),
        name="attn_prep",
    )(proj, proj, proj, cos, sin, gq, gk, gm)


def _flash_kernel(q_ref, k_ref, v_ref, o_ref, s_ref, m_ref, acc_ref, *, tk, nk):
    r = ATTN_HEADS // ATTN_KV_HEADS
    hd = ATTN_HEAD_DIM

    def scores(j, h, slot):
        start = pl.multiple_of(j * tk, tk)
        s_ref[slot] = _dot(k_ref[0, 0, pl.ds(start, tk), :], q_ref[0, h])

    m_ref[...] = jnp.full(m_ref.shape, -1e30, F32)
    acc_ref[...] = jnp.zeros(acc_ref.shape, F32)
    scores(0, 0, 0)

    def body(j, _):
        start = pl.multiple_of(j * tk, tk)
        vs = v_ref[0, 0, :, pl.ds(start, tk)]
        j_next = jnp.minimum(j + 1, nk - 1)
        for h in range(r):
            if h < r - 1:
                scores(j, h + 1, (h + 1) % 2)
            else:
                scores(j_next, 0, 0)
            s = s_ref[h % 2]
            m = m_ref[h]
            mn = jnp.maximum(m, jnp.max(s, axis=0, keepdims=True))
            alpha = jnp.exp2(m - mn)
            p = jnp.exp2(s - mn).astype(BF16)
            m_ref[h] = mn
            acc_ref[h] = alpha * acc_ref[h] + _dot(vs, p)
        return 0

    lax.fori_loop(0, nk, body, 0)
    outs = []
    for h in range(r):
        acc = acc_ref[h]
        outs.append((acc[:hd] * (1.0 / acc[hd:hd + 1])).T)
    o_ref[0] = jnp.concatenate(outs, axis=1)


def _flash(q, k, v):
    b, _, hd, l = q.shape
    lk = k.shape[2]
    r = ATTN_HEADS // ATTN_KV_HEADS
    assert r % 2 == 0
    tq = min(l, 512)
    tk = 768 if lk % 768 == 0 else 256
    return pl.pallas_call(
        functools.partial(_flash_kernel, tk=tk, nk=lk // tk),
        grid=(b, ATTN_KV_HEADS, l // tq),
        in_specs=[pl.BlockSpec((1, r, hd, tq), lambda bi, g, i: (bi, g, 0, i)),
                  pl.BlockSpec((1, 1, lk, hd), lambda bi, g, i: (bi, g, 0, 0)),
                  pl.BlockSpec((1, 1, ATTN_V_ROWS, lk), lambda bi, g, i: (bi, g, 0, 0))],
        out_specs=pl.BlockSpec((1, tq, r * hd), lambda bi, g, i: (bi, i, g)),
        out_shape=jax.ShapeDtypeStruct((b, l, ATTN_INNER), F32),
        scratch_shapes=[pltpu.VMEM((2, tk, tq), F32), pltpu.VMEM((r, 1, tq), F32),
                        pltpu.VMEM((r, ATTN_V_ROWS, tq), F32)],
        compiler_params=_cp("parallel", "parallel", "parallel"),
        name="flash_gqa",
    )(q, k, v)


def _taps_kernel(f_ref, w1_ref, b1_ref, fr_ref, w2_ref, b2_ref, w3_ref, adel_ref, o_ref, *, seq):
    tm = f_ref.shape[0]
    f = f_ref[...]
    fr = fr_ref[...]
    h = jnp.sin(fr * (_dot_hi(f, w1_ref[...]) + b1_ref[...]))
    h = jnp.sin(fr * (_dot_hi(h, w2_ref[...]) + b2_ref[...]))
    y = _dot_hi(h, w3_ref[...]) * jnp.exp(-f[:, 0:1] * adel_ref[...])
    row = pl.program_id(0) * tm + lax.broadcasted_iota(jnp.int32, y.shape, 0)
    o_ref[...] = jnp.where(row == seq, 0.0, y)


def _hy_taps(feats, p, adel, seq):
    n = 2 * seq
    tm = min(seq, 1024)
    nh = seq // tm
    const = lambda r, c: pl.BlockSpec((r, c), lambda i: (0, 0))
    return pl.pallas_call(
        functools.partial(_taps_kernel, seq=seq),
        grid=(n // tm,),
        in_specs=[pl.BlockSpec((tm, LANE), lambda i: (i, 0)),
                  const(LANE, HY_HIDDEN), const(1, HY_HIDDEN), const(1, HY_HIDDEN),
                  const(HY_HIDDEN, HY_HIDDEN), const(1, HY_HIDDEN),
                  pl.BlockSpec((HY_HIDDEN, HY_WIDTH), lambda i: (0, i // nh)),
                  const(1, HY_WIDTH)],
        out_specs=pl.BlockSpec((tm, HY_WIDTH), lambda i: (i, 0)),
        out_shape=jax.ShapeDtypeStruct((n, HY_WIDTH), F32),
        compiler_params=_cp("parallel"),
        name="hy_taps",
    )(feats, p["hy_w1p"], p["hy_b1"][None], p["hy_freq"][None], p["hy_w2"], p["hy_b2"][None], p["hy_w3"], adel)


def _fft1_kernel(x_ref, t_ref, are_ref, aim_ref, *, full):
    g = pl.program_id(2)
    r_ = FFT_R
    half = r_ // 2
    if full:
        k1 = lax.broadcasted_iota(jnp.int32, (2 * FFT_K1, LANE), 0)
        sgn = jnp.where(jnp.bitwise_and(k1, 1) == 0, 1.0, -1.0)
    for r in range(FFT_G):
        n2 = g * FFT_G + r
        t = t_ref[r]
        if full:
            a = (_dot3_table(t, x_ref[0, pl.ds(n2, half, stride=r_), :])
                 + sgn * _dot3_table(t, x_ref[0, pl.ds(half * r_ + n2, half, stride=r_), :]))
        else:
            a = _dot1_table(t, x_ref[0, pl.ds(n2, half, stride=r_), :])
        are_ref[0, r] = a[:FFT_K1]
        aim_ref[0, r] = a[FFT_K1:]


def _fft1(x, table, full):
    nb, ln, c = x.shape
    r_ = FFT_R
    spec_o = pl.BlockSpec((1, FFT_G, FFT_K1, LANE), lambda bi, j, g: (bi, g, 0, j))
    shp = jax.ShapeDtypeStruct((nb, r_, FFT_K1, c), F32)
    return pl.pallas_call(
        functools.partial(_fft1_kernel, full=full),
        grid=(nb, c // LANE, r_ // FFT_G),
        in_specs=[pl.BlockSpec((1, ln, LANE), lambda bi, j, g: (bi, 0, j)),
                  pl.BlockSpec((FFT_G, 4 * FFT_K1, r_ // 2), lambda bi, j, g: (g, 0, 0))],
        out_specs=[spec_o, spec_o],
        out_shape=[shp, shp],
        compiler_params=_cp("parallel", "parallel", "arbitrary"),
        name="hy_fft1",
    )(x, table)


def _fft2_filter_kernel(are_ref, aim_ref, fs_ref, kre_ref, kim_ref):
    r_ = FFT_R
    for r in range(SUBLANE):
        a = jnp.concatenate([are_ref[0, :, r, :], aim_ref[0, :, r, :]], axis=0)
        b = _dot3_table(fs_ref[...], a)
        kre_ref[r] = b[:r_]
        kim_ref[r] = b[r_:]


def _fft2_filter(are, aim, fs):
    r_ = FFT_R
    c = are.shape[-1]
    spec_i = pl.BlockSpec((1, r_, SUBLANE, c), lambda kg: (0, 0, kg, 0))
    spec_o = pl.BlockSpec((SUBLANE, r_, c), lambda kg: (kg, 0, 0))
    shp = jax.ShapeDtypeStruct((FFT_K1, r_, c), F32)
    return pl.pallas_call(
        _fft2_filter_kernel,
        grid=(FFT_K1 // SUBLANE,),
        in_specs=[spec_i, spec_i, pl.BlockSpec((4 * r_, 2 * r_), lambda kg: (0, 0))],
        out_specs=[spec_o, spec_o],
        out_shape=[shp, shp],
        compiler_params=_cp("parallel"),
        name="hy_fft2_filter",
    )(are, aim, fs)


def _fft2_kernel(are_ref, aim_ref, kre_ref, kim_ref, fs_ref, fc_ref, zre_ref, zim_ref):
    r_ = FFT_R
    for r in range(SUBLANE):
        a = jnp.concatenate([are_ref[0, :, r, :], aim_ref[0, :, r, :]], axis=0)
        b = _dot1_table(fs_ref[...], a)
        br, bi = b[:r_], b[r_:]
        kr, ki = kre_ref[r], kim_ref[r]
        y = jnp.concatenate([br * kr - bi * ki, br * ki + bi * kr], axis=0)
        z = _dot1_table(fc_ref[...], y)
        zre_ref[0, :, r, :] = z[:r_]
        zim_ref[0, :, r, :] = z[r_:]


def _fft2(are, aim, kre, kim, fs, fc):
    nb, r_, _, c = are.shape
    spec_a = pl.BlockSpec((1, r_, SUBLANE, c), lambda bi, kg: (bi, 0, kg, 0))
    spec_k = pl.BlockSpec((SUBLANE, r_, c), lambda bi, kg: (kg, 0, 0))
    spec_f = pl.BlockSpec((4 * r_, 2 * r_), lambda bi, kg: (0, 0))
    shp = jax.ShapeDtypeStruct(are.shape, F32)
    return pl.pallas_call(
        _fft2_kernel,
        grid=(nb, FFT_K1 // SUBLANE),
        in_specs=[spec_a, spec_a, spec_k, spec_k, spec_f, spec_f],
        out_specs=[spec_a, spec_a],
        out_shape=[shp, shp],
        compiler_params=_cp("parallel", "parallel"),
        name="hy_fft2",
    )(are, aim, kre, kim, fs, fc)


def _ifft1_kernel(zre_ref, zim_ref, t_ref, vx_ref, x0_ref, bias_ref, o_ref):
    g = pl.program_id(2)
    r_ = FFT_R
    half = r_ // 2
    for r in range(FFT_G):
        n2 = g * FFT_G + r
        z = jnp.concatenate([zre_ref[0, r], zim_ref[0, r]], axis=0)
        conv = _dot1_table(t_ref[r], z)
        rows = pl.ds(n2, half, stride=r_)
        o_ref[0, rows, :] = x0_ref[0, rows, :] * (conv + bias_ref[...] * vx_ref[0, rows, :])


def _ifft1(zre, zim, table, vx, x0, bias):
    nb, seq, c = vx.shape
    r_ = FFT_R
    spec_z = pl.BlockSpec((1, FFT_G, FFT_K1, LANE), lambda bi, j, g: (bi, g, 0, j))
    spec_x = pl.BlockSpec((1, seq, LANE), lambda bi, j, g: (bi, 0, j))
    return pl.pallas_call(
        _ifft1_kernel,
        grid=(nb, c // LANE, r_ // FFT_G),
        in_specs=[spec_z, spec_z,
                  pl.BlockSpec((FFT_G, r_, 2 * FFT_K1), lambda bi, j, g: (g, 0, 0)),
                  spec_x, spec_x,
                  pl.BlockSpec((1, LANE), lambda bi, j, g: (0, j))],
        out_specs=spec_x,
        out_shape=jax.ShapeDtypeStruct((nb, seq, c), F32),
        compiler_params=_cp("parallel", "parallel", "arbitrary"),
        name="hy_ifft1",
    )(zre, zim, table, vx, x0, bias)


def _hy_ctx_kernel(vx_ref, x0_ref, taps_ref, ff_ref, fi_ref, bias_ref, o_ref):
    vx = vx_ref[0]
    seq = vx.shape[0]
    n = 2 * seq
    u = _dot_hi(ff_ref[:, :seq], vx)
    k = _dot_hi(ff_ref[...], taps_ref[...])
    ur, ui = u[:n], u[n:]
    kr, ki = k[:n], k[n:]
    y = jnp.concatenate([ur * kr - ui * ki, ur * ki + ui * kr], axis=0)
    conv = _dot_hi(fi_ref[...], y)
    o_ref[0] = x0_ref[0] * (conv + bias_ref[...] * vx)


def _hy_ctx(vx, x0, taps, ff, fi, bias):
    b, seq, c = vx.shape
    n = 2 * seq
    spec_x = pl.BlockSpec((1, seq, c), lambda bi: (bi, 0, 0))
    const = lambda r, cc: pl.BlockSpec((r, cc), lambda bi: (0, 0))
    return pl.pallas_call(
        _hy_ctx_kernel,
        grid=(b,),
        in_specs=[spec_x, spec_x, const(n, c), const(2 * n, n), const(seq, 2 * n), const(1, c)],
        out_specs=spec_x,
        out_shape=jax.ShapeDtypeStruct((b, seq, c), F32),
        compiler_params=_cp("parallel"),
        name="hy_ctx",
    )(vx, x0, taps, ff, fi, bias)


def _mix_kernel(x_ref, yf_ref, yb_ref, z_ref, at_ref, hy_ref, g1_ref, gs_ref, ga_ref, gh_ref, w_ref, o_ref):
    ys = _rms((yf_ref[0] + yb_ref[0]) * _silu(z_ref[0]), gs_ref[...]).astype(BF16)
    ya = _rms(at_ref[0], ga_ref[...]).astype(BF16)
    yh = _rms(hy_ref[0], gh_ref[...]).astype(BF16)
    r = (_dot(ys, w_ref[0:SSD_INNER, :])
         + _dot(ya, w_ref[SSD_INNER:SSD_INNER + ATTN_INNER, :])
         + _dot(yh, w_ref[SSD_INNER + ATTN_INNER:, :]))
    o_ref[0] = x_ref[0] + g1_ref[0] * r


def _mix_out(x, yf, yb, proj, at, hy, g1, gs, ga, gh, w):
    b, l, _ = x.shape
    tm = min(l, 512)
    c = SSD_INNER
    t512 = lambda col: pl.BlockSpec((1, tm, c), lambda bi, i: (bi, i, col))
    const = lambda r, cc: pl.BlockSpec((r, cc), lambda bi, i: (0, 0))
    xs = pl.BlockSpec((1, tm, D_MODEL), lambda bi, i: (bi, i, 0))
    return pl.pallas_call(
        _mix_kernel,
        grid=(b, l // tm),
        in_specs=[xs, t512(0), t512(0), t512(P_Z // c), t512(0), t512(0),
                  pl.BlockSpec((1, 1, D_MODEL), lambda bi, i: (bi, 0, 0)),
                  const(1, c), const(1, c), const(1, c), const(D_MIX, D_MODEL)],
        out_specs=xs,
        out_shape=jax.ShapeDtypeStruct(x.shape, F32),
        compiler_params=_cp("parallel", "parallel"),
        name="mix_out",
    )(x, yf, yb, proj, at, hy, g1, gs, ga, gh, w)


def _ffn_kernel(x_ref, sh_ref, sc_ref, g2_ref, ng_ref, wg_ref, wu_ref, wd_ref, fg_ref, o_ref, h_ref, acc_ref,
                *, final):
    j = pl.program_id(2)

    @pl.when(j == 0)
    def _():
        y = _rms(x_ref[0], ng_ref[...])
        h_ref[...] = (y * (1.0 + sc_ref[0]) + sh_ref[0]).astype(BF16)
        acc_ref[...] = jnp.zeros_like(acc_ref)

    h = h_ref[...]
    act = (_silu(_dot(h, wg_ref[...])) * _dot(h, wu_ref[...])).astype(BF16)
    acc_ref[...] += _dot(act, wd_ref[...])

    @pl.when(j == pl.num_programs(2) - 1)
    def _():
        y = x_ref[0] + g2_ref[0] * acc_ref[...]
        if final:
            y = _rms(y, fg_ref[...])
        o_ref[0] = y


def _ffn(x, sh, sc, g2, ng, w_gu, w_down, fg, final):
    b, l, _ = x.shape
    tm = min(l, 1024)
    tf = 256
    nf = D_FF // tf
    xs = pl.BlockSpec((1, tm, D_MODEL), lambda bi, i, j: (bi, i, 0))
    mod = pl.BlockSpec((1, 1, D_MODEL), lambda bi, i, j: (bi, 0, 0))
    row = pl.BlockSpec((1, D_MODEL), lambda bi, i, j: (0, 0))
    return pl.pallas_call(
        functools.partial(_ffn_kernel, final=final),
        grid=(b, l // tm, nf),
        in_specs=[xs, mod, mod, mod, row,
                  pl.BlockSpec((D_MODEL, tf), lambda bi, i, j: (0, j)),
                  pl.BlockSpec((D_MODEL, tf), lambda bi, i, j: (0, nf + j)),
                  pl.BlockSpec((tf, D_MODEL), lambda bi, i, j: (j, 0)),
                  row],
        out_specs=xs,
        out_shape=jax.ShapeDtypeStruct(x.shape, F32),
        scratch_shapes=[pltpu.VMEM((tm, D_MODEL), BF16), pltpu.VMEM((tm, D_MODEL), F32)],
        compiler_params=_cp("parallel", "parallel", "arbitrary"),
        name="ffn",
    )(x, sh, sc, g2, ng, w_gu, w_gu, w_down, fg)


def _rope_tables(seq):
    t = np.arange(seq)
    inv = ROPE_THETA ** (-np.arange(0, ROPE_AXIS_DIM, 2, dtype=np.float64) / ROPE_AXIS_DIM)
    ang_r = (t // GRID_W)[:, None] * inv
    ang_c = (t % GRID_W)[:, None] * inv
    cos = np.concatenate([np.cos(ang_r), np.cos(ang_r), np.cos(ang_c), np.cos(ang_c)], axis=1)
    sin = np.concatenate([-np.sin(ang_r), np.sin(ang_r), -np.sin(ang_c), np.sin(ang_c)], axis=1)
    return (jnp.asarray(np.tile(cos, (1, 2)), F32), jnp.asarray(np.tile(sin, (1, 2)), F32))


def _hy_feats(seq):
    t = np.linspace(0.0, 1.0, seq)[:, None]
    w = 2.0 * math.pi * np.arange(seq)[:, None] / seq
    f = np.linspace(1e-4, HY_BANDS - 1, HY_BANDS)
    feats = np.concatenate([t, np.cos(f * w), -np.sin(f * w)], axis=1)
    full = np.concatenate([feats, feats[:1], feats[:0:-1]], axis=0)
    return jnp.asarray(np.pad(full, ((0, 0), (0, LANE - HY_EMB))), F32)


def _hy_abs_deltas():
    lo = math.log(HY_TARGET) / HY_SLOW_DECAY_PCT
    hi = math.log(HY_TARGET) / HY_FAST_DECAY_PCT
    return jnp.asarray(np.abs(np.linspace(lo, hi, HY_WIDTH))[None], F32)


def _fft_tables():
    r_ = FFT_R
    n = r_ * r_
    n2 = np.arange(r_)[:, None, None]
    k1 = np.arange(FFT_K1)[None, :, None]
    n1 = np.arange(r_ // 2)[None, None, :]
    live = (k1 <= r_ // 2).astype(np.float64)
    th = 2.0 * math.pi * ((k1 * (r_ * n1 + n2)) % n) / n
    fwd = np.concatenate([np.cos(th) * live, -np.sin(th) * live], axis=1)
    wgt = np.where((k1 == 0) | (k1 == r_ // 2), 1.0, 2.0)
    inv = np.transpose(fwd * np.concatenate([wgt, wgt], axis=1), (0, 2, 1)) / n
    kk = np.arange(r_)
    ph = 2.0 * math.pi * ((kk[:, None] * kk[None, :]) % r_) / r_
    fr, fi = np.cos(ph), -np.sin(ph)
    fs = np.block([[fr, -fi], [fi, fr]])
    fc = np.block([[fr, fi], [-fi, fr]])
    return tuple(_hi_lo_rows(a) for a in (fwd, inv, fs, fc))


def _hi_lo_rows(t):
    t = jnp.asarray(t, F32)
    hi = t.astype(BF16)
    lo = (t - hi.astype(F32)).astype(BF16)
    return jnp.concatenate([hi, lo], axis=-2)


def _dense_dft_tables(seq):
    n = 2 * seq
    kk = np.arange(n)
    ph = 2.0 * math.pi * ((kk[:, None] * kk[None, :]) % n) / n
    ff = np.concatenate([np.cos(ph), -np.sin(ph)], axis=0)
    fi = np.concatenate([np.cos(ph[:seq]), -np.sin(ph[:seq])], axis=1) / n
    return jnp.asarray(ff, F32), jnp.asarray(fi, F32)


def _head_expand():
    e = np.zeros((LANE, 2 * SSD_INNER), np.float32)
    for h in range(2 * SSD_HEADS):
        e[h, h * SSD_HEAD_DIM:(h + 1) * SSD_HEAD_DIM] = 1.0
    return jnp.asarray(e, BF16)


def _group_mean():
    g = np.kron(np.eye(LANE // ATTN_HEAD_DIM), np.ones((ATTN_HEAD_DIM, ATTN_HEAD_DIM))) / ATTN_HEAD_DIM
    return jnp.asarray(g, F32)


def _relayout_w_in(w):
    cols = [w[:, OFF_HY:OFF_HY + 3 * HY_WIDTH], w[:, OFF_Q:OFF_Q + ATTN_INNER], w[:, OFF_Z:OFF_Z + SSD_INNER],
            w[:, OFF_XB:OFF_XB + SSD_XB], w[:, OFF_C:OFF_C + SSD_GN],
            w[:, OFF_K:OFF_K + ATTN_KV_INNER], w[:, OFF_V:OFF_V + ATTN_KV_INNER],
            w[:, OFF_DT:OFF_DT + 2 * SSD_HEADS]]
    wr = jnp.concatenate(cols, axis=1)
    return jnp.pad(wr, ((0, 0), (0, NP - wr.shape[1]))).astype(BF16)


def _pad_row(v):
    v = v.reshape(1, -1)
    return jnp.pad(v, ((0, 0), (0, LANE - v.shape[1])))


def _layer_params(l, raw, tables):
    p = {k: v[l] for k, v in raw.items()}
    p["w_in_r"] = _relayout_w_in(p["w_in"])
    p["dt_bias_row"] = _pad_row(p["ssd_dt_bias"])
    p["a_log_row"] = _pad_row(p["ssd_a_log"])
    p["d_row"] = jnp.repeat(p["ssd_d"], SSD_HEAD_DIM)[None]
    p["head_expand"] = tables["head_expand"]
    p["gq"] = jnp.tile(p["q_norm_g"], LANE // ATTN_HEAD_DIM)[None]
    p["gk"] = jnp.tile(p["k_norm_g"], LANE // ATTN_HEAD_DIM)[None]
    p["hy_w1p"] = jnp.pad(p["hy_w1"], ((0, LANE - HY_EMB), (0, 0)))
    p["w_out_b"] = p["w_out"].astype(BF16)
    p["w_gu_b"] = p["w_gu"].astype(BF16)
    p["w_down_b"] = p["w_down"].astype(BF16)
    return p


def _mixers(proj, p, tables, is_ctx, ssd_init):
    b, l, _ = proj.shape
    xbc = _ssd_conv(proj, p["ssd_conv_w"], p["ssd_conv_b"])
    yf, yb, sf, sb = _ssd_scan(xbc, proj, ssd_init[0], ssd_init[1], p)
    rope = tables["rope_ctx"] if is_ctx else tables["rope"]
    q, k, v = _attn_prep(proj, rope[0], rope[1], p["gq"], p["gk"], tables["group_mean"])
    return (yf, yb, sf, sb), (q, k, v)


def _hyena(proj, p, tables, is_ctx):
    vx, x0 = _hy_conv(proj, p["hy_conv_w"], p["hy_conv_b"])
    seq = proj.shape[1]
    bias = p["hy_bias"][None]
    if is_ctx:
        taps = _hy_taps(tables["feats_ctx"], p, tables["abs_deltas"], seq)
        return _hy_ctx(vx, x0, taps, tables["dft_ctx"][0], tables["dft_ctx"][1], bias)
    t_fwd, t_inv, fs, fc = tables["fft"]
    taps = _hy_taps(tables["feats"], p, tables["abs_deltas"], seq)
    kre, kim = _fft2_filter(*_fft1(taps[None], t_fwd, True), fs)
    are, aim = _fft1(vx, t_fwd, False)
    zre, zim = _fft2(are, aim, kre, kim, fs, fc)
    return _ifft1(zre, zim, t_inv, vx, x0, bias)


def _tail(x, proj, ssd, at, hy, mod, p, final_g, final):
    x = _mix_out(x, ssd[0], ssd[1], proj, at, hy, mod[2], p["ssd_norm_g"][None], p["attn_norm_g"][None],
                 p["hy_norm_g"][None], p["w_out_b"])
    return _ffn(x, mod[3], mod[4], mod[5], p["norm2_g"][None], p["w_gu_b"], p["w_down_b"], final_g[None], final)


def _layer(x, xc, mod_rows, p, tables, final_g, last):
    b = x.shape[0]
    mod_x = [mod_rows[:b, i * D_MODEL:(i + 1) * D_MODEL][:, None, :] for i in range(N_MOD)]
    mod_c = [jnp.broadcast_to(mod_rows[b:b + 1, i * D_MODEL:(i + 1) * D_MODEL][:, None, :], (b, 1, D_MODEL))
             for i in range(N_MOD)]
    g1 = p["norm1_g"][None]
    proj = _proj_in(x, mod_x[0], mod_x[1], g1, p["w_in_r"])
    projc = _proj_in(xc, mod_c[0], mod_c[1], g1, p["w_in_r"])
    zeros = jnp.zeros((b, SSD_STATE, SSD_INNER), F32)
    ssd_c, qkv_c = _mixers(projc, p, tables, True, (zeros, zeros))
    ssd_x, qkv_x = _mixers(proj, p, tables, False, (ssd_c[2], ssd_c[3]))
    k_all = jnp.concatenate([qkv_c[1], qkv_x[1]], axis=2)
    v_all = jnp.concatenate([qkv_c[2], qkv_x[2]], axis=3)
    at = _flash(qkv_x[0], k_all, v_all)
    hy = _hyena(proj, p, tables, False)
    x = _tail(x, proj, ssd_x, at, hy, mod_x, p, final_g, last)
    if last:
        return x, None
    at_c = _flash(qkv_c[0], qkv_c[1], qkv_c[2])
    hy_c = _hyena(projc, p, tables, True)
    xc = _tail(xc, projc, ssd_c, at_c, hy_c, mod_c, p, final_g, False)
    return x, xc


def kernel(x, c, ctx, c_ctx, w_mod, b_mod, norm1_g, w_in, ssd_conv_w, ssd_conv_b, ssd_a_log, ssd_dt_bias, ssd_d, ssd_norm_g, q_norm_g, k_norm_g, attn_norm_g, hy_conv_w, hy_conv_b, hy_w1, hy_b1, hy_freq, hy_w2, hy_b2, hy_w3, hy_bias, hy_norm_g, w_out, norm2_g, w_gu, w_down, final_g):
    b, seq, _ = x.shape
    ctx_len = ctx.shape[1]
    depth = w_mod.shape[0]
    assert 2 * seq == FFT_R * FFT_R and b + 1 <= SUBLANE
    raw = dict(w_in=w_in, ssd_conv_w=ssd_conv_w, ssd_conv_b=ssd_conv_b, ssd_a_log=ssd_a_log,
               ssd_dt_bias=ssd_dt_bias, ssd_d=ssd_d, ssd_norm_g=ssd_norm_g, q_norm_g=q_norm_g,
               k_norm_g=k_norm_g, attn_norm_g=attn_norm_g, hy_conv_w=hy_conv_w, hy_conv_b=hy_conv_b,
               hy_w1=hy_w1, hy_b1=hy_b1, hy_freq=hy_freq, hy_w2=hy_w2, hy_b2=hy_b2, hy_w3=hy_w3,
               hy_bias=hy_bias, hy_norm_g=hy_norm_g, w_out=w_out, norm2_g=norm2_g, w_gu=w_gu, w_down=w_down,
               norm1_g=norm1_g)
    ones = jnp.ones((ctx_len, LANE), F32)
    tables = dict(rope=_rope_tables(seq), rope_ctx=(ones, jnp.zeros_like(ones)),
                  feats=_hy_feats(seq), feats_ctx=_hy_feats(ctx_len), abs_deltas=_hy_abs_deltas(),
                  fft=_fft_tables(), dft_ctx=_dense_dft_tables(ctx_len),
                  head_expand=_head_expand(), group_mean=_group_mean())
    c_rows = jnp.concatenate([c, c_ctx[None], jnp.zeros((SUBLANE - b - 1, D_MODEL), F32)], axis=0)
    xc = ctx
    for l in range(depth):
        p = _layer_params(l, raw, tables)
        mod_rows = _mod_call(c_rows, w_mod[l], b_mod[l][None])
        x, xc = _layer(x, xc, mod_rows, p, tables, final_g, l == depth - 1)
    return x
```

```python
import functools
import math

import numpy as np
import jax
import jax.numpy as jnp
from jax import lax
from jax.experimental import pallas as pl
from jax.experimental.pallas import tpu as pltpu

F32 = jnp.float32
BF16 = jnp.bfloat16
HI = lax.Precision.HIGHEST

D_MODEL = 1024
GRID_W = 64
EPS = 1e-6
SSD_HEADS = 8
SSD_HEAD_DIM = 64
SSD_INNER = 512
SSD_STATE = 128
SSD_CHUNK = 128
SSD_GN = 256
SSD_XB = SSD_INNER + SSD_GN
ATTN_HEADS = 8
ATTN_KV_HEADS = 2
ATTN_HEAD_DIM = 64
ATTN_INNER = 512
ATTN_KV_INNER = 128
ATTN_SCALE = ATTN_HEAD_DIM ** -0.5
LOG2E = math.log2(math.e)
ATTN_V_ROWS = 80
ROPE_THETA = 10000.0
ROPE_AXIS_DIM = ATTN_HEAD_DIM // 2
HY_WIDTH = 512
HY_BANDS = 16
HY_EMB = 1 + 2 * HY_BANDS
HY_HIDDEN = 64
HY_FAST_DECAY_PCT = 0.3
HY_SLOW_DECAY_PCT = 1.5
HY_TARGET = 1e-2
D_MIX = 1536
D_FF = 2816
N_MOD = 6

OFF_K = 0
OFF_V = OFF_K + ATTN_KV_INNER
OFF_XB = OFF_V + ATTN_KV_INNER
OFF_DT = OFF_XB + SSD_XB
OFF_C = OFF_DT + 2 * SSD_HEADS
OFF_Q = OFF_C + SSD_GN
OFF_Z = OFF_Q + ATTN_INNER
OFF_HY = OFF_Z + SSD_INNER

P_HY = 0
P_Q = 1536
P_X = 2048
P_B = 2560
P_C = 2816
P_Z = 3072
P_K = 3584
P_V = 3712
P_DT = 3840
NP = 4096

LANE = 128
SUBLANE = 8
HALO = 16
VMEM_LIMIT = 48 * 1024 * 1024

FFT_R = 128
FFT_G = 8
FFT_K1 = 72


def _cp(*sem):
    return pltpu.CompilerParams(dimension_semantics=sem, vmem_limit_bytes=VMEM_LIMIT)


def _silu(x):
    return x * (1.0 / (1.0 + jnp.exp(-x)))


def _softplus(x):
    return jnp.maximum(x, 0.0) + jnp.log(1.0 + jnp.exp(-jnp.abs(x)))


def _rms(x, g):
    return x * lax.rsqrt(jnp.mean(x * x, axis=-1, keepdims=True) + EPS) * g


def _dot(a, b):
    return jnp.dot(a, b, preferred_element_type=F32)


def _dot_hi(a, b):
    return jnp.dot(a, b, precision=HI, preferred_element_type=F32)


def _split_bf16(x):
    hi = x.astype(BF16)
    return hi, (x - hi.astype(F32)).astype(BF16)


def _dot3_table(t_hl, x):
    m = t_hl.shape[0] // 2
    x_hi, x_lo = _split_bf16(x)
    a = _dot(t_hl, x_hi)
    return a[:m] + a[m:] + _dot(t_hl[:m], x_lo)


def _dot1_table(t_hl, x):
    return _dot(t_hl[:t_hl.shape[0] // 2], x.astype(BF16))


def _mod_kernel(c_ref, w_ref, b_ref, o_ref):
    o_ref[...] = _dot_hi(_silu(c_ref[...]), w_ref[...]) + b_ref[...]


def _mod_call(c_rows, w, b):
    n = w.shape[1]
    tn = 1024
    return pl.pallas_call(
        _mod_kernel,
        grid=(n // tn,),
        in_specs=[pl.BlockSpec((SUBLANE, D_MODEL), lambda j: (0, 0)),
                  pl.BlockSpec((D_MODEL, tn), lambda j: (0, j)),
                  pl.BlockSpec((1, tn), lambda j: (0, j))],
        out_specs=pl.BlockSpec((SUBLANE, tn), lambda j: (0, j)),
        out_shape=jax.ShapeDtypeStruct((SUBLANE, n), F32),
        compiler_params=_cp("parallel"),
        name="adaln_mod",
    )(c_rows, w, b)


def _proj_kernel(x_ref, sh_ref, sc_ref, g_ref, w_ref, o_ref, dt_ref, h_ref, *, dt_tile, dt_off):
    j = pl.program_id(2)

    @pl.when(j == 0)
    def _():
        y = _rms(x_ref[0], g_ref[...])
        h_ref[...] = (y * (1.0 + sc_ref[0]) + sh_ref[0]).astype(BF16)

    r = _dot(h_ref[...], w_ref[...])
    o_ref[0] = r.astype(BF16)

    @pl.when(j == dt_tile)
    def _():
        dt_ref[0] = r[:, dt_off:dt_off + LANE]


def _proj_in(x, sh, sc, g, w):
    b, l, _ = x.shape
    tm = min(l, 1024)
    tn = 512
    return pl.pallas_call(
        functools.partial(_proj_kernel, dt_tile=P_DT // tn, dt_off=P_DT % tn),
        grid=(b, l // tm, NP // tn),
        in_specs=[pl.BlockSpec((1, tm, D_MODEL), lambda bi, i, j: (bi, i, 0)),
                  pl.BlockSpec((1, 1, D_MODEL), lambda bi, i, j: (bi, 0, 0)),
                  pl.BlockSpec((1, 1, D_MODEL), lambda bi, i, j: (bi, 0, 0)),
                  pl.BlockSpec((1, D_MODEL), lambda bi, i, j: (0, 0)),
                  pl.BlockSpec((D_MODEL, tn), lambda bi, i, j: (0, j))],
        out_specs=[pl.BlockSpec((1, tm, tn), lambda bi, i, j: (bi, i, j)),
                   pl.BlockSpec((1, tm, LANE), lambda bi, i, j: (bi, i, 0))],
        out_shape=[jax.ShapeDtypeStruct((b, l, NP), BF16), jax.ShapeDtypeStruct((b, l, LANE), F32)],
        scratch_shapes=[pltpu.VMEM((tm, D_MODEL), BF16)],
        compiler_params=_cp("parallel", "parallel", "arbitrary"),
        name="proj_in",
    )(x, sh, sc, g, w)


def _dwconv3(u, prev_row, next_row, w, b):
    tm = u.shape[0]
    ri = lax.broadcasted_iota(jnp.int32, u.shape, 0)
    um = jnp.where(ri == 0, prev_row, pltpu.roll(u, 1, 0))
    up = jnp.where(ri == tm - 1, next_row, pltpu.roll(u, tm - 1, 0))
    return um * w[0:1] + u * w[1:2] + up * w[2:3] + b


def _conv_group(refs, i, n_i):
    u_ref, p_ref, n_ref, w_ref, b_ref = refs
    prev_row = jnp.where(i > 0, p_ref[0].astype(F32)[HALO - 1:HALO, :], 0.0)
    next_row = jnp.where(i < n_i - 1, n_ref[0].astype(F32)[0:1, :], 0.0)
    return _dwconv3(u_ref[0].astype(F32), prev_row, next_row, w_ref[...], b_ref[...])


def _hy_conv_kernel(*refs):
    i, n_i = pl.program_id(1), pl.num_programs(1)
    v = _conv_group(refs[0:5], i, n_i)
    x1 = _conv_group(refs[5:10], i, n_i)
    x0 = _conv_group(refs[10:15], i, n_i)
    vx_ref, x0_ref = refs[15], refs[16]
    vx_ref[0] = v * x1
    x0_ref[0] = x0


def _conv_specs(tm, tc, l, col_block, w_block):
    nrb = l // HALO
    per = tm // HALO
    return [
        pl.BlockSpec((1, tm, tc), lambda bi, i: (bi, i, col_block)),
        pl.BlockSpec((1, HALO, tc), lambda bi, i: (bi, jnp.maximum(i * per - 1, 0), col_block)),
        pl.BlockSpec((1, HALO, tc), lambda bi, i: (bi, jnp.minimum((i + 1) * per, nrb - 1), col_block)),
        pl.BlockSpec((SUBLANE, tc), lambda bi, i: (0, w_block)),
        pl.BlockSpec((1, tc), lambda bi, i: (0, w_block)),
    ]


def _pad_taps(w):
    return jnp.pad(w, ((0, SUBLANE - w.shape[0]), (0, 0)))


def _hy_conv(proj, conv_w, conv_b):
    b, l, _ = proj.shape
    tm = min(l, 1024)
    tc = HY_WIDTH
    wp = _pad_taps(conv_w)
    bp = conv_b[None]
    specs, args = [], []
    for grp in range(3):
        specs += _conv_specs(tm, tc, l, P_HY // tc + grp, grp)
        args += [proj, proj, proj, wp, bp]
    out_spec = pl.BlockSpec((1, tm, tc), lambda bi, i: (bi, i, 0))
    shp = jax.ShapeDtypeStruct((b, l, tc), F32)
    return pl.pallas_call(
        _hy_conv_kernel,
        grid=(b, l // tm),
        in_specs=specs,
        out_specs=[out_spec, out_spec],
        out_shape=[shp, shp],
        compiler_params=_cp("parallel", "parallel"),
        name="hy_conv",
    )(*args)


def _split_pieces(x, n):
    pieces, r = [], x
    for k in range(n):
        pc = r.astype(BF16)
        pieces.append(pc)
        if k + 1 < n:
            r = r - pc.astype(F32)
    return pieces


def _select_left(sel, x, n):
    w = x.shape[1]
    r = _dot(sel, jnp.concatenate(_split_pieces(x, n), axis=1))
    return sum(r[:, k * w:(k + 1) * w] for k in range(n))


def _select_right(x, sel, n):
    m = x.shape[0]
    r = _dot(jnp.concatenate(_split_pieces(x, n), axis=0), sel)
    return sum(r[k * m:(k + 1) * m] for k in range(n))


def _ssd_chunk(xbc, dt_raw, a, bias, d_row, e_d, st, lane0, fwd):
    q = SSD_CHUNK
    xs = xbc[:, :SSD_INNER]
    bm = xbc[:, SSD_INNER:SSD_XB]
    cm = xbc[:, SSD_XB:]
    ri = lax.broadcasted_iota(jnp.int32, (q, q), 0)
    ci = lax.broadcasted_iota(jnp.int32, (q, q), 1)
    dt = _softplus(dt_raw + bias)
    adt = dt * a
    cs = _select_left((ci <= ri).astype(BF16), adt, 3)
    tot = cs[q - 1:q, :]
    if fwd:
        key = cs
        w_c = dt * jnp.exp(tot - key)
        e_c = jnp.exp(key)
        mask = ci <= ri
    else:
        key = cs - adt
        w_c = dt * jnp.exp(key)
        e_c = jnp.exp(tot - key)
        mask = ci >= ri
    key_t = key.T
    dt_t = dt.T
    dec_c = jnp.broadcast_to(jnp.exp(tot), (2 * SUBLANE, LANE))
    spread = _select_right(jnp.concatenate([w_c, e_c, dec_c], axis=0), e_d, 2)
    w_e, e_off, decay = spread[:q], spread[q:2 * q], spread[2 * q:2 * q + 1]
    x_b = xs.astype(BF16)
    x_w = (xs * w_e).astype(BF16)
    st_b = st.astype(BF16)
    y_parts, st_parts, off_parts = [], [], []
    hg = SSD_HEADS // 2
    for g in range(2):
        bg = bm[:, g * SSD_STATE:(g + 1) * SSD_STATE]
        cg = cm[:, g * SSD_STATE:(g + 1) * SSD_STATE].astype(BF16)
        bg_t = bg.T.astype(BF16)
        gmat = _dot(cg, bg_t)
        gs = slice(g * hg * SSD_HEAD_DIM, (g + 1) * hg * SSD_HEAD_DIM)
        off_parts.append(_dot(cg, st_b[:, gs]))
        st_parts.append(_dot(bg_t, x_w[:, gs]))
        for hh in range(hg):
            h = g * hg + hh
            col = key[:, lane0 + h:lane0 + h + 1]
            row = key_t[lane0 + h:lane0 + h + 1, :]
            diff = (col - row) if fwd else (row - col)
            lm = jnp.exp(jnp.where(mask, diff, -1e30)) * dt_t[lane0 + h:lane0 + h + 1, :]
            s = (gmat * lm).astype(BF16)
            y_parts.append(_dot(s, x_b[:, h * SSD_HEAD_DIM:(h + 1) * SSD_HEAD_DIM]))
    y = jnp.concatenate(y_parts, axis=1) + jnp.concatenate(off_parts, axis=1) * e_off
    if fwd:
        y = y + d_row * xs
    st_new = st * decay + jnp.concatenate(st_parts, axis=1)
    return y, st_new


def _ssd_kernel(xf_ref, xfp_ref, xfn_ref, xb_ref, xbp_ref, xbn_ref, dtf_ref, dtb_ref, cw_ref, cb_ref,
                bias_ref, alog_ref, d_ref, e_ref, sf0_ref, sb0_ref, yf_ref, yb_ref, sf_ref, sb_ref, stf, stb):
    c = pl.program_id(1)
    nc = pl.num_programs(1)

    @pl.when(c == 0)
    def _():
        stf[...] = sf0_ref[0]
        stb[...] = sb0_ref[0]

    def conv_silu(u_ref, p_ref, n_ref, chunk):
        prev_row = jnp.where(chunk > 0, p_ref[0].astype(F32)[HALO - 1:HALO, :], 0.0)
        next_row = jnp.where(chunk < nc - 1, n_ref[0].astype(F32)[0:1, :], 0.0)
        return _silu(_dwconv3(u_ref[0].astype(F32), prev_row, next_row, cw_ref[...], cb_ref[...]))

    a = -jnp.exp(alog_ref[...])
    bias = bias_ref[...]
    xf = conv_silu(xf_ref, xfp_ref, xfn_ref, c)
    xb = conv_silu(xb_ref, xbp_ref, xbn_ref, nc - 1 - c)
    yf, sf = _ssd_chunk(xf, dtf_ref[0], a, bias, d_ref[...], e_ref[:, :SSD_INNER], stf[...], 0, True)
    yb, sb = _ssd_chunk(xb, dtb_ref[0], a, bias, d_ref[...], e_ref[:, SSD_INNER:], stb[...], SSD_HEADS, False)
    yf_ref[0] = yf
    yb_ref[0] = yb
    stf[...] = sf
    stb[...] = sb
    sf_ref[0] = sf
    sb_ref[0] = sb


def _ssd_scan(proj, dt_raw, sf0, sb0, p):
    b, l, _ = proj.shape
    nc = l // SSD_CHUNK
    q = SSD_CHUNK
    w = 2 * SSD_INNER
    xcol = P_X // w
    per = q // HALO
    nrb = l // HALO
    st_spec = pl.BlockSpec((1, SSD_STATE, SSD_INNER), lambda bi, c: (bi, 0, 0))
    y_shape = jax.ShapeDtypeStruct((b, l, SSD_INNER), F32)
    st_shape = jax.ShapeDtypeStruct((b, SSD_STATE, SSD_INNER), F32)
    row = lambda n: pl.BlockSpec((1, n), lambda bi, c: (0, 0))

    def chunk_specs(chunk_of):
        return [pl.BlockSpec((1, q, w), lambda bi, c: (bi, chunk_of(c), xcol)),
                pl.BlockSpec((1, HALO, w), lambda bi, c: (bi, jnp.maximum(chunk_of(c) * per - 1, 0), xcol)),
                pl.BlockSpec((1, HALO, w), lambda bi, c: (bi, jnp.minimum((chunk_of(c) + 1) * per, nrb - 1), xcol))]

    fwd_of = lambda c: c
    bwd_of = lambda c: nc - 1 - c
    return pl.pallas_call(
        _ssd_kernel,
        grid=(b, nc),
        in_specs=chunk_specs(fwd_of) + chunk_specs(bwd_of) + [
            pl.BlockSpec((1, q, LANE), lambda bi, c: (bi, c, 0)),
            pl.BlockSpec((1, q, LANE), lambda bi, c: (bi, nc - 1 - c, 0)),
            pl.BlockSpec((SUBLANE, w), lambda bi, c: (0, 0)), row(w),
            row(LANE), row(LANE), row(SSD_INNER),
            pl.BlockSpec((LANE, 2 * SSD_INNER), lambda bi, c: (0, 0)),
            st_spec, st_spec],
        out_specs=[pl.BlockSpec((1, q, SSD_INNER), lambda bi, c: (bi, c, 0)),
                   pl.BlockSpec((1, q, SSD_INNER), lambda bi, c: (bi, nc - 1 - c, 0)),
                   st_spec, st_spec],
        out_shape=[y_shape, y_shape, st_shape, st_shape],
        scratch_shapes=[pltpu.VMEM((SSD_STATE, SSD_INNER), F32), pltpu.VMEM((SSD_STATE, SSD_INNER), F32)],
        compiler_params=_cp("parallel", "arbitrary"),
        name="ssd_scan",
    )(proj, proj, proj, proj, proj, proj, dt_raw, dt_raw, _pad_taps(p["ssd_conv_w"]), p["ssd_conv_b"][None],
      p["dt_bias_row"], p["a_log_row"], p["d_row"], p["head_expand"], sf0, sb0)


def _attn_prep_kernel(q_ref, k_ref, v_ref, cos_ref, sin_ref, gq_ref, gk_ref, gm_ref, qo_ref, ko_ref, vo_ref):
    cos = cos_ref[...]
    sin = sin_ref[...]
    gm = gm_ref[...]
    lane = lax.broadcasted_iota(jnp.int32, cos.shape, 1)
    first = jnp.bitwise_and(lane, 31) < 16
    hd = ATTN_HEAD_DIM

    def norm_rope(t, g):
        ms = _dot_hi(t * t, gm)
        y = t * lax.rsqrt(ms + EPS) * g
        partner = jnp.where(first, pltpu.roll(y, LANE - 16, 1), pltpu.roll(y, 16, 1))
        return y * cos + partner * sin

    for s in range(ATTN_INNER // LANE):
        qs = norm_rope(q_ref[0, :, s * LANE:(s + 1) * LANE].astype(F32), gq_ref[...]) * (ATTN_SCALE * LOG2E)
        qt = qs.T.astype(BF16)
        qo_ref[0, 2 * s] = qt[:hd]
        qo_ref[0, 2 * s + 1] = qt[hd:]
    ks = norm_rope(k_ref[0].astype(F32), gk_ref[...])
    ko_ref[0, 0] = ks[:, :hd].astype(BF16)
    ko_ref[0, 1] = ks[:, hd:].astype(BF16)
    vt = v_ref[0].astype(F32).T.astype(BF16)
    tm = vt.shape[1]
    pad_rows = lax.broadcasted_iota(jnp.int32, (ATTN_V_ROWS - hd, tm), 0)
    tail = jnp.where(pad_rows == 0, 1.0, 0.0).astype(BF16)
    for g in range(ATTN_KV_HEADS):
        vo_ref[0, g, 0:hd, :] = vt[g * hd:(g + 1) * hd]
        vo_ref[0, g, hd:ATTN_V_ROWS, :] = tail


def _attn_prep(proj, cos, sin, gq, gk, gm):
    b, l, _ = proj.shape
    tm = min(l, 1024)
    hd = ATTN_HEAD_DIM
    const = lambda r, c: pl.BlockSpec((r, c), lambda bi, i: (0, 0))
    return pl.pallas_call(
        _attn_prep_kernel,
        grid=(b, l // tm),
        in_specs=[pl.BlockSpec((1, tm, ATTN_INNER), lambda bi, i: (bi, i, P_Q // ATTN_INNER)),
                  pl.BlockSpec((1, tm, LANE), lambda bi, i: (bi, i, P_K // LANE)),
                  pl.BlockSpec((1, tm, LANE), lambda bi, i: (bi, i, P_V // LANE)),
                  pl.BlockSpec((tm, LANE), lambda bi, i: (i, 0)),
                  pl.BlockSpec((tm, LANE), lambda bi, i: (i, 0)),
                  const(1, LANE), const(1, LANE), const(LANE, LANE)],
        out_specs=[pl.BlockSpec((1, ATTN_HEADS, hd, tm), lambda bi, i: (bi, 0, 0, i)),
                   pl.BlockSpec((1, ATTN_KV_HEADS, tm, hd), lambda bi, i: (bi, 0, i, 0)),
                   pl.BlockSpec((1, ATTN_KV_HEADS, ATTN_V_ROWS, tm), lambda bi, i: (bi, 0, 0, i))],
        out_shape=[jax.ShapeDtypeStruct((b, ATTN_HEADS, hd, l), BF16),
                   jax.ShapeDtypeStruct((b, ATTN_KV_HEADS, l, hd), BF16),
                   jax.ShapeDtypeStruct((b, ATTN_KV_HEADS, ATTN_V_ROWS, l), BF16)],
        compiler_params=_cp("parallel", "parallel"),
        name="attn_prep",
    )(proj, proj, proj, cos, sin, gq, gk, gm)


def _flash_kernel(q_ref, k_ref, v_ref, o_ref, s_ref, m_ref, acc_ref, *, tk, nk):
    r = ATTN_HEADS // ATTN_KV_HEADS
    hd = ATTN_HEAD_DIM

    def scores(j, h, slot):
        start = pl.multiple_of(j * tk, tk)
        s_ref[slot] = _dot(k_ref[0, 0, pl.ds(start, tk), :], q_ref[0, h])

    m_ref[...] = jnp.full(m_ref.shape, -1e30, F32)
    acc_ref[...] = jnp.zeros(acc_ref.shape, F32)
    scores(0, 0, 0)

    def body(j, _):
        start = pl.multiple_of(j * tk, tk)
        vs = v_ref[0, 0, :, pl.ds(start, tk)]
        j_next = jnp.minimum(j + 1, nk - 1)
        for h in range(r):
            if h < r - 1:
                scores(j, h + 1, (h + 1) % 2)
            else:
                scores(j_next, 0, 0)
            s = s_ref[h % 2]
            m = m_ref[h]
            mn = jnp.maximum(m, jnp.max(s, axis=0, keepdims=True))
            alpha = jnp.exp2(m - mn)
            p = jnp.exp2(s - mn).astype(BF16)
            m_ref[h] = mn
            acc_ref[h] = alpha * acc_ref[h] + _dot(vs, p)
        return 0

    lax.fori_loop(0, nk, body, 0)
    outs = []
    for h in range(r):
        acc = acc_ref[h]
        outs.append((acc[:hd] * (1.0 / acc[hd:hd + 1])).T)
    o_ref[0] = jnp.concatenate(outs, axis=1)


def _flash(q, k, v):
    b, _, hd, l = q.shape
    lk = k.shape[2]
    r = ATTN_HEADS // ATTN_KV_HEADS
    assert r % 2 == 0
    tq = min(l, 512)
    tk = 768 if lk % 768 == 0 else 256
    return pl.pallas_call(
        functools.partial(_flash_kernel, tk=tk, nk=lk // tk),
        grid=(b, ATTN_KV_HEADS, l // tq),
        in_specs=[pl.BlockSpec((1, r, hd, tq), lambda bi, g, i: (bi, g, 0, i)),
                  pl.BlockSpec((1, 1, lk, hd), lambda bi, g, i: (bi, g, 0, 0)),
                  pl.BlockSpec((1, 1, ATTN_V_ROWS, lk), lambda bi, g, i: (bi, g, 0, 0))],
        out_specs=pl.BlockSpec((1, tq, r * hd), lambda bi, g, i: (bi, i, g)),
        out_shape=jax.ShapeDtypeStruct((b, l, ATTN_INNER), F32),
        scratch_shapes=[pltpu.VMEM((2, tk, tq), F32), pltpu.VMEM((r, 1, tq), F32),
                        pltpu.VMEM((r, ATTN_V_ROWS, tq), F32)],
        compiler_params=_cp("parallel", "parallel", "parallel"),
        name="flash_gqa",
    )(q, k, v)


def _taps_kernel(ft_ref, t_ref, w1_ref, b1_ref, fr_ref, w2_ref, b2_ref, w3_ref, adel_ref, f0_ref, f1_ref):
    fr = fr_ref[...]
    h = jnp.sin(fr * (_dot_hi(w1_ref[...], ft_ref[...]) + b1_ref[...]))
    h = jnp.sin(fr * (_dot_hi(w2_ref[...], h) + b2_ref[...]))
    y = _dot_hi(h.T, w3_ref[...])
    decay = jnp.exp(-t_ref[...] * adel_ref[...])
    f0_ref[...] = y[:, :HY_WIDTH] * decay
    f1_ref[...] = y[:, HY_WIDTH:] * decay


def _hy_taps(feats_t, t_col, p, adel, seq):
    tm = min(seq, 1024)
    const = lambda r, c: pl.BlockSpec((r, c), lambda i: (0, 0))
    out = pl.BlockSpec((tm, HY_WIDTH), lambda i: (i, 0))
    shp = jax.ShapeDtypeStruct((seq, HY_WIDTH), F32)
    f0, f1 = pl.pallas_call(
        _taps_kernel,
        grid=(seq // tm,),
        in_specs=[pl.BlockSpec((LANE, tm), lambda i: (0, i)),
                  pl.BlockSpec((tm, 1), lambda i: (i, 0)),
                  const(HY_HIDDEN, LANE), const(HY_HIDDEN, 1), const(HY_HIDDEN, 1),
                  const(HY_HIDDEN, HY_HIDDEN), const(HY_HIDDEN, 1),
                  const(HY_HIDDEN, 2 * HY_WIDTH), const(1, HY_WIDTH)],
        out_specs=[out, out],
        out_shape=[shp, shp],
        compiler_params=_cp("parallel"),
        name="hy_taps",
    )(feats_t, t_col, p["hy_w1p"].T, p["hy_b1"][:, None], p["hy_freq"][:, None], p["hy_w2"].T,
      p["hy_b2"][:, None], p["hy_w3"], adel)
    return jnp.concatenate([f0, jnp.zeros((1, HY_WIDTH), F32), jnp.flip(f1[1:], axis=0)], axis=0)


def _fft1_kernel(x_ref, t_ref, are_ref, aim_ref, *, full):
    half = FFT_R // 2
    if full:
        k1 = lax.broadcasted_iota(jnp.int32, (2 * FFT_K1, x_ref.shape[3]), 0)
        sgn = jnp.where(jnp.bitwise_and(k1, 1) == 0, 1.0, -1.0)
    for r in range(FFT_G):
        t = t_ref[r]
        if full:
            a = _dot3_table(t, x_ref[0, :half, r, :]) + sgn * _dot3_table(t, x_ref[0, half:, r, :])
        else:
            a = _dot1_table(t, x_ref[0, :, r, :])
        are_ref[0, :, r, :] = a[:FFT_K1]
        aim_ref[0, :, r, :] = a[FFT_K1:]


def _fft1(x, table, full):
    nb, ln, c = x.shape
    r_ = FFT_R
    n1 = ln // r_
    spec_o = pl.BlockSpec((1, FFT_K1, FFT_G, c), lambda bi, g: (bi, 0, g, 0))
    shp = jax.ShapeDtypeStruct((nb, FFT_K1, r_, c), F32)
    return pl.pallas_call(
        functools.partial(_fft1_kernel, full=full),
        grid=(nb, r_ // FFT_G),
        in_specs=[pl.BlockSpec((1, n1, FFT_G, c), lambda bi, g: (bi, 0, g, 0)),
                  pl.BlockSpec((FFT_G, 4 * FFT_K1, r_ // 2), lambda bi, g: (g, 0, 0))],
        out_specs=[spec_o, spec_o],
        out_shape=[shp, shp],
        compiler_params=_cp("parallel", "parallel"),
        name="hy_fft1",
    )(x.reshape(nb, n1, r_, c), table)


def _fft2_filter_kernel(are_ref, aim_ref, fs_ref, kre_ref, kim_ref):
    r_ = FFT_R
    for r in range(SUBLANE):
        a = jnp.concatenate([are_ref[0, r], aim_ref[0, r]], axis=0)
        b = _dot3_table(fs_ref[...], a)
        kre_ref[r] = b[:r_]
        kim_ref[r] = b[r_:]


def _fft2_filter(are, aim, fs):
    r_ = FFT_R
    c = are.shape[-1]
    spec_i = pl.BlockSpec((1, SUBLANE, r_, c), lambda kg: (0, kg, 0, 0))
    spec_o = pl.BlockSpec((SUBLANE, r_, c), lambda kg: (kg, 0, 0))
    shp = jax.ShapeDtypeStruct((FFT_K1, r_, c), F32)
    return pl.pallas_call(
        _fft2_filter_kernel,
        grid=(FFT_K1 // SUBLANE,),
        in_specs=[spec_i, spec_i, pl.BlockSpec((4 * r_, 2 * r_), lambda kg: (0, 0))],
        out_specs=[spec_o, spec_o],
        out_shape=[shp, shp],
        compiler_params=_cp("parallel"),
        name="hy_fft2_filter",
    )(are, aim, fs)


def _fft2_kernel(are_ref, aim_ref, kre_ref, kim_ref, fs_ref, fc_ref, zre_ref, zim_ref):
    r_ = FFT_R
    for r in range(SUBLANE):
        a = jnp.concatenate([are_ref[0, r], aim_ref[0, r]], axis=0)
        b = _dot1_table(fs_ref[...], a)
        br, bi = b[:r_], b[r_:]
        kr, ki = kre_ref[r], kim_ref[r]
        y = jnp.concatenate([br * kr - bi * ki, br * ki + bi * kr], axis=0)
        z = _dot1_table(fc_ref[...], y)
        zre_ref[0, r] = z[:r_]
        zim_ref[0, r] = z[r_:]


def _fft2(are, aim, kre, kim, fs, fc):
    nb, _, r_, c = are.shape
    spec_a = pl.BlockSpec((1, SUBLANE, r_, c), lambda bi, kg: (bi, kg, 0, 0))
    spec_k = pl.BlockSpec((SUBLANE, r_, c), lambda bi, kg: (kg, 0, 0))
    spec_f = pl.BlockSpec((4 * r_, 2 * r_), lambda bi, kg: (0, 0))
    shp = jax.ShapeDtypeStruct(are.shape, F32)
    return pl.pallas_call(
        _fft2_kernel,
        grid=(nb, FFT_K1 // SUBLANE),
        in_specs=[spec_a, spec_a, spec_k, spec_k, spec_f, spec_f],
        out_specs=[spec_a, spec_a],
        out_shape=[shp, shp],
        compiler_params=_cp("parallel", "parallel"),
        name="hy_fft2",
    )(are, aim, kre, kim, fs, fc)


def _ifft1_kernel(zre_ref, zim_ref, t_ref, vx_ref, x0_ref, bias_ref, o_ref):
    for r in range(FFT_G):
        z = jnp.concatenate([zre_ref[0, :, r, :], zim_ref[0, :, r, :]], axis=0)
        conv = _dot1_table(t_ref[r], z)
        o_ref[0, :, r, :] = x0_ref[0, :, r, :] * (conv + bias_ref[...] * vx_ref[0, :, r, :])


def _ifft1(zre, zim, table, vx, x0, bias):
    nb, seq, c = vx.shape
    r_ = FFT_R
    n1 = seq // r_
    spec_z = pl.BlockSpec((1, FFT_K1, FFT_G, c), lambda bi, g: (bi, 0, g, 0))
    spec_x = pl.BlockSpec((1, n1, FFT_G, c), lambda bi, g: (bi, 0, g, 0))
    out = pl.pallas_call(
        _ifft1_kernel,
        grid=(nb, r_ // FFT_G),
        in_specs=[spec_z, spec_z,
                  pl.BlockSpec((FFT_G, r_, 2 * FFT_K1), lambda bi, g: (g, 0, 0)),
                  spec_x, spec_x,
                  pl.BlockSpec((1, c), lambda bi, g: (0, 0))],
        out_specs=spec_x,
        out_shape=jax.ShapeDtypeStruct((nb, n1, r_, c), F32),
        compiler_params=_cp("parallel", "parallel"),
        name="hy_ifft1",
    )(zre, zim, table, vx.reshape(nb, n1, r_, c), x0.reshape(nb, n1, r_, c), bias)
    return out.reshape(nb, seq, c)


def _hy_ctx_kernel(vx_ref, x0_ref, taps_ref, ff_ref, fi_ref, bias_ref, o_ref):
    vx = vx_ref[0]
    seq = vx.shape[0]
    n = 2 * seq
    u = _dot_hi(ff_ref[:, :seq], vx)
    k = _dot_hi(ff_ref[...], taps_ref[...])
    ur, ui = u[:n], u[n:]
    kr, ki = k[:n], k[n:]
    y = jnp.concatenate([ur * kr - ui * ki, ur * ki + ui * kr], axis=0)
    conv = _dot_hi(fi_ref[...], y)
    o_ref[0] = x0_ref[0] * (conv + bias_ref[...] * vx)


def _hy_ctx(vx, x0, taps, ff, fi, bias):
    b, seq, c = vx.shape
    n = 2 * seq
    spec_x = pl.BlockSpec((1, seq, c), lambda bi: (bi, 0, 0))
    const = lambda r, cc: pl.BlockSpec((r, cc), lambda bi: (0, 0))
    return pl.pallas_call(
        _hy_ctx_kernel,
        grid=(b,),
        in_specs=[spec_x, spec_x, const(n, c), const(2 * n, n), const(seq, 2 * n), const(1, c)],
        out_specs=spec_x,
        out_shape=jax.ShapeDtypeStruct((b, seq, c), F32),
        compiler_params=_cp("parallel"),
        name="hy_ctx",
    )(vx, x0, taps, ff, fi, bias)


def _mix_kernel(x_ref, yf_ref, yb_ref, z_ref, at_ref, hy_ref, g1_ref, gs_ref, ga_ref, gh_ref, w_ref, o_ref):
    ys = _rms((yf_ref[0] + yb_ref[0]) * _silu(z_ref[0].astype(F32)), gs_ref[...]).astype(BF16)
    ya = _rms(at_ref[0], ga_ref[...]).astype(BF16)
    yh = _rms(hy_ref[0], gh_ref[...]).astype(BF16)
    r = (_dot(ys, w_ref[0:SSD_INNER, :])
         + _dot(ya, w_ref[SSD_INNER:SSD_INNER + ATTN_INNER, :])
         + _dot(yh, w_ref[SSD_INNER + ATTN_INNER:, :]))
    o_ref[0] = x_ref[0] + g1_ref[0] * r


def _mix_out(x, yf, yb, proj, at, hy, g1, gs, ga, gh, w):
    b, l, _ = x.shape
    tm = min(l, 512)
    c = SSD_INNER
    t512 = lambda col: pl.BlockSpec((1, tm, c), lambda bi, i: (bi, i, col))
    const = lambda r, cc: pl.BlockSpec((r, cc), lambda bi, i: (0, 0))
    xs = pl.BlockSpec((1, tm, D_MODEL), lambda bi, i: (bi, i, 0))
    return pl.pallas_call(
        _mix_kernel,
        grid=(b, l // tm),
        in_specs=[xs, t512(0), t512(0), t512(P_Z // c), t512(0), t512(0),
                  pl.BlockSpec((1, 1, D_MODEL), lambda bi, i: (bi, 0, 0)),
                  const(1, c), const(1, c), const(1, c), const(D_MIX, D_MODEL)],
        out_specs=xs,
        out_shape=jax.ShapeDtypeStruct(x.shape, F32),
        compiler_params=_cp("parallel", "parallel"),
        name="mix_out",
    )(x, yf, yb, proj, at, hy, g1, gs, ga, gh, w)


def _ffn_kernel(x_ref, sh_ref, sc_ref, g2_ref, ng_ref, wg_ref, wu_ref, wd_ref, fg_ref, o_ref, h_ref, acc_ref,
                *, final):
    j = pl.program_id(2)

    @pl.when(j == 0)
    def _():
        y = _rms(x_ref[0], ng_ref[...])
        h_ref[...] = (y * (1.0 + sc_ref[0]) + sh_ref[0]).astype(BF16)
        acc_ref[...] = jnp.zeros_like(acc_ref)

    h = h_ref[...]
    act = (_silu(_dot(h, wg_ref[...])) * _dot(h, wu_ref[...])).astype(BF16)
    acc_ref[...] += _dot(act, wd_ref[...])

    @pl.when(j == pl.num_programs(2) - 1)
    def _():
        y = x_ref[0] + g2_ref[0] * acc_ref[...]
        if final:
            y = _rms(y, fg_ref[...])
        o_ref[0] = y


def _ffn(x, sh, sc, g2, ng, w_gu, w_down, fg, final):
    b, l, _ = x.shape
    tm = min(l, 1024)
    tf = 256
    nf = D_FF // tf
    xs = pl.BlockSpec((1, tm, D_MODEL), lambda bi, i, j: (bi, i, 0))
    mod = pl.BlockSpec((1, 1, D_MODEL), lambda bi, i, j: (bi, 0, 0))
    row = pl.BlockSpec((1, D_MODEL), lambda bi, i, j: (0, 0))
    return pl.pallas_call(
        functools.partial(_ffn_kernel, final=final),
        grid=(b, l // tm, nf),
        in_specs=[xs, mod, mod, mod, row,
                  pl.BlockSpec((D_MODEL, tf), lambda bi, i, j: (0, j)),
                  pl.BlockSpec((D_MODEL, tf), lambda bi, i, j: (0, nf + j)),
                  pl.BlockSpec((tf, D_MODEL), lambda bi, i, j: (j, 0)),
                  row],
        out_specs=xs,
        out_shape=jax.ShapeDtypeStruct(x.shape, F32),
        scratch_shapes=[pltpu.VMEM((tm, D_MODEL), BF16), pltpu.VMEM((tm, D_MODEL), F32)],
        compiler_params=_cp("parallel", "parallel", "arbitrary"),
        name="ffn",
    )(x, sh, sc, g2, ng, w_gu, w_gu, w_down, fg)


def _rope_tables(seq):
    t = np.arange(seq)
    inv = ROPE_THETA ** (-np.arange(0, ROPE_AXIS_DIM, 2, dtype=np.float64) / ROPE_AXIS_DIM)
    ang_r = (t // GRID_W)[:, None] * inv
    ang_c = (t % GRID_W)[:, None] * inv
    cos = np.concatenate([np.cos(ang_r), np.cos(ang_r), np.cos(ang_c), np.cos(ang_c)], axis=1)
    sin = np.concatenate([-np.sin(ang_r), np.sin(ang_r), -np.sin(ang_c), np.sin(ang_c)], axis=1)
    return (jnp.asarray(np.tile(cos, (1, 2)), F32), jnp.asarray(np.tile(sin, (1, 2)), F32))


def _hy_feats(seq):
    t = np.linspace(0.0, 1.0, seq)[:, None]
    w = 2.0 * math.pi * np.arange(seq)[:, None] / seq
    f = np.linspace(1e-4, HY_BANDS - 1, HY_BANDS)
    feats = np.concatenate([t, np.cos(f * w), -np.sin(f * w)], axis=1)
    feats_t = np.pad(feats, ((0, 0), (0, LANE - HY_EMB))).T
    return jnp.asarray(feats_t, F32), jnp.asarray(t, F32)


def _hy_abs_deltas():
    lo = math.log(HY_TARGET) / HY_SLOW_DECAY_PCT
    hi = math.log(HY_TARGET) / HY_FAST_DECAY_PCT
    return jnp.asarray(np.abs(np.linspace(lo, hi, HY_WIDTH))[None], F32)


def _fft_tables():
    r_ = FFT_R
    n = r_ * r_
    n2 = np.arange(r_)[:, None, None]
    k1 = np.arange(FFT_K1)[None, :, None]
    n1 = np.arange(r_ // 2)[None, None, :]
    live = (k1 <= r_ // 2).astype(np.float64)
    th = 2.0 * math.pi * ((k1 * (r_ * n1 + n2)) % n) / n
    fwd = np.concatenate([np.cos(th) * live, -np.sin(th) * live], axis=1)
    wgt = np.where((k1 == 0) | (k1 == r_ // 2), 1.0, 2.0)
    inv = np.transpose(fwd * np.concatenate([wgt, wgt], axis=1), (0, 2, 1)) / n
    kk = np.arange(r_)
    ph = 2.0 * math.pi * ((kk[:, None] * kk[None, :]) % r_) / r_
    fr, fi = np.cos(ph), -np.sin(ph)
    fs = np.block([[fr, -fi], [fi, fr]])
    fc = np.block([[fr, fi], [-fi, fr]])
    return tuple(_hi_lo_rows(a) for a in (fwd, inv, fs, fc))


def _hi_lo_rows(t):
    t = jnp.asarray(t, F32)
    hi = t.astype(BF16)
    lo = (t - hi.astype(F32)).astype(BF16)
    return jnp.concatenate([hi, lo], axis=-2)


def _dense_dft_tables(seq):
    n = 2 * seq
    kk = np.arange(n)
    ph = 2.0 * math.pi * ((kk[:, None] * kk[None, :]) % n) / n
    ff = np.concatenate([np.cos(ph), -np.sin(ph)], axis=0)
    fi = np.concatenate([np.cos(ph[:seq]), -np.sin(ph[:seq])], axis=1) / n
    return jnp.asarray(ff, F32), jnp.asarray(fi, F32)


def _head_expand():
    e = np.zeros((LANE, 2 * SSD_INNER), np.float32)
    for h in range(2 * SSD_HEADS):
        e[h, h * SSD_HEAD_DIM:(h + 1) * SSD_HEAD_DIM] = 1.0
    return jnp.asarray(e, BF16)


def _group_mean():
    g = np.kron(np.eye(LANE // ATTN_HEAD_DIM), np.ones((ATTN_HEAD_DIM, ATTN_HEAD_DIM))) / ATTN_HEAD_DIM
    return jnp.asarray(g, F32)


def _relayout_w_in(w):
    cols = [w[:, OFF_HY:OFF_HY + 3 * HY_WIDTH], w[:, OFF_Q:OFF_Q + ATTN_INNER],
            w[:, OFF_XB:OFF_XB + SSD_XB], w[:, OFF_C:OFF_C + SSD_GN], w[:, OFF_Z:OFF_Z + SSD_INNER],
            w[:, OFF_K:OFF_K + ATTN_KV_INNER], w[:, OFF_V:OFF_V + ATTN_KV_INNER],
            w[:, OFF_DT:OFF_DT + 2 * SSD_HEADS]]
    wr = jnp.concatenate(cols, axis=1)
    return jnp.pad(wr, ((0, 0), (0, NP - wr.shape[1]))).astype(BF16)


def _pad_row(v):
    v = v.reshape(1, -1)
    return jnp.pad(v, ((0, 0), (0, LANE - v.shape[1])))


def _layer_params(l, raw, tables):
    p = {k: v[l] for k, v in raw.items()}
    p["w_in_r"] = _relayout_w_in(p["w_in"])
    p["dt_bias_row"] = _pad_row(p["ssd_dt_bias"])
    p["a_log_row"] = _pad_row(p["ssd_a_log"])
    p["d_row"] = jnp.repeat(p["ssd_d"], SSD_HEAD_DIM)[None]
    p["head_expand"] = tables["head_expand"]
    p["gq"] = jnp.tile(p["q_norm_g"], LANE // ATTN_HEAD_DIM)[None]
    p["gk"] = jnp.tile(p["k_norm_g"], LANE // ATTN_HEAD_DIM)[None]
    p["hy_w1p"] = jnp.pad(p["hy_w1"], ((0, LANE - HY_EMB), (0, 0)))
    p["w_out_b"] = p["w_out"].astype(BF16)
    p["w_gu_b"] = p["w_gu"].astype(BF16)
    p["w_down_b"] = p["w_down"].astype(BF16)
    return p


def _mixers(proj, dt_raw, p, tables, is_ctx, ssd_init):
    yf, yb, sf, sb = _ssd_scan(proj, dt_raw, ssd_init[0], ssd_init[1], p)
    rope = tables["rope_ctx"] if is_ctx else tables["rope"]
    q, k, v = _attn_prep(proj, rope[0], rope[1], p["gq"], p["gk"], tables["group_mean"])
    return (yf, yb, sf, sb), (q, k, v)


def _hyena(proj, p, tables, is_ctx):
    vx, x0 = _hy_conv(proj, p["hy_conv_w"], p["hy_conv_b"])
    seq = proj.shape[1]
    bias = p["hy_bias"][None]
    if is_ctx:
        taps = _hy_taps(*tables["feats_ctx"], p, tables["abs_deltas"], seq)
        return _hy_ctx(vx, x0, taps, tables["dft_ctx"][0], tables["dft_ctx"][1], bias)
    t_fwd, t_inv, fs, fc = tables["fft"]
    taps = _hy_taps(*tables["feats"], p, tables["abs_deltas"], seq)
    kre, kim = _fft2_filter(*_fft1(taps[None], t_fwd, True), fs)
    are, aim = _fft1(vx, t_fwd, False)
    zre, zim = _fft2(are, aim, kre, kim, fs, fc)
    return _ifft1(zre, zim, t_inv, vx, x0, bias)


def _tail(x, proj, ssd, at, hy, mod, p, final_g, final):
    x = _mix_out(x, ssd[0], ssd[1], proj, at, hy, mod[2], p["ssd_norm_g"][None], p["attn_norm_g"][None],
                 p["hy_norm_g"][None], p["w_out_b"])
    return _ffn(x, mod[3], mod[4], mod[5], p["norm2_g"][None], p["w_gu_b"], p["w_down_b"], final_g[None], final)


def _layer(x, xc, mod_rows, p, tables, final_g, last):
    b = x.shape[0]
    mod_x = [mod_rows[:b, i * D_MODEL:(i + 1) * D_MODEL][:, None, :] for i in range(N_MOD)]
    mod_c = [jnp.broadcast_to(mod_rows[b:b + 1, i * D_MODEL:(i + 1) * D_MODEL][:, None, :], (b, 1, D_MODEL))
             for i in range(N_MOD)]
    g1 = p["norm1_g"][None]
    proj, dt_raw = _proj_in(x, mod_x[0], mod_x[1], g1, p["w_in_r"])
    projc, dt_raw_c = _proj_in(xc, mod_c[0], mod_c[1], g1, p["w_in_r"])
    zeros = jnp.zeros((b, SSD_STATE, SSD_INNER), F32)
    ssd_c, qkv_c = _mixers(projc, dt_raw_c, p, tables, True, (zeros, zeros))
    ssd_x, qkv_x = _mixers(proj, dt_raw, p, tables, False, (ssd_c[2], ssd_c[3]))
    k_all = jnp.concatenate([qkv_c[1], qkv_x[1]], axis=2)
    v_all = jnp.concatenate([qkv_c[2], qkv_x[2]], axis=3)
    at = _flash(qkv_x[0], k_all, v_all)
    hy = _hyena(proj, p, tables, False)
    x = _tail(x, proj, ssd_x, at, hy, mod_x, p, final_g, last)
    if last:
        return x, None
    at_c = _flash(qkv_c[0], qkv_c[1], qkv_c[2])
    hy_c = _hyena(projc, p, tables, True)
    xc = _tail(xc, projc, ssd_c, at_c, hy_c, mod_c, p, final_g, False)
    return x, xc


def kernel(x, c, ctx, c_ctx, w_mod, b_mod, norm1_g, w_in, ssd_conv_w, ssd_conv_b, ssd_a_log, ssd_dt_bias, ssd_d, ssd_norm_g, q_norm_g, k_norm_g, attn_norm_g, hy_conv_w, hy_conv_b, hy_w1, hy_b1, hy_freq, hy_w2, hy_b2, hy_w3, hy_bias, hy_norm_g, w_out, norm2_g, w_gu, w_down, final_g):
    b, seq, _ = x.shape
    ctx_len = ctx.shape[1]
    depth = w_mod.shape[0]
    assert 2 * seq == FFT_R * FFT_R and b + 1 <= SUBLANE
    raw = dict(w_in=w_in, ssd_conv_w=ssd_conv_w, ssd_conv_b=ssd_conv_b, ssd_a_log=ssd_a_log,
               ssd_dt_bias=ssd_dt_bias, ssd_d=ssd_d, ssd_norm_g=ssd_norm_g, q_norm_g=q_norm_g,
               k_norm_g=k_norm_g, attn_norm_g=attn_norm_g, hy_conv_w=hy_conv_w, hy_conv_b=hy_conv_b,
               hy_w1=hy_w1, hy_b1=hy_b1, hy_freq=hy_freq, hy_w2=hy_w2, hy_b2=hy_b2, hy_w3=hy_w3,
               hy_bias=hy_bias, hy_norm_g=hy_norm_g, w_out=w_out, norm2_g=norm2_g, w_gu=w_gu, w_down=w_down,
               norm1_g=norm1_g)
    ones = jnp.ones((ctx_len, LANE), F32)
    tables = dict(rope=_rope_tables(seq), rope_ctx=(ones, jnp.zeros_like(ones)),
                  feats=_hy_feats(seq), feats_ctx=_hy_feats(ctx_len), abs_deltas=_hy_abs_deltas(),
                  fft=_fft_tables(), dft_ctx=_dense_dft_tables(ctx_len),
                  head_expand=_head_expand(), group_mean=_group_mean())
    c_rows = jnp.concatenate([c, c_ctx[None], jnp.zeros((SUBLANE - b - 1, D_MODEL), F32)], axis=0)
    xc = ctx
    for l in range(depth):
        p = _layer_params(l, raw, tables)
        mod_rows = _mod_call(c_rows, w_mod[l], b_mod[l][None])
        x, xc = _layer(x, xc, mod_rows, p, tables, final_g, l == depth - 1)
    return x
```

```python
import functools
import math

import numpy as np
import jax
import jax.numpy as jnp
from jax import lax
from jax.experimental import pallas as pl
from jax.experimental.pallas import tpu as pltpu

F32 = jnp.float32
BF16 = jnp.bfloat16
HI = lax.Precision.HIGHEST

D_MODEL = 1024
GRID_W = 64
EPS = 1e-6
SSD_HEADS = 8
SSD_HEAD_DIM = 64
SSD_INNER = 512
SSD_STATE = 128
SSD_CHUNK = 128
SSD_GN = 256
SSD_XB = SSD_INNER + SSD_GN
ATTN_HEADS = 8
ATTN_KV_HEADS = 2
ATTN_HEAD_DIM = 64
ATTN_INNER = 512
ATTN_KV_INNER = 128
ATTN_SCALE = ATTN_HEAD_DIM ** -0.5
LOG2E = math.log2(math.e)
ATTN_V_ROWS = 80
ROPE_THETA = 10000.0
ROPE_AXIS_DIM = ATTN_HEAD_DIM // 2
HY_WIDTH = 512
HY_BANDS = 16
HY_EMB = 1 + 2 * HY_BANDS
HY_HIDDEN = 64
HY_FAST_DECAY_PCT = 0.3
HY_SLOW_DECAY_PCT = 1.5
HY_TARGET = 1e-2
D_MIX = 1536
D_FF = 2816
N_MOD = 6

OFF_K = 0
OFF_V = OFF_K + ATTN_KV_INNER
OFF_XB = OFF_V + ATTN_KV_INNER
OFF_DT = OFF_XB + SSD_XB
OFF_C = OFF_DT + 2 * SSD_HEADS
OFF_Q = OFF_C + SSD_GN
OFF_Z = OFF_Q + ATTN_INNER
OFF_HY = OFF_Z + SSD_INNER

P_HY = 0
P_Q = 1536
P_X = 2048
P_B = 2560
P_C = 2816
P_Z = 3072
P_K = 3584
P_V = 3712
P_DT = 3840
NP = 4096

LANE = 128
SUBLANE = 8
HALO = 16
VMEM_LIMIT = 48 * 1024 * 1024

FFT_R = 128
FFT_G = 8
FFT_K1 = 72


def _cp(*sem):
    return pltpu.CompilerParams(dimension_semantics=sem, vmem_limit_bytes=VMEM_LIMIT)


def _silu(x):
    return x * (1.0 / (1.0 + jnp.exp(-x)))


def _softplus(x):
    return jnp.maximum(x, 0.0) + jnp.log(1.0 + jnp.exp(-jnp.abs(x)))


def _rms(x, g):
    return x * lax.rsqrt(jnp.mean(x * x, axis=-1, keepdims=True) + EPS) * g


def _dot(a, b):
    return jnp.dot(a, b, preferred_element_type=F32)


def _dot_hi(a, b):
    return jnp.dot(a, b, precision=HI, preferred_element_type=F32)


def _split_bf16(x):
    hi = x.astype(BF16)
    return hi, (x - hi.astype(F32)).astype(BF16)


def _dot3_table(t_hl, x):
    m = t_hl.shape[0] // 2
    x_hi, x_lo = _split_bf16(x)
    a = _dot(t_hl, x_hi)
    return a[:m] + a[m:] + _dot(t_hl[:m], x_lo)


def _dot1_table(t_hi, x):
    return _dot(t_hi, x.astype(BF16))


def _mod_kernel(c_ref, w_ref, b_ref, o_ref):
    o_ref[...] = _dot_hi(_silu(c_ref[...]), w_ref[...]) + b_ref[...]


def _mod_call(c_rows, w, b):
    n = w.shape[1]
    tn = 1024
    return pl.pallas_call(
        _mod_kernel,
        grid=(n // tn,),
        in_specs=[pl.BlockSpec((SUBLANE, D_MODEL), lambda j: (0, 0)),
                  pl.BlockSpec((D_MODEL, tn), lambda j: (0, j)),
                  pl.BlockSpec((1, tn), lambda j: (0, j))],
        out_specs=pl.BlockSpec((SUBLANE, tn), lambda j: (0, j)),
        out_shape=jax.ShapeDtypeStruct((SUBLANE, n), F32),
        compiler_params=_cp("parallel"),
        name="adaln_mod",
    )(c_rows, w, b)


def _proj_kernel(x_ref, sh_ref, sc_ref, g_ref, w_ref, o_ref, dt_ref, h_ref, *, dt_tile, dt_off):
    j = pl.program_id(2)

    @pl.when(j == 0)
    def _():
        y = _rms(x_ref[0], g_ref[...])
        h_ref[...] = (y * (1.0 + sc_ref[0]) + sh_ref[0]).astype(BF16)

    r = _dot(h_ref[...], w_ref[...])
    o_ref[0] = r.astype(BF16)

    @pl.when(j == dt_tile)
    def _():
        dt_ref[0] = r[:, dt_off:dt_off + LANE]


def _proj_in(x, sh, sc, g, w):
    b, l, _ = x.shape
    tm = min(l, 1024)
    tn = 512
    return pl.pallas_call(
        functools.partial(_proj_kernel, dt_tile=P_DT // tn, dt_off=P_DT % tn),
        grid=(b, l // tm, NP // tn),
        in_specs=[pl.BlockSpec((1, tm, D_MODEL), lambda bi, i, j: (bi, i, 0)),
                  pl.BlockSpec((1, 1, D_MODEL), lambda bi, i, j: (bi, 0, 0)),
                  pl.BlockSpec((1, 1, D_MODEL), lambda bi, i, j: (bi, 0, 0)),
                  pl.BlockSpec((1, D_MODEL), lambda bi, i, j: (0, 0)),
                  pl.BlockSpec((D_MODEL, tn), lambda bi, i, j: (0, j))],
        out_specs=[pl.BlockSpec((1, tm, tn), lambda bi, i, j: (bi, i, j)),
                   pl.BlockSpec((1, tm, LANE), lambda bi, i, j: (bi, i, 0))],
        out_shape=[jax.ShapeDtypeStruct((b, l, NP), BF16), jax.ShapeDtypeStruct((b, l, LANE), F32)],
        scratch_shapes=[pltpu.VMEM((tm, D_MODEL), BF16)],
        compiler_params=_cp("parallel", "parallel", "arbitrary"),
        name="proj_in",
    )(x, sh, sc, g, w)


def _dwconv3(u, prev_row, next_row, w, b):
    tm = u.shape[0]
    ri = lax.broadcasted_iota(jnp.int32, u.shape, 0)
    um = jnp.where(ri == 0, prev_row, pltpu.roll(u, 1, 0))
    up = jnp.where(ri == tm - 1, next_row, pltpu.roll(u, tm - 1, 0))
    return um * w[0:1] + u * w[1:2] + up * w[2:3] + b


def _conv_group(refs, i, n_i):
    u_ref, p_ref, n_ref, w_ref, b_ref = refs
    prev_row = jnp.where(i > 0, p_ref[0].astype(F32)[HALO - 1:HALO, :], 0.0)
    next_row = jnp.where(i < n_i - 1, n_ref[0].astype(F32)[0:1, :], 0.0)
    return _dwconv3(u_ref[0].astype(F32), prev_row, next_row, w_ref[...], b_ref[...])


def _hy_conv_kernel(*refs, group_major):
    i, n_i = pl.program_id(1), pl.num_programs(1)
    v = _conv_group(refs[0:5], i, n_i)
    x1 = _conv_group(refs[5:10], i, n_i)
    x0 = _conv_group(refs[10:15], i, n_i)
    vx_ref, x0_ref = refs[15], refs[16]
    vx = v * x1
    vx_ref[0] = _to_group_major(vx) if group_major else vx
    x0_ref[0] = _to_group_major(x0) if group_major else x0


def _to_group_major(u):
    rows, c = u.shape
    return jnp.swapaxes(u.reshape(rows // FFT_R, FFT_R // FFT_G, FFT_G, c), 0, 1)


def _from_group_major(u):
    g, n1, r, c = u.shape
    return jnp.swapaxes(u, 0, 1).reshape(n1 * g * r, c)


def _conv_specs(tm, tc, l, col_block, w_block):
    nrb = l // HALO
    per = tm // HALO
    return [
        pl.BlockSpec((1, tm, tc), lambda bi, i: (bi, i, col_block)),
        pl.BlockSpec((1, HALO, tc), lambda bi, i: (bi, jnp.maximum(i * per - 1, 0), col_block)),
        pl.BlockSpec((1, HALO, tc), lambda bi, i: (bi, jnp.minimum((i + 1) * per, nrb - 1), col_block)),
        pl.BlockSpec((SUBLANE, tc), lambda bi, i: (0, w_block)),
        pl.BlockSpec((1, tc), lambda bi, i: (0, w_block)),
    ]


def _pad_taps(w):
    return jnp.pad(w, ((0, SUBLANE - w.shape[0]), (0, 0)))


def _hy_conv(proj, conv_w, conv_b, group_major):
    b, l, _ = proj.shape
    tm = min(l, 1024)
    tc = HY_WIDTH
    wp = _pad_taps(conv_w)
    bp = conv_b[None]
    specs, args = [], []
    for grp in range(3):
        specs += _conv_specs(tm, tc, l, P_HY // tc + grp, grp)
        args += [proj, proj, proj, wp, bp]
    if group_major:
        ng = FFT_R // FFT_G
        out_spec = pl.BlockSpec((1, ng, tm // FFT_R, FFT_G, tc), lambda bi, i: (bi, 0, i, 0, 0))
        shp = jax.ShapeDtypeStruct((b, ng, l // FFT_R, FFT_G, tc), F32)
    else:
        out_spec = pl.BlockSpec((1, tm, tc), lambda bi, i: (bi, i, 0))
        shp = jax.ShapeDtypeStruct((b, l, tc), F32)
    return pl.pallas_call(
        functools.partial(_hy_conv_kernel, group_major=group_major),
        grid=(b, l // tm),
        in_specs=specs,
        out_specs=[out_spec, out_spec],
        out_shape=[shp, shp],
        compiler_params=_cp("parallel", "parallel"),
        name="hy_conv",
    )(*args)


def _split_pieces(x, n):
    pieces, r = [], x
    for k in range(n):
        pc = r.astype(BF16)
        pieces.append(pc)
        if k + 1 < n:
            r = r - pc.astype(F32)
    return pieces


def _select_left(sel, x, n):
    w = x.shape[1]
    r = _dot(sel, jnp.concatenate(_split_pieces(x, n), axis=1))
    return sum(r[:, k * w:(k + 1) * w] for k in range(n))


def _select_right(x, sel, n):
    m = x.shape[0]
    r = _dot(jnp.concatenate(_split_pieces(x, n), axis=0), sel)
    return sum(r[k * m:(k + 1) * m] for k in range(n))


def _ssd_chunk(xbc, dt_raw, a, bias, d_row, e_d, st, lane0, fwd):
    q = SSD_CHUNK
    xs = xbc[:, :SSD_INNER]
    bm = xbc[:, SSD_INNER:SSD_XB]
    cm = xbc[:, SSD_XB:]
    ri = lax.broadcasted_iota(jnp.int32, (q, q), 0)
    ci = lax.broadcasted_iota(jnp.int32, (q, q), 1)
    dt = _softplus(dt_raw + bias)
    adt = dt * a
    cs = _select_left((ci <= ri).astype(BF16), adt, 3)
    tot = cs[q - 1:q, :]
    if fwd:
        key = cs
        w_c = dt * jnp.exp(tot - key)
        e_c = jnp.exp(key)
        mask = ci <= ri
    else:
        key = cs - adt
        w_c = dt * jnp.exp(key)
        e_c = jnp.exp(tot - key)
        mask = ci >= ri
    key_t = key.T
    dt_t = dt.T
    dec_c = jnp.broadcast_to(jnp.exp(tot), (2 * SUBLANE, LANE))
    spread = _select_right(jnp.concatenate([w_c, e_c, dec_c], axis=0), e_d, 2)
    w_e, e_off, decay = spread[:q], spread[q:2 * q], spread[2 * q:2 * q + 1]
    x_b = xs.astype(BF16)
    x_w = (xs * w_e).astype(BF16)
    st_b = st.astype(BF16)
    y_parts, st_parts, off_parts = [], [], []
    hg = SSD_HEADS // 2
    for g in range(2):
        bg = bm[:, g * SSD_STATE:(g + 1) * SSD_STATE]
        cg = cm[:, g * SSD_STATE:(g + 1) * SSD_STATE].astype(BF16)
        bg_t = bg.T.astype(BF16)
        gmat = _dot(cg, bg_t)
        gs = slice(g * hg * SSD_HEAD_DIM, (g + 1) * hg * SSD_HEAD_DIM)
        off_parts.append(_dot(cg, st_b[:, gs]))
        st_parts.append(_dot(bg_t, x_w[:, gs]))
        for hh in range(hg):
            h = g * hg + hh
            col = key[:, lane0 + h:lane0 + h + 1]
            row = key_t[lane0 + h:lane0 + h + 1, :]
            diff = (col - row) if fwd else (row - col)
            lm = jnp.exp(jnp.where(mask, diff, -1e30)) * dt_t[lane0 + h:lane0 + h + 1, :]
            s = (gmat * lm).astype(BF16)
            y_parts.append(_dot(s, x_b[:, h * SSD_HEAD_DIM:(h + 1) * SSD_HEAD_DIM]))
    y = jnp.concatenate(y_parts, axis=1) + jnp.concatenate(off_parts, axis=1) * e_off
    if fwd:
        y = y + d_row * xs
    st_new = st * decay + jnp.concatenate(st_parts, axis=1)
    return y, st_new


def _ssd_kernel(xf_ref, xfp_ref, xfn_ref, xb_ref, xbp_ref, xbn_ref, dtf_ref, dtb_ref, cw_ref, cb_ref,
                bias_ref, alog_ref, d_ref, e_ref, sf0_ref, sb0_ref, yf_ref, yb_ref, sf_ref, sb_ref, stf, stb):
    c = pl.program_id(1)
    nc = pl.num_programs(1)

    @pl.when(c == 0)
    def _():
        stf[...] = sf0_ref[0]
        stb[...] = sb0_ref[0]

    def conv_silu(u_ref, p_ref, n_ref, chunk):
        prev_row = jnp.where(chunk > 0, p_ref[0].astype(F32)[HALO - 1:HALO, :], 0.0)
        next_row = jnp.where(chunk < nc - 1, n_ref[0].astype(F32)[0:1, :], 0.0)
        return _silu(_dwconv3(u_ref[0].astype(F32), prev_row, next_row, cw_ref[...], cb_ref[...]))

    a = -jnp.exp(alog_ref[...])
    bias = bias_ref[...]
    xf = conv_silu(xf_ref, xfp_ref, xfn_ref, c)
    xb = conv_silu(xb_ref, xbp_ref, xbn_ref, nc - 1 - c)
    yf, sf = _ssd_chunk(xf, dtf_ref[0], a, bias, d_ref[...], e_ref[:, :SSD_INNER], stf[...], 0, True)
    yb, sb = _ssd_chunk(xb, dtb_ref[0], a, bias, d_ref[...], e_ref[:, SSD_INNER:], stb[...], SSD_HEADS, False)
    yf_ref[0] = yf
    yb_ref[0] = yb
    stf[...] = sf
    stb[...] = sb
    sf_ref[0] = sf
    sb_ref[0] = sb


def _ssd_scan(proj, dt_raw, sf0, sb0, p):
    b, l, _ = proj.shape
    nc = l // SSD_CHUNK
    q = SSD_CHUNK
    w = 2 * SSD_INNER
    xcol = P_X // w
    per = q // HALO
    nrb = l // HALO
    st_spec = pl.BlockSpec((1, SSD_STATE, SSD_INNER), lambda bi, c: (bi, 0, 0))
    y_shape = jax.ShapeDtypeStruct((b, l, SSD_INNER), F32)
    st_shape = jax.ShapeDtypeStruct((b, SSD_STATE, SSD_INNER), F32)
    row = lambda n: pl.BlockSpec((1, n), lambda bi, c: (0, 0))

    def chunk_specs(chunk_of):
        return [pl.BlockSpec((1, q, w), lambda bi, c: (bi, chunk_of(c), xcol)),
                pl.BlockSpec((1, HALO, w), lambda bi, c: (bi, jnp.maximum(chunk_of(c) * per - 1, 0), xcol)),
                pl.BlockSpec((1, HALO, w), lambda bi, c: (bi, jnp.minimum((chunk_of(c) + 1) * per, nrb - 1), xcol))]

    fwd_of = lambda c: c
    bwd_of = lambda c: nc - 1 - c
    return pl.pallas_call(
        _ssd_kernel,
        grid=(b, nc),
        in_specs=chunk_specs(fwd_of) + chunk_specs(bwd_of) + [
            pl.BlockSpec((1, q, LANE), lambda bi, c: (bi, c, 0)),
            pl.BlockSpec((1, q, LANE), lambda bi, c: (bi, nc - 1 - c, 0)),
            pl.BlockSpec((SUBLANE, w), lambda bi, c: (0, 0)), row(w),
            row(LANE), row(LANE), row(SSD_INNER),
            pl.BlockSpec((LANE, 2 * SSD_INNER), lambda bi, c: (0, 0)),
            st_spec, st_spec],
        out_specs=[pl.BlockSpec((1, q, SSD_INNER), lambda bi, c: (bi, c, 0)),
                   pl.BlockSpec((1, q, SSD_INNER), lambda bi, c: (bi, nc - 1 - c, 0)),
                   st_spec, st_spec],
        out_shape=[y_shape, y_shape, st_shape, st_shape],
        scratch_shapes=[pltpu.VMEM((SSD_STATE, SSD_INNER), F32), pltpu.VMEM((SSD_STATE, SSD_INNER), F32)],
        compiler_params=_cp("parallel", "arbitrary"),
        name="ssd_scan",
    )(proj, proj, proj, proj, proj, proj, dt_raw, dt_raw, _pad_taps(p["ssd_conv_w"]), p["ssd_conv_b"][None],
      p["dt_bias_row"], p["a_log_row"], p["d_row"], p["head_expand"], sf0, sb0)


def _attn_prep_kernel(q_ref, k_ref, v_ref, cos_ref, sin_ref, gq_ref, gk_ref, gm_ref, qo_ref, ko_ref, vo_ref):
    cos = cos_ref[...]
    sin = sin_ref[...]
    gm = gm_ref[...]
    lane = lax.broadcasted_iota(jnp.int32, cos.shape, 1)
    first = jnp.bitwise_and(lane, 31) < 16
    hd = ATTN_HEAD_DIM

    def norm_rope(t, g):
        ms = _dot_hi(t * t, gm)
        y = t * lax.rsqrt(ms + EPS) * g
        partner = jnp.where(first, pltpu.roll(y, LANE - 16, 1), pltpu.roll(y, 16, 1))
        return y * cos + partner * sin

    for s in range(ATTN_INNER // LANE):
        qs = norm_rope(q_ref[0, :, s * LANE:(s + 1) * LANE].astype(F32), gq_ref[...]) * (ATTN_SCALE * LOG2E)
        qt = qs.T.astype(BF16)
        qo_ref[0, 2 * s] = qt[:hd]
        qo_ref[0, 2 * s + 1] = qt[hd:]
    ks = norm_rope(k_ref[0].astype(F32), gk_ref[...])
    ko_ref[0, 0] = ks[:, :hd].astype(BF16)
    ko_ref[0, 1] = ks[:, hd:].astype(BF16)
    vt = v_ref[0].astype(F32).T.astype(BF16)
    tm = vt.shape[1]
    pad_rows = lax.broadcasted_iota(jnp.int32, (ATTN_V_ROWS - hd, tm), 0)
    tail = jnp.where(pad_rows == 0, 1.0, 0.0).astype(BF16)
    for g in range(ATTN_KV_HEADS):
        vo_ref[0, g, 0:hd, :] = vt[g * hd:(g + 1) * hd]
        vo_ref[0, g, hd:ATTN_V_ROWS, :] = tail


def _attn_prep(proj, cos, sin, gq, gk, gm):
    b, l, _ = proj.shape
    tm = min(l, 1024)
    hd = ATTN_HEAD_DIM
    const = lambda r, c: pl.BlockSpec((r, c), lambda bi, i: (0, 0))
    return pl.pallas_call(
        _attn_prep_kernel,
        grid=(b, l // tm),
        in_specs=[pl.BlockSpec((1, tm, ATTN_INNER), lambda bi, i: (bi, i, P_Q // ATTN_INNER)),
                  pl.BlockSpec((1, tm, LANE), lambda bi, i: (bi, i, P_K // LANE)),
                  pl.BlockSpec((1, tm, LANE), lambda bi, i: (bi, i, P_V // LANE)),
                  pl.BlockSpec((tm, LANE), lambda bi, i: (i, 0)),
                  pl.BlockSpec((tm, LANE), lambda bi, i: (i, 0)),
                  const(1, LANE), const(1, LANE), const(LANE, LANE)],
        out_specs=[pl.BlockSpec((1, ATTN_HEADS, hd, tm), lambda bi, i: (bi, 0, 0, i)),
                   pl.BlockSpec((1, ATTN_KV_HEADS, tm, hd), lambda bi, i: (bi, 0, i, 0)),
                   pl.BlockSpec((1, ATTN_KV_HEADS, ATTN_V_ROWS, tm), lambda bi, i: (bi, 0, 0, i))],
        out_shape=[jax.ShapeDtypeStruct((b, ATTN_HEADS, hd, l), BF16),
                   jax.ShapeDtypeStruct((b, ATTN_KV_HEADS, l, hd), BF16),
                   jax.ShapeDtypeStruct((b, ATTN_KV_HEADS, ATTN_V_ROWS, l), BF16)],
        compiler_params=_cp("parallel", "parallel"),
        name="attn_prep",
    )(proj, proj, proj, cos, sin, gq, gk, gm)


def _flash_kernel(q_ref, k_ref, v_ref, o_ref, s_ref, m_ref, acc_ref, *, tk, nk):
    r = ATTN_HEADS // ATTN_KV_HEADS
    hd = ATTN_HEAD_DIM

    def scores(j, h, slot):
        start = pl.multiple_of(j * tk, tk)
        s_ref[slot] = _dot(k_ref[0, 0, pl.ds(start, tk), :], q_ref[0, h])

    m_ref[...] = jnp.full(m_ref.shape, -1e30, F32)
    acc_ref[...] = jnp.zeros(acc_ref.shape, F32)
    scores(0, 0, 0)

    def body(j, _):
        start = pl.multiple_of(j * tk, tk)
        vs = v_ref[0, 0, :, pl.ds(start, tk)]
        j_next = jnp.minimum(j + 1, nk - 1)
        for h in range(r):
            if h < r - 1:
                scores(j, h + 1, (h + 1) % 2)
            else:
                scores(j_next, 0, 0)
            s = s_ref[h % 2]
            m = m_ref[h]
            mn = jnp.maximum(m, jnp.max(s, axis=0, keepdims=True))
            alpha = jnp.exp2(m - mn)
            p = jnp.exp2(s - mn).astype(BF16)
            m_ref[h] = mn
            acc_ref[h] = alpha * acc_ref[h] + _dot(vs, p)
        return 0

    lax.fori_loop(0, nk, body, 0)
    outs = []
    for h in range(r):
        acc = acc_ref[h]
        outs.append((acc[:hd] * (1.0 / acc[hd:hd + 1])).T)
    o_ref[0] = jnp.concatenate(outs, axis=1)


def _flash(q, k, v):
    b, _, hd, l = q.shape
    lk = k.shape[2]
    r = ATTN_HEADS // ATTN_KV_HEADS
    assert r % 2 == 0
    tq = min(l, 512)
    tk = 768 if lk % 768 == 0 else 256
    return pl.pallas_call(
        functools.partial(_flash_kernel, tk=tk, nk=lk // tk),
        grid=(b, ATTN_KV_HEADS, l // tq),
        in_specs=[pl.BlockSpec((1, r, hd, tq), lambda bi, g, i: (bi, g, 0, i)),
                  pl.BlockSpec((1, 1, lk, hd), lambda bi, g, i: (bi, g, 0, 0)),
                  pl.BlockSpec((1, 1, ATTN_V_ROWS, lk), lambda bi, g, i: (bi, g, 0, 0))],
        out_specs=pl.BlockSpec((1, tq, r * hd), lambda bi, g, i: (bi, i, g)),
        out_shape=jax.ShapeDtypeStruct((b, l, ATTN_INNER), F32),
        scratch_shapes=[pltpu.VMEM((2, tk, tq), F32), pltpu.VMEM((r, 1, tq), F32),
                        pltpu.VMEM((r, ATTN_V_ROWS, tq), F32)],
        compiler_params=_cp("parallel", "parallel", "parallel"),
        name="flash_gqa",
    )(q, k, v)


def _taps_kernel(ft_ref, t_ref, w1_ref, b1_ref, fr_ref, w2_ref, b2_ref, w3_ref, adel_ref, o_ref, *, group_major):
    fr = fr_ref[...]
    h = jnp.sin(fr * (_dot_hi(w1_ref[...], ft_ref[...]) + b1_ref[...]))
    h = jnp.sin(fr * (_dot_hi(w2_ref[...], h) + b2_ref[...]))
    y = _dot_hi(h.T, w3_ref[...])
    decay = jnp.exp(-t_ref[...] * adel_ref[...])
    f0 = y[:, :HY_WIDTH] * decay
    f1 = y[:, HY_WIDTH:] * decay
    lag = pl.program_id(0) * f1.shape[0] + lax.broadcasted_iota(jnp.int32, f1.shape, 0)
    f1 = jnp.where(lag == 0, 0.0, f1)
    o_ref[0] = _to_group_major(f0) if group_major else f0
    o_ref[1] = _to_group_major(f1) if group_major else f1


def _hy_taps(feats_t, t_col, p, adel, seq, group_major):
    tm = min(seq, 1024)
    const = lambda r, c: pl.BlockSpec((r, c), lambda i: (0, 0))
    if group_major:
        ng = FFT_R // FFT_G
        out = pl.BlockSpec((2, ng, tm // FFT_R, FFT_G, HY_WIDTH), lambda i: (0, 0, i, 0, 0))
        shp = jax.ShapeDtypeStruct((2, ng, seq // FFT_R, FFT_G, HY_WIDTH), F32)
    else:
        out = pl.BlockSpec((2, tm, HY_WIDTH), lambda i: (0, i, 0))
        shp = jax.ShapeDtypeStruct((2, seq, HY_WIDTH), F32)
    return pl.pallas_call(
        functools.partial(_taps_kernel, group_major=group_major),
        grid=(seq // tm,),
        in_specs=[pl.BlockSpec((LANE, tm), lambda i: (0, i)),
                  pl.BlockSpec((tm, 1), lambda i: (i, 0)),
                  const(HY_HIDDEN, LANE), const(HY_HIDDEN, 1), const(HY_HIDDEN, 1),
                  const(HY_HIDDEN, HY_HIDDEN), const(HY_HIDDEN, 1),
                  const(HY_HIDDEN, 2 * HY_WIDTH), const(1, HY_WIDTH)],
        out_specs=out,
        out_shape=shp,
        compiler_params=_cp("parallel"),
        name="hy_taps",
    )(feats_t, t_col, p["hy_w1p"].T, p["hy_b1"][:, None], p["hy_freq"][:, None], p["hy_w2"].T,
      p["hy_b2"][:, None], p["hy_w3"], adel)


def _fft1_kernel(x_ref, t_ref, are_ref, aim_ref, *, precise):
    g = pl.program_id(2)
    half = FFT_R // 2
    for r in range(FFT_G):
        t = t_ref[g * FFT_G + r]
        x = x_ref[0, pl.ds(r, half, stride=FFT_G), :]
        a = _dot3_table(t, x) if precise else _dot1_table(t, x)
        rows = pl.ds(r, FFT_K1, stride=FFT_G)
        are_ref[0, rows, :] = a[:FFT_K1]
        aim_ref[0, rows, :] = a[FFT_K1:]


def _fft1(x, table, precise):
    nb, ng, rows, c = x.shape
    spec_o = pl.BlockSpec((1, None, FFT_K1 * FFT_G, LANE), lambda bi, j, g: (bi, g, 0, j))
    shp = jax.ShapeDtypeStruct((nb, ng, FFT_K1 * FFT_G, c), F32)
    return pl.pallas_call(
        functools.partial(_fft1_kernel, precise=precise),
        grid=(nb, c // LANE, ng),
        in_specs=[pl.BlockSpec((1, None, rows, LANE), lambda bi, j, g: (bi, g, 0, j)),
                  pl.BlockSpec(table.shape, lambda bi, j, g: (0, 0, 0))],
        out_specs=[spec_o, spec_o],
        out_shape=[shp, shp],
        compiler_params=_cp("parallel", "parallel", "parallel"),
        name="hy_fft1",
    )(x, table)


def _k1_rows(ref, b, kk):
    blk = ref[b, :, kk]
    return blk.reshape(blk.shape[0] * blk.shape[1], blk.shape[2])


def _fft2_filter_kernel(are_ref, aim_ref, fs_ref, kre_ref, kim_ref):
    r_ = FFT_R
    for kk in range(SUBLANE):
        xc = _dot3_table(fs_ref[...], jnp.concatenate([_k1_rows(are_ref, 0, kk), _k1_rows(aim_ref, 0, kk)], axis=0))
        xa = _dot3_table(fs_ref[...], jnp.concatenate([_k1_rows(are_ref, 1, kk), _k1_rows(aim_ref, 1, kk)], axis=0))
        kre_ref[kk] = xc[:r_] + xa[:r_]
        kim_ref[kk] = xc[r_:] - xa[r_:]


def _fft2_filter(are, aim, fs):
    r_ = FFT_R
    nb, ng, _, c = are.shape
    view = lambda a: a.reshape(nb, ng, FFT_K1, FFT_G, c)
    spec_i = pl.BlockSpec((nb, ng, SUBLANE, FFT_G, c), lambda kg: (0, 0, kg, 0, 0))
    spec_o = pl.BlockSpec((SUBLANE, r_, c), lambda kg: (kg, 0, 0))
    shp = jax.ShapeDtypeStruct((FFT_K1, r_, c), F32)
    return pl.pallas_call(
        _fft2_filter_kernel,
        grid=(FFT_K1 // SUBLANE,),
        in_specs=[spec_i, spec_i, pl.BlockSpec((4 * r_, 2 * r_), lambda kg: (0, 0))],
        out_specs=[spec_o, spec_o],
        out_shape=[shp, shp],
        compiler_params=_cp("parallel"),
        name="hy_fft2_filter",
    )(view(are), view(aim), fs)


def _fft2_kernel(are_ref, aim_ref, kre_ref, kim_ref, fs_ref, fc_ref, zre_ref, zim_ref):
    r_ = FFT_R
    ng = are_ref.shape[1]
    for kk in range(SUBLANE):
        a = jnp.concatenate([_k1_rows(are_ref, 0, kk), _k1_rows(aim_ref, 0, kk)], axis=0)
        b = _dot1_table(fs_ref[...], a)
        br, bi = b[:r_], b[r_:]
        kr, ki = kre_ref[kk], kim_ref[kk]
        y = jnp.concatenate([br * kr - bi * ki, br * ki + bi * kr], axis=0)
        z = _dot1_table(fc_ref[...], y)
        zre_ref[0, :, kk] = z[:r_].reshape(ng, FFT_G, z.shape[1])
        zim_ref[0, :, kk] = z[r_:].reshape(ng, FFT_G, z.shape[1])


def _fft2(are, aim, kre, kim, fs, fc):
    nb, ng, rows, c = are.shape
    r_ = FFT_R
    view = lambda a: a.reshape(nb, ng, FFT_K1, FFT_G, c)
    spec_a = pl.BlockSpec((1, ng, SUBLANE, FFT_G, c), lambda bi, kg: (bi, 0, kg, 0, 0))
    spec_k = pl.BlockSpec((SUBLANE, r_, c), lambda bi, kg: (kg, 0, 0))
    spec_f = pl.BlockSpec((2 * r_, 2 * r_), lambda bi, kg: (0, 0))
    shp = jax.ShapeDtypeStruct((nb, ng, FFT_K1, FFT_G, c), F32)
    zre, zim = pl.pallas_call(
        _fft2_kernel,
        grid=(nb, FFT_K1 // SUBLANE),
        in_specs=[spec_a, spec_a, spec_k, spec_k, spec_f, spec_f],
        out_specs=[spec_a, spec_a],
        out_shape=[shp, shp],
        compiler_params=_cp("parallel", "parallel"),
        name="hy_fft2",
    )(view(are), view(aim), kre, kim, fs, fc)
    return zre.reshape(are.shape), zim.reshape(are.shape)


def _ifft1_kernel(zre_ref, zim_ref, t_ref, vx_ref, x0_ref, bias_ref, o_ref):
    g = pl.program_id(2)
    n1 = o_ref.shape[1] // FFT_G
    for r in range(FFT_G):
        zrows = pl.ds(r, FFT_K1, stride=FFT_G)
        z = jnp.concatenate([zre_ref[0, zrows, :], zim_ref[0, zrows, :]], axis=0)
        conv = _dot1_table(t_ref[g * FFT_G + r], z)
        rows = pl.ds(r, n1, stride=FFT_G)
        o_ref[0, rows, :] = x0_ref[0, rows, :] * (conv + bias_ref[...] * vx_ref[0, rows, :])


def _ifft1(zre, zim, table, vx, x0, bias):
    nb, ng, rows, c = vx.shape
    spec_z = pl.BlockSpec((1, None, FFT_K1 * FFT_G, LANE), lambda bi, j, g: (bi, g, 0, j))
    spec_x = pl.BlockSpec((1, None, rows, LANE), lambda bi, j, g: (bi, g, 0, j))
    return pl.pallas_call(
        _ifft1_kernel,
        grid=(nb, c // LANE, ng),
        in_specs=[spec_z, spec_z,
                  pl.BlockSpec(table.shape, lambda bi, j, g: (0, 0, 0)),
                  spec_x, spec_x,
                  pl.BlockSpec((1, LANE), lambda bi, j, g: (0, j))],
        out_specs=spec_x,
        out_shape=jax.ShapeDtypeStruct(vx.shape, F32),
        compiler_params=_cp("parallel", "parallel", "parallel"),
        name="hy_ifft1",
    )(zre, zim, table, vx, x0, bias)


def _hy_ctx_kernel(vx_ref, x0_ref, taps_ref, ff_ref, fi_ref, bias_ref, o_ref):
    vx = vx_ref[0]
    seq = vx.shape[0]
    n = 2 * seq
    u = _dot_hi(ff_ref[:, :seq], vx)
    k = _dot_hi(ff_ref[...], taps_ref[...])
    ur, ui = u[:n], u[n:]
    kr, ki = k[:n], k[n:]
    y = jnp.concatenate([ur * kr - ui * ki, ur * ki + ui * kr], axis=0)
    conv = _dot_hi(fi_ref[...], y)
    o_ref[0] = x0_ref[0] * (conv + bias_ref[...] * vx)


def _hy_ctx(vx, x0, taps, ff, fi, bias):
    b, seq, c = vx.shape
    n = 2 * seq
    spec_x = pl.BlockSpec((1, seq, c), lambda bi: (bi, 0, 0))
    const = lambda r, cc: pl.BlockSpec((r, cc), lambda bi: (0, 0))
    return pl.pallas_call(
        _hy_ctx_kernel,
        grid=(b,),
        in_specs=[spec_x, spec_x, const(n, c), const(2 * n, n), const(seq, 2 * n), const(1, c)],
        out_specs=spec_x,
        out_shape=jax.ShapeDtypeStruct((b, seq, c), F32),
        compiler_params=_cp("parallel"),
        name="hy_ctx",
    )(vx, x0, taps, ff, fi, bias)


def _mix_kernel(x_ref, yf_ref, yb_ref, z_ref, at_ref, hy_ref, g1_ref, gs_ref, ga_ref, gh_ref, w_ref, o_ref):
    ys = _rms((yf_ref[0] + yb_ref[0]) * _silu(z_ref[0].astype(F32)), gs_ref[...]).astype(BF16)
    ya = _rms(at_ref[0], ga_ref[...]).astype(BF16)
    hy = hy_ref[0]
    if hy.ndim == 4:
        hy = _from_group_major(hy)
    yh = _rms(hy, gh_ref[...]).astype(BF16)
    r = (_dot(ys, w_ref[0:SSD_INNER, :])
         + _dot(ya, w_ref[SSD_INNER:SSD_INNER + ATTN_INNER, :])
         + _dot(yh, w_ref[SSD_INNER + ATTN_INNER:, :]))
    o_ref[0] = x_ref[0] + g1_ref[0] * r


def _mix_out(x, yf, yb, proj, at, hy, g1, gs, ga, gh, w):
    b, l, _ = x.shape
    tm = min(l, 512)
    c = SSD_INNER
    t512 = lambda col: pl.BlockSpec((1, tm, c), lambda bi, i: (bi, i, col))
    const = lambda r, cc: pl.BlockSpec((r, cc), lambda bi, i: (0, 0))
    xs = pl.BlockSpec((1, tm, D_MODEL), lambda bi, i: (bi, i, 0))
    if hy.ndim == 5:
        hy_spec = pl.BlockSpec((1, hy.shape[1], tm // FFT_R, hy.shape[3], c), lambda bi, i: (bi, 0, i, 0, 0))
    else:
        hy_spec = t512(0)
    return pl.pallas_call(
        _mix_kernel,
        grid=(b, l // tm),
        in_specs=[xs, t512(0), t512(0), t512(P_Z // c), t512(0), hy_spec,
                  pl.BlockSpec((1, 1, D_MODEL), lambda bi, i: (bi, 0, 0)),
                  const(1, c), const(1, c), const(1, c), const(D_MIX, D_MODEL)],
        out_specs=xs,
        out_shape=jax.ShapeDtypeStruct(x.shape, F32),
        compiler_params=_cp("parallel", "parallel"),
        name="mix_out",
    )(x, yf, yb, proj, at, hy, g1, gs, ga, gh, w)


def _ffn_kernel(x_ref, sh_ref, sc_ref, g2_ref, ng_ref, wg_ref, wu_ref, wd_ref, fg_ref, o_ref, h_ref, acc_ref,
                *, final):
    j = pl.program_id(2)

    @pl.when(j == 0)
    def _():
        y = _rms(x_ref[0], ng_ref[...])
        h_ref[...] = (y * (1.0 + sc_ref[0]) + sh_ref[0]).astype(BF16)
        acc_ref[...] = jnp.zeros_like(acc_ref)

    h = h_ref[...]
    act = (_silu(_dot(h, wg_ref[...])) * _dot(h, wu_ref[...])).astype(BF16)
    acc_ref[...] += _dot(act, wd_ref[...])

    @pl.when(j == pl.num_programs(2) - 1)
    def _():
        y = x_ref[0] + g2_ref[0] * acc_ref[...]
        if final:
            y = _rms(y, fg_ref[...])
        o_ref[0] = y


def _ffn(x, sh, sc, g2, ng, w_gu, w_down, fg, final):
    b, l, _ = x.shape
    tm = min(l, 1024)
    tf = 256
    nf = D_FF // tf
    xs = pl.BlockSpec((1, tm, D_MODEL), lambda bi, i, j: (bi, i, 0))
    mod = pl.BlockSpec((1, 1, D_MODEL), lambda bi, i, j: (bi, 0, 0))
    row = pl.BlockSpec((1, D_MODEL), lambda bi, i, j: (0, 0))
    return pl.pallas_call(
        functools.partial(_ffn_kernel, final=final),
        grid=(b, l // tm, nf),
        in_specs=[xs, mod, mod, mod, row,
                  pl.BlockSpec((D_MODEL, tf), lambda bi, i, j: (0, j)),
                  pl.BlockSpec((D_MODEL, tf), lambda bi, i, j: (0, nf + j)),
                  pl.BlockSpec((tf, D_MODEL), lambda bi, i, j: (j, 0)),
                  row],
        out_specs=xs,
        out_shape=jax.ShapeDtypeStruct(x.shape, F32),
        scratch_shapes=[pltpu.VMEM((tm, D_MODEL), BF16), pltpu.VMEM((tm, D_MODEL), F32)],
        compiler_params=_cp("parallel", "parallel", "arbitrary"),
        name="ffn",
    )(x, sh, sc, g2, ng, w_gu, w_gu, w_down, fg)


def _rope_tables(seq):
    t = np.arange(seq)
    inv = ROPE_THETA ** (-np.arange(0, ROPE_AXIS_DIM, 2, dtype=np.float64) / ROPE_AXIS_DIM)
    ang_r = (t // GRID_W)[:, None] * inv
    ang_c = (t % GRID_W)[:, None] * inv
    cos = np.concatenate([np.cos(ang_r), np.cos(ang_r), np.cos(ang_c), np.cos(ang_c)], axis=1)
    sin = np.concatenate([-np.sin(ang_r), np.sin(ang_r), -np.sin(ang_c), np.sin(ang_c)], axis=1)
    return (jnp.asarray(np.tile(cos, (1, 2)), F32), jnp.asarray(np.tile(sin, (1, 2)), F32))


def _hy_feats(seq):
    t = np.linspace(0.0, 1.0, seq)[:, None]
    w = 2.0 * math.pi * np.arange(seq)[:, None] / seq
    f = np.linspace(1e-4, HY_BANDS - 1, HY_BANDS)
    feats = np.concatenate([t, np.cos(f * w), -np.sin(f * w)], axis=1)
    feats_t = np.pad(feats, ((0, 0), (0, LANE - HY_EMB))).T
    return jnp.asarray(feats_t, F32), jnp.asarray(t, F32)


def _hy_abs_deltas():
    lo = math.log(HY_TARGET) / HY_SLOW_DECAY_PCT
    hi = math.log(HY_TARGET) / HY_FAST_DECAY_PCT
    return jnp.asarray(np.abs(np.linspace(lo, hi, HY_WIDTH))[None], F32)


def _fft_tables():
    r_ = FFT_R
    n = r_ * r_
    n2 = np.arange(r_)[:, None, None]
    k1 = np.arange(FFT_K1)[None, :, None]
    n1 = np.arange(r_ // 2)[None, None, :]
    live = (k1 <= r_ // 2).astype(np.float64)
    th = 2.0 * math.pi * ((k1 * (r_ * n1 + n2)) % n) / n
    fwd = np.concatenate([np.cos(th) * live, -np.sin(th) * live], axis=1)
    wgt = np.where((k1 == 0) | (k1 == r_ // 2), 1.0, 2.0)
    inv = np.transpose(fwd * np.concatenate([wgt, wgt], axis=1), (0, 2, 1)) / n
    kk = np.arange(r_)
    ph = 2.0 * math.pi * ((kk[:, None] * kk[None, :]) % r_) / r_
    fr, fi = np.cos(ph), -np.sin(ph)
    fs = np.block([[fr, -fi], [fi, fr]])
    fc = np.block([[fr, fi], [-fi, fr]])
    return tuple(_hi_lo_rows(a) for a in (fwd, inv, fs, fc))


def _hi_lo_rows(t):
    t = jnp.asarray(t, F32)
    hi = t.astype(BF16)
    lo = (t - hi.astype(F32)).astype(BF16)
    return jnp.concatenate([hi, lo], axis=-2)


def _dense_dft_tables(seq):
    n = 2 * seq
    kk = np.arange(n)
    ph = 2.0 * math.pi * ((kk[:, None] * kk[None, :]) % n) / n
    ff = np.concatenate([np.cos(ph), -np.sin(ph)], axis=0)
    fi = np.concatenate([np.cos(ph[:seq]), -np.sin(ph[:seq])], axis=1) / n
    return jnp.asarray(ff, F32), jnp.asarray(fi, F32)


def _head_expand():
    e = np.zeros((LANE, 2 * SSD_INNER), np.float32)
    for h in range(2 * SSD_HEADS):
        e[h, h * SSD_HEAD_DIM:(h + 1) * SSD_HEAD_DIM] = 1.0
    return jnp.asarray(e, BF16)


def _group_mean():
    g = np.kron(np.eye(LANE // ATTN_HEAD_DIM), np.ones((ATTN_HEAD_DIM, ATTN_HEAD_DIM))) / ATTN_HEAD_DIM
    return jnp.asarray(g, F32)


def _relayout_w_in(w):
    cols = [w[:, OFF_HY:OFF_HY + 3 * HY_WIDTH], w[:, OFF_Q:OFF_Q + ATTN_INNER],
            w[:, OFF_XB:OFF_XB + SSD_XB], w[:, OFF_C:OFF_C + SSD_GN], w[:, OFF_Z:OFF_Z + SSD_INNER],
            w[:, OFF_K:OFF_K + ATTN_KV_INNER], w[:, OFF_V:OFF_V + ATTN_KV_INNER],
            w[:, OFF_DT:OFF_DT + 2 * SSD_HEADS]]
    wr = jnp.concatenate(cols, axis=1)
    return jnp.pad(wr, ((0, 0), (0, NP - wr.shape[1]))).astype(BF16)


def _pad_row(v):
    v = v.reshape(1, -1)
    return jnp.pad(v, ((0, 0), (0, LANE - v.shape[1])))


def _layer_params(l, raw, tables):
    p = {k: v[l] for k, v in raw.items()}
    p["w_in_r"] = _relayout_w_in(p["w_in"])
    p["dt_bias_row"] = _pad_row(p["ssd_dt_bias"])
    p["a_log_row"] = _pad_row(p["ssd_a_log"])
    p["d_row"] = jnp.repeat(p["ssd_d"], SSD_HEAD_DIM)[None]
    p["head_expand"] = tables["head_expand"]
    p["gq"] = jnp.tile(p["q_norm_g"], LANE // ATTN_HEAD_DIM)[None]
    p["gk"] = jnp.tile(p["k_norm_g"], LANE // ATTN_HEAD_DIM)[None]
    p["hy_w1p"] = jnp.pad(p["hy_w1"], ((0, LANE - HY_EMB), (0, 0)))
    p["w_out_b"] = p["w_out"].astype(BF16)
    p["w_gu_b"] = p["w_gu"].astype(BF16)
    p["w_down_b"] = p["w_down"].astype(BF16)
    return p


def _mixers(proj, dt_raw, p, tables, is_ctx, ssd_init):
    yf, yb, sf, sb = _ssd_scan(proj, dt_raw, ssd_init[0], ssd_init[1], p)
    rope = tables["rope_ctx"] if is_ctx else tables["rope"]
    q, k, v = _attn_prep(proj, rope[0], rope[1], p["gq"], p["gk"], tables["group_mean"])
    return (yf, yb, sf, sb), (q, k, v)


def _hyena(proj, p, tables, is_ctx):
    seq = proj.shape[1]
    bias = p["hy_bias"][None]
    vx, x0 = _hy_conv(proj, p["hy_conv_w"], p["hy_conv_b"], not is_ctx)
    if is_ctx:
        f = _hy_taps(*tables["feats_ctx"], p, tables["abs_deltas"], seq, False)
        taps = jnp.concatenate([f[0], jnp.zeros((1, HY_WIDTH), F32), jnp.flip(f[1, 1:], axis=0)], axis=0)
        return _hy_ctx(vx, x0, taps, tables["dft_ctx"][0], tables["dft_ctx"][1], bias)
    b, ng, n1, r, c = vx.shape
    t_fwd, t_inv, fs, fc = tables["fft"]
    hi = lambda t: t[..., :t.shape[-2] // 2, :]
    f = _hy_taps(*tables["feats"], p, tables["abs_deltas"], seq, True)
    kre, kim = _fft2_filter(*_fft1(f.reshape(2, ng, n1 * r, c), t_fwd, True), fs)
    vx = vx.reshape(b, ng, n1 * r, c)
    are, aim = _fft1(vx, hi(t_fwd), False)
    zre, zim = _fft2(are, aim, kre, kim, hi(fs), hi(fc))
    y = _ifft1(zre, zim, hi(t_inv), vx, x0.reshape(b, ng, n1 * r, c), bias)
    return y.reshape(b, ng, n1, r, c)


def _tail(x, proj, ssd, at, hy, mod, p, final_g, final):
    x = _mix_out(x, ssd[0], ssd[1], proj, at, hy, mod[2], p["ssd_norm_g"][None], p["attn_norm_g"][None],
                 p["hy_norm_g"][None], p["w_out_b"])
    return _ffn(x, mod[3], mod[4], mod[5], p["norm2_g"][None], p["w_gu_b"], p["w_down_b"], final_g[None], final)


def _layer(x, xc, mod_rows, p, tables, final_g, last):
    b = x.shape[0]
    mod_x = [mod_rows[:b, i * D_MODEL:(i + 1) * D_MODEL][:, None, :] for i in range(N_MOD)]
    mod_c = [jnp.broadcast_to(mod_rows[b:b + 1, i * D_MODEL:(i + 1) * D_MODEL][:, None, :], (b, 1, D_MODEL))
             for i in range(N_MOD)]
    g1 = p["norm1_g"][None]
    proj, dt_raw = _proj_in(x, mod_x[0], mod_x[1], g1, p["w_in_r"])
    projc, dt_raw_c = _proj_in(xc, mod_c[0], mod_c[1], g1, p["w_in_r"])
    zeros = jnp.zeros((b, SSD_STATE, SSD_INNER), F32)
    ssd_c, qkv_c = _mixers(projc, dt_raw_c, p, tables, True, (zeros, zeros))
    ssd_x, qkv_x = _mixers(proj, dt_raw, p, tables, False, (ssd_c[2], ssd_c[3]))
    k_all = jnp.concatenate([qkv_c[1], qkv_x[1]], axis=2)
    v_all = jnp.concatenate([qkv_c[2], qkv_x[2]], axis=3)
    at = _flash(qkv_x[0], k_all, v_all)
    hy = _hyena(proj, p, tables, False)
    x = _tail(x, proj, ssd_x, at, hy, mod_x, p, final_g, last)
    if last:
        return x, None
    at_c = _flash(qkv_c[0], qkv_c[1], qkv_c[2])
    hy_c = _hyena(projc, p, tables, True)
    xc = _tail(xc, projc, ssd_c, at_c, hy_c, mod_c, p, final_g, False)
    return x, xc


def kernel(x, c, ctx, c_ctx, w_mod, b_mod, norm1_g, w_in, ssd_conv_w, ssd_conv_b, ssd_a_log, ssd_dt_bias, ssd_d, ssd_norm_g, q_norm_g, k_norm_g, attn_norm_g, hy_conv_w, hy_conv_b, hy_w1, hy_b1, hy_freq, hy_w2, hy_b2, hy_w3, hy_bias, hy_norm_g, w_out, norm2_g, w_gu, w_down, final_g):
    b, seq, _ = x.shape
    ctx_len = ctx.shape[1]
    depth = w_mod.shape[0]
    assert 2 * seq == FFT_R * FFT_R and b + 1 <= SUBLANE
    raw = dict(w_in=w_in, ssd_conv_w=ssd_conv_w, ssd_conv_b=ssd_conv_b, ssd_a_log=ssd_a_log,
               ssd_dt_bias=ssd_dt_bias, ssd_d=ssd_d, ssd_norm_g=ssd_norm_g, q_norm_g=q_norm_g,
               k_norm_g=k_norm_g, attn_norm_g=attn_norm_g, hy_conv_w=hy_conv_w, hy_conv_b=hy_conv_b,
               hy_w1=hy_w1, hy_b1=hy_b1, hy_freq=hy_freq, hy_w2=hy_w2, hy_b2=hy_b2, hy_w3=hy_w3,
               hy_bias=hy_bias, hy_norm_g=hy_norm_g, w_out=w_out, norm2_g=norm2_g, w_gu=w_gu, w_down=w_down,
               norm1_g=norm1_g)
    ones = jnp.ones((ctx_len, LANE), F32)
    tables = dict(rope=_rope_tables(seq), rope_ctx=(ones, jnp.zeros_like(ones)),
                  feats=_hy_feats(seq), feats_ctx=_hy_feats(ctx_len), abs_deltas=_hy_abs_deltas(),
                  fft=_fft_tables(), dft_ctx=_dense_dft_tables(ctx_len),
                  head_expand=_head_expand(), group_mean=_group_mean())
    c_rows = jnp.concatenate([c, c_ctx[None], jnp.zeros((SUBLANE - b - 1, D_MODEL), F32)], axis=0)
    xc = ctx
    for l in range(depth):
        p = _layer_params(l, raw, tables)
        mod_rows = _mod_call(c_rows, w_mod[l], b_mod[l][None])
        x, xc = _layer(x, xc, mod_rows, p, tables, final_g, l == depth - 1)
    return x
```

```python
import functools
import math

import numpy as np
import jax
import jax.numpy as jnp
from jax import lax
from jax.experimental import pallas as pl
from jax.experimental.pallas import tpu as pltpu

F32 = jnp.float32
BF16 = jnp.bfloat16
HI = lax.Precision.HIGHEST

D_MODEL = 1024
GRID_W = 64
EPS = 1e-6
SSD_HEADS = 8
SSD_HEAD_DIM = 64
SSD_INNER = 512
SSD_STATE = 128
SSD_CHUNK = 128
SSD_GN = 256
SSD_XB = SSD_INNER + SSD_GN
ATTN_HEADS = 8
ATTN_KV_HEADS = 2
ATTN_HEAD_DIM = 64
ATTN_INNER = 512
ATTN_KV_INNER = 128
ATTN_SCALE = ATTN_HEAD_DIM ** -0.5
LOG2E = math.log2(math.e)
ATTN_V_ROWS = 80
ROPE_THETA = 10000.0
ROPE_AXIS_DIM = ATTN_HEAD_DIM // 2
HY_WIDTH = 512
HY_BANDS = 16
HY_EMB = 1 + 2 * HY_BANDS
HY_HIDDEN = 64
HY_FAST_DECAY_PCT = 0.3
HY_SLOW_DECAY_PCT = 1.5
HY_TARGET = 1e-2
D_MIX = 1536
D_FF = 2816
N_MOD = 6

OFF_K = 0
OFF_V = OFF_K + ATTN_KV_INNER
OFF_XB = OFF_V + ATTN_KV_INNER
OFF_DT = OFF_XB + SSD_XB
OFF_C = OFF_DT + 2 * SSD_HEADS
OFF_Q = OFF_C + SSD_GN
OFF_Z = OFF_Q + ATTN_INNER
OFF_HY = OFF_Z + SSD_INNER

P_HY = 0
P_Q = 1536
P_X = 2048
P_B = 2560
P_C = 2816
P_Z = 3072
P_K = 3584
P_V = 3712
P_DT = 3840
NP = 4096

LANE = 128
SUBLANE = 8
HALO = 16
VMEM_LIMIT = 48 * 1024 * 1024

FFT_R = 128
FFT_G = 8
FFT_K1 = 72


def _cp(*sem):
    return pltpu.CompilerParams(dimension_semantics=sem, vmem_limit_bytes=VMEM_LIMIT)


def _silu(x):
    return x * (1.0 / (1.0 + jnp.exp(-x)))


def _softplus(x):
    return jnp.maximum(x, 0.0) + jnp.log(1.0 + jnp.exp(-jnp.abs(x)))


def _rms(x, g):
    return x * lax.rsqrt(jnp.mean(x * x, axis=-1, keepdims=True) + EPS) * g


def _dot(a, b):
    return jnp.dot(a, b, preferred_element_type=F32)


def _dot_hi(a, b):
    return jnp.dot(a, b, precision=HI, preferred_element_type=F32)


def _split_bf16(x):
    hi = x.astype(BF16)
    return hi, (x - hi.astype(F32)).astype(BF16)


def _dot3_table(t, x):
    t_hi, t_lo = _split_bf16(t)
    x_hi, x_lo = _split_bf16(x)
    return _dot(t_hi, x_hi) + _dot(t_lo, x_hi) + _dot(t_hi, x_lo)


def _dot1_table(t, x):
    return _dot(t.astype(BF16), x.astype(BF16))


def _mod_kernel(c_ref, w_ref, b_ref, o_ref):
    o_ref[...] = _dot_hi(_silu(c_ref[...]), w_ref[...]) + b_ref[...]


def _mod_call(c_rows, w, b):
    n = w.shape[1]
    tn = 1024
    return pl.pallas_call(
        _mod_kernel,
        grid=(n // tn,),
        in_specs=[pl.BlockSpec((SUBLANE, D_MODEL), lambda j: (0, 0)),
                  pl.BlockSpec((D_MODEL, tn), lambda j: (0, j)),
                  pl.BlockSpec((1, tn), lambda j: (0, j))],
        out_specs=pl.BlockSpec((SUBLANE, tn), lambda j: (0, j)),
        out_shape=jax.ShapeDtypeStruct((SUBLANE, n), F32),
        compiler_params=_cp("parallel"),
        name="adaln_mod",
    )(c_rows, w, b)


def _proj_kernel(x_ref, sh_ref, sc_ref, g_ref, w_ref, o_ref, dt_ref, h_ref, *, dt_tile, dt_off):
    j = pl.program_id(2)

    @pl.when(j == 0)
    def _():
        y = _rms(x_ref[0], g_ref[...])
        h_ref[...] = (y * (1.0 + sc_ref[0]) + sh_ref[0]).astype(BF16)

    r = _dot(h_ref[...], w_ref[...])
    o_ref[0] = r.astype(BF16)

    @pl.when(j == dt_tile)
    def _():
        dt_ref[0] = r[:, dt_off:dt_off + LANE]


def _proj_in(x, sh, sc, g, w):
    b, l, _ = x.shape
    tm = min(l, 1024)
    tn = 1024
    return pl.pallas_call(
        functools.partial(_proj_kernel, dt_tile=P_DT // tn, dt_off=P_DT % tn),
        grid=(b, l // tm, NP // tn),
        in_specs=[pl.BlockSpec((1, tm, D_MODEL), lambda bi, i, j: (bi, i, 0)),
                  pl.BlockSpec((1, 1, D_MODEL), lambda bi, i, j: (bi, 0, 0)),
                  pl.BlockSpec((1, 1, D_MODEL), lambda bi, i, j: (bi, 0, 0)),
                  pl.BlockSpec((1, D_MODEL), lambda bi, i, j: (0, 0)),
                  pl.BlockSpec((D_MODEL, tn), lambda bi, i, j: (0, j))],
        out_specs=[pl.BlockSpec((1, tm, tn), lambda bi, i, j: (bi, i, j)),
                   pl.BlockSpec((1, tm, LANE), lambda bi, i, j: (bi, i, 0))],
        out_shape=[jax.ShapeDtypeStruct((b, l, NP), BF16), jax.ShapeDtypeStruct((b, l, LANE), F32)],
        scratch_shapes=[pltpu.VMEM((tm, D_MODEL), BF16)],
        compiler_params=_cp("parallel", "parallel", "arbitrary"),
        name="proj_in",
    )(x, sh, sc, g, w)


def _dwconv3(u, prev_row, next_row, w, b):
    tm = u.shape[0]
    ri = lax.broadcasted_iota(jnp.int32, u.shape, 0)
    um = jnp.where(ri == 0, prev_row, pltpu.roll(u, 1, 0))
    up = jnp.where(ri == tm - 1, next_row, pltpu.roll(u, tm - 1, 0))
    return um * w[0:1] + u * w[1:2] + up * w[2:3] + b


def _conv_group(refs, i, n_i):
    u_ref, p_ref, n_ref, w_ref, b_ref = refs
    prev_row = jnp.where(i > 0, p_ref[0].astype(F32)[HALO - 1:HALO, :], 0.0)
    next_row = jnp.where(i < n_i - 1, n_ref[0].astype(F32)[0:1, :], 0.0)
    return _dwconv3(u_ref[0].astype(F32), prev_row, next_row, w_ref[...], b_ref[...])


def _hy_conv_kernel(*refs, group_major):
    i, n_i = pl.program_id(1), pl.num_programs(1)
    v = _conv_group(refs[0:5], i, n_i)
    x1 = _conv_group(refs[5:10], i, n_i)
    x0 = _conv_group(refs[10:15], i, n_i)
    vx_ref, x0_ref = refs[15], refs[16]
    vx = v * x1
    vx_ref[0] = _to_group_major(vx) if group_major else vx
    x0_ref[0] = _to_group_major(x0) if group_major else x0


def _to_group_major(u):
    rows, c = u.shape
    return jnp.swapaxes(u.reshape(rows // FFT_R, FFT_R // FFT_G, FFT_G, c), 0, 1)


def _from_group_major(u):
    g, n1, r, c = u.shape
    return jnp.swapaxes(u, 0, 1).reshape(n1 * g * r, c)


def _conv_specs(tm, tc, l, col_block, w_block):
    nrb = l // HALO
    per = tm // HALO
    return [
        pl.BlockSpec((1, tm, tc), lambda bi, i: (bi, i, col_block)),
        pl.BlockSpec((1, HALO, tc), lambda bi, i: (bi, jnp.maximum(i * per - 1, 0), col_block)),
        pl.BlockSpec((1, HALO, tc), lambda bi, i: (bi, jnp.minimum((i + 1) * per, nrb - 1), col_block)),
        pl.BlockSpec((SUBLANE, tc), lambda bi, i: (0, w_block)),
        pl.BlockSpec((1, tc), lambda bi, i: (0, w_block)),
    ]


def _pad_taps(w):
    return jnp.pad(w, ((0, SUBLANE - w.shape[0]), (0, 0)))


def _hy_conv(proj, conv_w, conv_b, group_major):
    b, l, _ = proj.shape
    tm = min(l, 1024)
    tc = HY_WIDTH
    wp = _pad_taps(conv_w)
    bp = conv_b[None]
    specs, args = [], []
    for grp in range(3):
        specs += _conv_specs(tm, tc, l, P_HY // tc + grp, grp)
        args += [proj, proj, proj, wp, bp]
    if group_major:
        ng = FFT_R // FFT_G
        out_spec = pl.BlockSpec((1, ng, tm // FFT_R, FFT_G, tc), lambda bi, i: (bi, 0, i, 0, 0))
        shp = jax.ShapeDtypeStruct((b, ng, l // FFT_R, FFT_G, tc), F32)
    else:
        out_spec = pl.BlockSpec((1, tm, tc), lambda bi, i: (bi, i, 0))
        shp = jax.ShapeDtypeStruct((b, l, tc), F32)
    return pl.pallas_call(
        functools.partial(_hy_conv_kernel, group_major=group_major),
        grid=(b, l // tm),
        in_specs=specs,
        out_specs=[out_spec, out_spec],
        out_shape=[shp, shp],
        compiler_params=_cp("parallel", "parallel"),
        name="hy_conv",
    )(*args)


def _split_pieces(x, n):
    pieces, r = [], x
    for k in range(n):
        pc = r.astype(BF16)
        pieces.append(pc)
        if k + 1 < n:
            r = r - pc.astype(F32)
    return pieces


def _select_left(sel, x, n):
    w = x.shape[1]
    r = _dot(sel, jnp.concatenate(_split_pieces(x, n), axis=1))
    return sum(r[:, k * w:(k + 1) * w] for k in range(n))


def _select_right(x, sel, n):
    m = x.shape[0]
    r = _dot(jnp.concatenate(_split_pieces(x, n), axis=0), sel)
    return sum(r[k * m:(k + 1) * m] for k in range(n))


def _ssd_chunk(xbc, dt_raw, a, bias, d_row, e_d, st, lane0, fwd):
    q = SSD_CHUNK
    xs = xbc[:, :SSD_INNER]
    bm = xbc[:, SSD_INNER:SSD_XB]
    cm = xbc[:, SSD_XB:]
    ri = lax.broadcasted_iota(jnp.int32, (q, q), 0)
    ci = lax.broadcasted_iota(jnp.int32, (q, q), 1)
    dt = _softplus(dt_raw + bias)
    adt = dt * a
    cs = _select_left((ci <= ri).astype(BF16), adt, 3)
    tot = cs[q - 1:q, :]
    if fwd:
        key = cs
        w_c = dt * jnp.exp(tot - key)
        e_c = jnp.exp(key)
        mask = ci <= ri
    else:
        key = cs - adt
        w_c = dt * jnp.exp(key)
        e_c = jnp.exp(tot - key)
        mask = ci >= ri
    key_t = key.T
    dt_t = dt.T
    dec_c = jnp.broadcast_to(jnp.exp(tot), (2 * SUBLANE, LANE))
    spread = _select_right(jnp.concatenate([w_c, e_c, dec_c], axis=0), e_d, 2)
    w_e, e_off, decay = spread[:q], spread[q:2 * q], spread[2 * q:2 * q + 1]
    x_b = xs.astype(BF16)
    x_w = (xs * w_e).astype(BF16)
    st_b = st.astype(BF16)
    y_parts, st_parts, off_parts = [], [], []
    hg = SSD_HEADS // 2
    for g in range(2):
        bg = bm[:, g * SSD_STATE:(g + 1) * SSD_STATE]
        cg = cm[:, g * SSD_STATE:(g + 1) * SSD_STATE].astype(BF16)
        bg_t = bg.T.astype(BF16)
        gmat = _dot(cg, bg_t)
        gs = slice(g * hg * SSD_HEAD_DIM, (g + 1) * hg * SSD_HEAD_DIM)
        off_parts.append(_dot(cg, st_b[:, gs]))
        st_parts.append(_dot(bg_t, x_w[:, gs]))
        for hh in range(hg):
            h = g * hg + hh
            col = key[:, lane0 + h:lane0 + h + 1]
            row = key_t[lane0 + h:lane0 + h + 1, :]
            diff = (col - row) if fwd else (row - col)
            lm = jnp.exp(jnp.where(mask, diff, -1e30)) * dt_t[lane0 + h:lane0 + h + 1, :]
            s = (gmat * lm).astype(BF16)
            y_parts.append(_dot(s, x_b[:, h * SSD_HEAD_DIM:(h + 1) * SSD_HEAD_DIM]))
    y = jnp.concatenate(y_parts, axis=1) + jnp.concatenate(off_parts, axis=1) * e_off
    if fwd:
        y = y + d_row * xs
    st_new = st * decay + jnp.concatenate(st_parts, axis=1)
    return y, st_new


def _ssd_kernel(xf_ref, xfp_ref, xfn_ref, xb_ref, xbp_ref, xbn_ref, dtf_ref, dtb_ref, cw_ref, cb_ref,
                bias_ref, alog_ref, d_ref, e_ref, sf0_ref, sb0_ref, yf_ref, yb_ref, sf_ref, sb_ref, stf, stb):
    c = pl.program_id(1)
    nc = pl.num_programs(1)

    @pl.when(c == 0)
    def _():
        stf[...] = sf0_ref[0]
        stb[...] = sb0_ref[0]

    def conv_silu(u_ref, p_ref, n_ref, chunk):
        prev_row = jnp.where(chunk > 0, p_ref[0].astype(F32)[HALO - 1:HALO, :], 0.0)
        next_row = jnp.where(chunk < nc - 1, n_ref[0].astype(F32)[0:1, :], 0.0)
        return _silu(_dwconv3(u_ref[0].astype(F32), prev_row, next_row, cw_ref[...], cb_ref[...]))

    a = -jnp.exp(alog_ref[...])
    bias = bias_ref[...]
    xf = conv_silu(xf_ref, xfp_ref, xfn_ref, c)
    xb = conv_silu(xb_ref, xbp_ref, xbn_ref, nc - 1 - c)
    yf, sf = _ssd_chunk(xf, dtf_ref[0], a, bias, d_ref[...], e_ref[:, :SSD_INNER], stf[...], 0, True)
    yb, sb = _ssd_chunk(xb, dtb_ref[0], a, bias, d_ref[...], e_ref[:, SSD_INNER:], stb[...], SSD_HEADS, False)
    yf_ref[0] = yf.astype(yf_ref.dtype)
    yb_ref[0] = yb.astype(yb_ref.dtype)
    stf[...] = sf
    stb[...] = sb
    sf_ref[0] = sf
    sb_ref[0] = sb


def _ssd_scan(proj, dt_raw, sf0, sb0, p):
    b, l, _ = proj.shape
    nc = l // SSD_CHUNK
    q = SSD_CHUNK
    w = 2 * SSD_INNER
    xcol = P_X // w
    per = q // HALO
    nrb = l // HALO
    st_spec = pl.BlockSpec((1, SSD_STATE, SSD_INNER), lambda bi, c: (bi, 0, 0))
    y_shape = jax.ShapeDtypeStruct((b, l, SSD_INNER), BF16)
    st_shape = jax.ShapeDtypeStruct((b, SSD_STATE, SSD_INNER), F32)
    row = lambda n: pl.BlockSpec((1, n), lambda bi, c: (0, 0))

    def chunk_specs(chunk_of):
        return [pl.BlockSpec((1, q, w), lambda bi, c: (bi, chunk_of(c), xcol)),
                pl.BlockSpec((1, HALO, w), lambda bi, c: (bi, jnp.maximum(chunk_of(c) * per - 1, 0), xcol)),
                pl.BlockSpec((1, HALO, w), lambda bi, c: (bi, jnp.minimum((chunk_of(c) + 1) * per, nrb - 1), xcol))]

    fwd_of = lambda c: c
    bwd_of = lambda c: nc - 1 - c
    return pl.pallas_call(
        _ssd_kernel,
        grid=(b, nc),
        in_specs=chunk_specs(fwd_of) + chunk_specs(bwd_of) + [
            pl.BlockSpec((1, q, LANE), lambda bi, c: (bi, c, 0)),
            pl.BlockSpec((1, q, LANE), lambda bi, c: (bi, nc - 1 - c, 0)),
            pl.BlockSpec((SUBLANE, w), lambda bi, c: (0, 0)), row(w),
            row(LANE), row(LANE), row(SSD_INNER),
            pl.BlockSpec((LANE, 2 * SSD_INNER), lambda bi, c: (0, 0)),
            st_spec, st_spec],
        out_specs=[pl.BlockSpec((1, q, SSD_INNER), lambda bi, c: (bi, c, 0)),
                   pl.BlockSpec((1, q, SSD_INNER), lambda bi, c: (bi, nc - 1 - c, 0)),
                   st_spec, st_spec],
        out_shape=[y_shape, y_shape, st_shape, st_shape],
        scratch_shapes=[pltpu.VMEM((SSD_STATE, SSD_INNER), F32), pltpu.VMEM((SSD_STATE, SSD_INNER), F32)],
        compiler_params=_cp("parallel", "arbitrary"),
        name="ssd_scan",
    )(proj, proj, proj, proj, proj, proj, dt_raw, dt_raw, _pad_taps(p["ssd_conv_w"]), p["ssd_conv_b"][None],
      p["dt_bias_row"], p["a_log_row"], p["d_row"], p["head_expand"], sf0, sb0)


def _attn_prep_kernel(q_ref, k_ref, v_ref, cos_ref, sin_ref, gq_ref, gk_ref, gm_ref, qo_ref, ko_ref, vo_ref):
    cos = cos_ref[...]
    sin = sin_ref[...]
    gm = gm_ref[...]
    lane = lax.broadcasted_iota(jnp.int32, cos.shape, 1)
    first = jnp.bitwise_and(lane, 31) < 16
    hd = ATTN_HEAD_DIM

    def norm_rope(t, g):
        ms = _dot_hi(t * t, gm)
        y = t * lax.rsqrt(ms + EPS) * g
        partner = jnp.where(first, pltpu.roll(y, LANE - 16, 1), pltpu.roll(y, 16, 1))
        return y * cos + partner * sin

    for s in range(ATTN_INNER // LANE):
        qs = norm_rope(q_ref[0, :, s * LANE:(s + 1) * LANE].astype(F32), gq_ref[...]) * (ATTN_SCALE * LOG2E)
        qt = qs.T.astype(BF16)
        qo_ref[0, 2 * s] = qt[:hd]
        qo_ref[0, 2 * s + 1] = qt[hd:]
    ks = norm_rope(k_ref[0].astype(F32), gk_ref[...])
    ko_ref[0, 0] = ks[:, :hd].astype(BF16)
    ko_ref[0, 1] = ks[:, hd:].astype(BF16)
    vt = v_ref[0].astype(F32).T.astype(BF16)
    tm = vt.shape[1]
    pad_rows = lax.broadcasted_iota(jnp.int32, (ATTN_V_ROWS - hd, tm), 0)
    tail = jnp.where(pad_rows == 0, 1.0, 0.0).astype(BF16)
    for g in range(ATTN_KV_HEADS):
        vo_ref[0, g, 0:hd, :] = vt[g * hd:(g + 1) * hd]
        vo_ref[0, g, hd:ATTN_V_ROWS, :] = tail


def _attn_prep(proj, cos, sin, gq, gk, gm):
    b, l, _ = proj.shape
    tm = min(l, 1024)
    hd = ATTN_HEAD_DIM
    const = lambda r, c: pl.BlockSpec((r, c), lambda bi, i: (0, 0))
    return pl.pallas_call(
        _attn_prep_kernel,
        grid=(b, l // tm),
        in_specs=[pl.BlockSpec((1, tm, ATTN_INNER), lambda bi, i: (bi, i, P_Q // ATTN_INNER)),
                  pl.BlockSpec((1, tm, LANE), lambda bi, i: (bi, i, P_K // LANE)),
                  pl.BlockSpec((1, tm, LANE), lambda bi, i: (bi, i, P_V // LANE)),
                  pl.BlockSpec((tm, LANE), lambda bi, i: (i, 0)),
                  pl.BlockSpec((tm, LANE), lambda bi, i: (i, 0)),
                  const(1, LANE), const(1, LANE), const(LANE, LANE)],
        out_specs=[pl.BlockSpec((1, ATTN_HEADS, hd, tm), lambda bi, i: (bi, 0, 0, i)),
                   pl.BlockSpec((1, ATTN_KV_HEADS, tm, hd), lambda bi, i: (bi, 0, i, 0)),
                   pl.BlockSpec((1, ATTN_KV_HEADS, ATTN_V_ROWS, tm), lambda bi, i: (bi, 0, 0, i))],
        out_shape=[jax.ShapeDtypeStruct((b, ATTN_HEADS, hd, l), BF16),
                   jax.ShapeDtypeStruct((b, ATTN_KV_HEADS, l, hd), BF16),
                   jax.ShapeDtypeStruct((b, ATTN_KV_HEADS, ATTN_V_ROWS, l), BF16)],
        compiler_params=_cp("parallel", "parallel"),
        name="attn_prep",
    )(proj, proj, proj, cos, sin, gq, gk, gm)


def _flash_kernel(q_ref, k_ref, v_ref, o_ref, s_ref, m_ref, acc_ref, *, tk, nk):
    r = ATTN_HEADS // ATTN_KV_HEADS
    hd = ATTN_HEAD_DIM

    def scores(j, h, slot):
        start = pl.multiple_of(j * tk, tk)
        s_ref[slot] = _dot(k_ref[0, 0, pl.ds(start, tk), :], q_ref[0, h])

    m_ref[...] = jnp.full(m_ref.shape, -1e30, F32)
    acc_ref[...] = jnp.zeros(acc_ref.shape, F32)
    scores(0, 0, 0)

    def body(j, _):
        start = pl.multiple_of(j * tk, tk)
        vs = v_ref[0, 0, :, pl.ds(start, tk)]
        j_next = jnp.minimum(j + 1, nk - 1)
        for h in range(r):
            if h < r - 1:
                scores(j, h + 1, (h + 1) % 2)
            else:
                scores(j_next, 0, 0)
            s = s_ref[h % 2]
            m = m_ref[h]
            mn = jnp.maximum(m, jnp.max(s, axis=0, keepdims=True))
            alpha = jnp.exp2(m - mn)
            p = jnp.exp2(s - mn).astype(BF16)
            m_ref[h] = mn
            acc_ref[h] = alpha * acc_ref[h] + _dot(vs, p)
        return 0

    lax.fori_loop(0, nk, body, 0)
    outs = []
    for h in range(r):
        acc = acc_ref[h]
        outs.append((acc[:hd] * (1.0 / acc[hd:hd + 1])).T)
    o_ref[0] = jnp.concatenate(outs, axis=1).astype(o_ref.dtype)


def _flash(q, k, v):
    b, _, hd, l = q.shape
    lk = k.shape[2]
    r = ATTN_HEADS // ATTN_KV_HEADS
    assert r % 2 == 0
    tq = min(l, 1024)
    tk = 384 if lk % 384 == 0 else 256
    return pl.pallas_call(
        functools.partial(_flash_kernel, tk=tk, nk=lk // tk),
        grid=(b, ATTN_KV_HEADS, l // tq),
        in_specs=[pl.BlockSpec((1, r, hd, tq), lambda bi, g, i: (bi, g, 0, i)),
                  pl.BlockSpec((1, 1, lk, hd), lambda bi, g, i: (bi, g, 0, 0)),
                  pl.BlockSpec((1, 1, ATTN_V_ROWS, lk), lambda bi, g, i: (bi, g, 0, 0))],
        out_specs=pl.BlockSpec((1, tq, r * hd), lambda bi, g, i: (bi, i, g)),
        out_shape=jax.ShapeDtypeStruct((b, l, ATTN_INNER), BF16),
        scratch_shapes=[pltpu.VMEM((2, tk, tq), F32), pltpu.VMEM((r, 1, tq), F32),
                        pltpu.VMEM((r, ATTN_V_ROWS, tq), F32)],
        compiler_params=_cp("parallel", "parallel", "parallel"),
        name="flash_gqa",
    )(q, k, v)


def _taps_kernel(ft_ref, t_ref, w1_ref, b1_ref, fr_ref, w2_ref, b2_ref, w3_ref, adel_ref, o_ref, *, group_major):
    fr = fr_ref[...]
    h = jnp.sin(fr * (_dot_hi(w1_ref[...], ft_ref[...]) + b1_ref[...]))
    h = jnp.sin(fr * (_dot_hi(w2_ref[...], h) + b2_ref[...]))
    y = _dot_hi(h.T, w3_ref[...])
    decay = jnp.exp(-t_ref[...] * adel_ref[...])
    f0 = y[:, :HY_WIDTH] * decay
    f1 = y[:, HY_WIDTH:] * decay
    lag = pl.program_id(0) * f1.shape[0] + lax.broadcasted_iota(jnp.int32, f1.shape, 0)
    f1 = jnp.where(lag == 0, 0.0, f1)
    o_ref[0] = _to_group_major(f0) if group_major else f0
    o_ref[1] = _to_group_major(f1) if group_major else f1


def _hy_taps(feats_t, t_col, p, adel, seq, group_major):
    tm = min(seq, 1024)
    const = lambda r, c: pl.BlockSpec((r, c), lambda i: (0, 0))
    if group_major:
        ng = FFT_R // FFT_G
        out = pl.BlockSpec((2, ng, tm // FFT_R, FFT_G, HY_WIDTH), lambda i: (0, 0, i, 0, 0))
        shp = jax.ShapeDtypeStruct((2, ng, seq // FFT_R, FFT_G, HY_WIDTH), F32)
    else:
        out = pl.BlockSpec((2, tm, HY_WIDTH), lambda i: (0, i, 0))
        shp = jax.ShapeDtypeStruct((2, seq, HY_WIDTH), F32)
    return pl.pallas_call(
        functools.partial(_taps_kernel, group_major=group_major),
        grid=(seq // tm,),
        in_specs=[pl.BlockSpec((LANE, tm), lambda i: (0, i)),
                  pl.BlockSpec((tm, 1), lambda i: (i, 0)),
                  const(HY_HIDDEN, LANE), const(HY_HIDDEN, 1), const(HY_HIDDEN, 1),
                  const(HY_HIDDEN, HY_HIDDEN), const(HY_HIDDEN, 1),
                  const(HY_HIDDEN, 2 * HY_WIDTH), const(1, HY_WIDTH)],
        out_specs=out,
        out_shape=shp,
        compiler_params=_cp("parallel"),
        name="hy_taps",
    )(feats_t, t_col, p["hy_w1p"].T, p["hy_b1"][:, None], p["hy_freq"][:, None], p["hy_w2"].T,
      p["hy_b2"][:, None], p["hy_w3"], adel)


def _fft1_kernel(x_ref, t_ref, are_ref, aim_ref, *, precise):
    x = x_ref[0]
    a = _dot3_table(t_ref[0], x) if precise else _dot1_table(t_ref[0], x)
    rows = FFT_K1 * FFT_G
    are_ref[0] = a[:rows]
    aim_ref[0] = a[rows:]


def _fft1(x, table, precise):
    nb, ng, rows, c = x.shape
    spec_o = pl.BlockSpec((1, None, FFT_K1 * FFT_G, c), lambda g, bi: (bi, g, 0, 0))
    shp = jax.ShapeDtypeStruct((nb, ng, FFT_K1 * FFT_G, c), F32)
    return pl.pallas_call(
        functools.partial(_fft1_kernel, precise=precise),
        grid=(ng, nb),
        in_specs=[pl.BlockSpec((1, None, rows, c), lambda g, bi: (bi, g, 0, 0)),
                  pl.BlockSpec((1,) + table.shape[1:], lambda g, bi: (g, 0, 0))],
        out_specs=[spec_o, spec_o],
        out_shape=[shp, shp],
        compiler_params=_cp("parallel", "parallel"),
        name="hy_fft1",
    )(x, table)


def _k1_rows(ref, b, kk):
    blk = ref[b, :, kk]
    return blk.reshape(blk.shape[0] * blk.shape[1], blk.shape[2])


def _fft2_filter_kernel(are_ref, aim_ref, fs_ref, kre_ref, kim_ref):
    r_ = FFT_R
    for kk in range(SUBLANE):
        xc = _dot3_table(fs_ref[...], jnp.concatenate([_k1_rows(are_ref, 0, kk), _k1_rows(aim_ref, 0, kk)], axis=0))
        xa = _dot3_table(fs_ref[...], jnp.concatenate([_k1_rows(are_ref, 1, kk), _k1_rows(aim_ref, 1, kk)], axis=0))
        kre_ref[kk] = xc[:r_] + xa[:r_]
        kim_ref[kk] = xc[r_:] - xa[r_:]


def _fft2_filter(are, aim, fs):
    r_ = FFT_R
    nb, ng, _, c = are.shape
    view = lambda a: a.reshape(nb, ng, FFT_K1, FFT_G, c)
    spec_i = pl.BlockSpec((nb, ng, SUBLANE, FFT_G, c), lambda kg: (0, 0, kg, 0, 0))
    spec_o = pl.BlockSpec((SUBLANE, r_, c), lambda kg: (kg, 0, 0))
    shp = jax.ShapeDtypeStruct((FFT_K1, r_, c), F32)
    return pl.pallas_call(
        _fft2_filter_kernel,
        grid=(FFT_K1 // SUBLANE,),
        in_specs=[spec_i, spec_i, pl.BlockSpec((2 * r_, 2 * r_), lambda kg: (0, 0))],
        out_specs=[spec_o, spec_o],
        out_shape=[shp, shp],
        compiler_params=_cp("parallel"),
        name="hy_fft2_filter",
    )(view(are), view(aim), fs)


def _fft2_kernel(are_ref, aim_ref, kre_ref, kim_ref, fs_ref, fc_ref, zre_ref, zim_ref):
    r_ = FFT_R
    ng = are_ref.shape[1]
    for kk in range(SUBLANE):
        a = jnp.concatenate([_k1_rows(are_ref, 0, kk), _k1_rows(aim_ref, 0, kk)], axis=0)
        b = _dot1_table(fs_ref[...], a)
        br, bi = b[:r_], b[r_:]
        kr, ki = kre_ref[kk], kim_ref[kk]
        y = jnp.concatenate([br * kr - bi * ki, br * ki + bi * kr], axis=0)
        z = _dot1_table(fc_ref[...], y)
        zre_ref[0, :, kk] = z[:r_].reshape(ng, FFT_G, z.shape[1])
        zim_ref[0, :, kk] = z[r_:].reshape(ng, FFT_G, z.shape[1])


def _fft2(are, aim, kre, kim, fs, fc):
    nb, ng, rows, c = are.shape
    r_ = FFT_R
    view = lambda a: a.reshape(nb, ng, FFT_K1, FFT_G, c)
    spec_a = pl.BlockSpec((1, ng, SUBLANE, FFT_G, c), lambda bi, kg: (bi, 0, kg, 0, 0))
    spec_k = pl.BlockSpec((SUBLANE, r_, c), lambda bi, kg: (kg, 0, 0))
    spec_f = pl.BlockSpec((2 * r_, 2 * r_), lambda bi, kg: (0, 0))
    shp = jax.ShapeDtypeStruct((nb, ng, FFT_K1, FFT_G, c), F32)
    zre, zim = pl.pallas_call(
        _fft2_kernel,
        grid=(nb, FFT_K1 // SUBLANE),
        in_specs=[spec_a, spec_a, spec_k, spec_k, spec_f, spec_f],
        out_specs=[spec_a, spec_a],
        out_shape=[shp, shp],
        compiler_params=_cp("parallel", "parallel"),
        name="hy_fft2",
    )(view(are), view(aim), kre, kim, fs, fc)
    return zre.reshape(are.shape), zim.reshape(are.shape)


def _ifft1_kernel(zre_ref, zim_ref, t_ref, vx_ref, x0_ref, bias_ref, o_ref):
    z = jnp.concatenate([zre_ref[0], zim_ref[0]], axis=0)
    conv = _dot1_table(t_ref[0], z)
    o_ref[0] = x0_ref[0] * (conv + bias_ref[...] * vx_ref[0])


def _ifft1(zre, zim, table, vx, x0, bias):
    nb, ng, rows, c = vx.shape
    spec_z = pl.BlockSpec((1, None, FFT_K1 * FFT_G, c), lambda g, bi: (bi, g, 0, 0))
    spec_x = pl.BlockSpec((1, None, rows, c), lambda g, bi: (bi, g, 0, 0))
    return pl.pallas_call(
        _ifft1_kernel,
        grid=(ng, nb),
        in_specs=[spec_z, spec_z,
                  pl.BlockSpec((1,) + table.shape[1:], lambda g, bi: (g, 0, 0)),
                  spec_x, spec_x,
                  pl.BlockSpec((1, c), lambda g, bi: (0, 0))],
        out_specs=spec_x,
        out_shape=jax.ShapeDtypeStruct(vx.shape, F32),
        compiler_params=_cp("parallel", "parallel"),
        name="hy_ifft1",
    )(zre, zim, table, vx, x0, bias)


def _hy_ctx_kernel(vx_ref, x0_ref, taps_ref, ff_ref, fi_ref, bias_ref, o_ref):
    vx = vx_ref[0]
    seq = vx.shape[0]
    n = 2 * seq
    u = _dot_hi(ff_ref[:, :seq], vx)
    k = _dot_hi(ff_ref[...], taps_ref[...])
    ur, ui = u[:n], u[n:]
    kr, ki = k[:n], k[n:]
    y = jnp.concatenate([ur * kr - ui * ki, ur * ki + ui * kr], axis=0)
    conv = _dot_hi(fi_ref[...], y)
    o_ref[0] = x0_ref[0] * (conv + bias_ref[...] * vx)


def _hy_ctx(vx, x0, taps, ff, fi, bias):
    b, seq, c = vx.shape
    n = 2 * seq
    spec_x = pl.BlockSpec((1, seq, c), lambda bi: (bi, 0, 0))
    const = lambda r, cc: pl.BlockSpec((r, cc), lambda bi: (0, 0))
    return pl.pallas_call(
        _hy_ctx_kernel,
        grid=(b,),
        in_specs=[spec_x, spec_x, const(n, c), const(2 * n, n), const(seq, 2 * n), const(1, c)],
        out_specs=spec_x,
        out_shape=jax.ShapeDtypeStruct((b, seq, c), F32),
        compiler_params=_cp("parallel"),
        name="hy_ctx",
    )(vx, x0, taps, ff, fi, bias)


def _mix_kernel(x_ref, yf_ref, yb_ref, z_ref, at_ref, hy_ref, g1_ref, gs_ref, ga_ref, gh_ref, w_ref, o_ref):
    ys = _rms((yf_ref[0].astype(F32) + yb_ref[0].astype(F32)) * _silu(z_ref[0].astype(F32)), gs_ref[...]).astype(BF16)
    ya = _rms(at_ref[0].astype(F32), ga_ref[...]).astype(BF16)
    hy = hy_ref[0]
    if hy.ndim == 4:
        hy = _from_group_major(hy)
    yh = _rms(hy, gh_ref[...]).astype(BF16)
    r = (_dot(ys, w_ref[0:SSD_INNER, :])
         + _dot(ya, w_ref[SSD_INNER:SSD_INNER + ATTN_INNER, :])
         + _dot(yh, w_ref[SSD_INNER + ATTN_INNER:, :]))
    o_ref[0] = x_ref[0] + g1_ref[0] * r


def _mix_out(x, yf, yb, proj, at, hy, g1, gs, ga, gh, w):
    b, l, _ = x.shape
    tm = min(l, 512)
    c = SSD_INNER
    t512 = lambda col: pl.BlockSpec((1, tm, c), lambda bi, i: (bi, i, col))
    const = lambda r, cc: pl.BlockSpec((r, cc), lambda bi, i: (0, 0))
    xs = pl.BlockSpec((1, tm, D_MODEL), lambda bi, i: (bi, i, 0))
    if hy.ndim == 5:
        hy_spec = pl.BlockSpec((1, hy.shape[1], tm // FFT_R, hy.shape[3], c), lambda bi, i: (bi, 0, i, 0, 0))
    else:
        hy_spec = t512(0)
    return pl.pallas_call(
        _mix_kernel,
        grid=(b, l // tm),
        in_specs=[xs, t512(0), t512(0), t512(P_Z // c), t512(0), hy_spec,
                  pl.BlockSpec((1, 1, D_MODEL), lambda bi, i: (bi, 0, 0)),
                  const(1, c), const(1, c), const(1, c), const(D_MIX, D_MODEL)],
        out_specs=xs,
        out_shape=jax.ShapeDtypeStruct(x.shape, F32),
        compiler_params=_cp("parallel", "parallel"),
        name="mix_out",
    )(x, yf, yb, proj, at, hy, g1, gs, ga, gh, w)


def _ffn_kernel(x_ref, sh_ref, sc_ref, g2_ref, ng_ref, wg_ref, wu_ref, wd_ref, fg_ref, o_ref, h_ref, acc_ref,
                *, final):
    j = pl.program_id(2)

    @pl.when(j == 0)
    def _():
        y = _rms(x_ref[0], ng_ref[...])
        h_ref[...] = (y * (1.0 + sc_ref[0]) + sh_ref[0]).astype(BF16)
        acc_ref[...] = jnp.zeros_like(acc_ref)

    h = h_ref[...]
    act = (_silu(_dot(h, wg_ref[...])) * _dot(h, wu_ref[...])).astype(BF16)
    acc_ref[...] += _dot(act, wd_ref[...])

    @pl.when(j == pl.num_programs(2) - 1)
    def _():
        y = x_ref[0] + g2_ref[0] * acc_ref[...]
        if final:
            y = _rms(y, fg_ref[...])
        o_ref[0] = y


def _ffn(x, sh, sc, g2, ng, w_gu, w_down, fg, final):
    b, l, _ = x.shape
    tm = min(l, 1024)
    tf = 256
    nf = D_FF // tf
    xs = pl.BlockSpec((1, tm, D_MODEL), lambda bi, i, j: (bi, i, 0))
    mod = pl.BlockSpec((1, 1, D_MODEL), lambda bi, i, j: (bi, 0, 0))
    row = pl.BlockSpec((1, D_MODEL), lambda bi, i, j: (0, 0))
    return pl.pallas_call(
        functools.partial(_ffn_kernel, final=final),
        grid=(b, l // tm, nf),
        in_specs=[xs, mod, mod, mod, row,
                  pl.BlockSpec((D_MODEL, tf), lambda bi, i, j: (0, j)),
                  pl.BlockSpec((D_MODEL, tf), lambda bi, i, j: (0, nf + j)),
                  pl.BlockSpec((tf, D_MODEL), lambda bi, i, j: (j, 0)),
                  row],
        out_specs=xs,
        out_shape=jax.ShapeDtypeStruct(x.shape, F32),
        scratch_shapes=[pltpu.VMEM((tm, D_MODEL), BF16), pltpu.VMEM((tm, D_MODEL), F32)],
        compiler_params=_cp("parallel", "parallel", "arbitrary"),
        name="ffn",
    )(x, sh, sc, g2, ng, w_gu, w_gu, w_down, fg)


def _rope_tables(seq):
    t = np.arange(seq)
    inv = ROPE_THETA ** (-np.arange(0, ROPE_AXIS_DIM, 2, dtype=np.float64) / ROPE_AXIS_DIM)
    ang_r = (t // GRID_W)[:, None] * inv
    ang_c = (t % GRID_W)[:, None] * inv
    cos = np.concatenate([np.cos(ang_r), np.cos(ang_r), np.cos(ang_c), np.cos(ang_c)], axis=1)
    sin = np.concatenate([-np.sin(ang_r), np.sin(ang_r), -np.sin(ang_c), np.sin(ang_c)], axis=1)
    return (jnp.asarray(np.tile(cos, (1, 2)), F32), jnp.asarray(np.tile(sin, (1, 2)), F32))


def _hy_feats(seq):
    t = np.linspace(0.0, 1.0, seq)[:, None]
    w = 2.0 * math.pi * np.arange(seq)[:, None] / seq
    f = np.linspace(1e-4, HY_BANDS - 1, HY_BANDS)
    feats = np.concatenate([t, np.cos(f * w), -np.sin(f * w)], axis=1)
    feats_t = np.pad(feats, ((0, 0), (0, LANE - HY_EMB))).T
    return jnp.asarray(feats_t, F32), jnp.asarray(t, F32)


def _hy_abs_deltas():
    lo = math.log(HY_TARGET) / HY_SLOW_DECAY_PCT
    hi = math.log(HY_TARGET) / HY_FAST_DECAY_PCT
    return jnp.asarray(np.abs(np.linspace(lo, hi, HY_WIDTH))[None], F32)


def _fft_tables():
    r_ = FFT_R
    n = r_ * r_
    n2 = np.arange(r_)[:, None, None]
    k1 = np.arange(FFT_K1)[None, :, None]
    n1 = np.arange(r_ // 2)[None, None, :]
    live = (k1 <= r_ // 2).astype(np.float64)
    th = 2.0 * math.pi * ((k1 * (r_ * n1 + n2)) % n) / n
    cos, msin = np.cos(th) * live, -np.sin(th) * live
    wgt = np.where((k1 == 0) | (k1 == r_ // 2), 1.0, 2.0) / n
    fwd = np.concatenate([_group_blocks(cos), _group_blocks(msin)], axis=1)
    inv = np.transpose(np.concatenate([_group_blocks(cos * wgt), _group_blocks(msin * wgt)], axis=1), (0, 2, 1))
    kk = np.arange(r_)
    ph = 2.0 * math.pi * ((kk[:, None] * kk[None, :]) % r_) / r_
    fr, fi = np.cos(ph), -np.sin(ph)
    fs = np.block([[fr, -fi], [fi, fr]])
    fc = np.block([[fr, fi], [-fi, fr]])
    return tuple(jnp.asarray(a, F32) for a in (fwd, inv, fs, fc))


def _group_blocks(t):
    n2, k1, n1 = t.shape
    ng = n2 // FFT_G
    out = np.zeros((ng, k1, FFT_G, n1, FFT_G))
    for r in range(FFT_G):
        out[:, :, r, :, r] = t.reshape(ng, FFT_G, k1, n1)[:, r]
    return out.reshape(ng, k1 * FFT_G, n1 * FFT_G)


def _dense_dft_tables(seq):
    n = 2 * seq
    kk = np.arange(n)
    ph = 2.0 * math.pi * ((kk[:, None] * kk[None, :]) % n) / n
    ff = np.concatenate([np.cos(ph), -np.sin(ph)], axis=0)
    fi = np.concatenate([np.cos(ph[:seq]), -np.sin(ph[:seq])], axis=1) / n
    return jnp.asarray(ff, F32), jnp.asarray(fi, F32)


def _head_expand():
    e = np.zeros((LANE, 2 * SSD_INNER), np.float32)
    for h in range(2 * SSD_HEADS):
        e[h, h * SSD_HEAD_DIM:(h + 1) * SSD_HEAD_DIM] = 1.0
    return jnp.asarray(e, BF16)


def _group_mean():
    g = np.kron(np.eye(LANE // ATTN_HEAD_DIM), np.ones((ATTN_HEAD_DIM, ATTN_HEAD_DIM))) / ATTN_HEAD_DIM
    return jnp.asarray(g, F32)


def _relayout_w_in(w):
    cols = [w[:, OFF_HY:OFF_HY + 3 * HY_WIDTH], w[:, OFF_Q:OFF_Q + ATTN_INNER],
            w[:, OFF_XB:OFF_XB + SSD_XB], w[:, OFF_C:OFF_C + SSD_GN], w[:, OFF_Z:OFF_Z + SSD_INNER],
            w[:, OFF_K:OFF_K + ATTN_KV_INNER], w[:, OFF_V:OFF_V + ATTN_KV_INNER],
            w[:, OFF_DT:OFF_DT + 2 * SSD_HEADS]]
    wr = jnp.concatenate(cols, axis=1)
    return jnp.pad(wr, ((0, 0), (0, NP - wr.shape[1]))).astype(BF16)


def _pad_row(v):
    v = v.reshape(1, -1)
    return jnp.pad(v, ((0, 0), (0, LANE - v.shape[1])))


def _layer_params(l, raw, tables):
    p = {k: v[l] for k, v in raw.items()}
    p["w_in_r"] = _relayout_w_in(p["w_in"])
    p["dt_bias_row"] = _pad_row(p["ssd_dt_bias"])
    p["a_log_row"] = _pad_row(p["ssd_a_log"])
    p["d_row"] = jnp.repeat(p["ssd_d"], SSD_HEAD_DIM)[None]
    p["head_expand"] = tables["head_expand"]
    p["gq"] = jnp.tile(p["q_norm_g"], LANE // ATTN_HEAD_DIM)[None]
    p["gk"] = jnp.tile(p["k_norm_g"], LANE // ATTN_HEAD_DIM)[None]
    p["hy_w1p"] = jnp.pad(p["hy_w1"], ((0, LANE - HY_EMB), (0, 0)))
    p["w_out_b"] = p["w_out"].astype(BF16)
    p["w_gu_b"] = p["w_gu"].astype(BF16)
    p["w_down_b"] = p["w_down"].astype(BF16)
    return p


def _mixers(proj, dt_raw, p, tables, is_ctx, ssd_init):
    yf, yb, sf, sb = _ssd_scan(proj, dt_raw, ssd_init[0], ssd_init[1], p)
    rope = tables["rope_ctx"] if is_ctx else tables["rope"]
    q, k, v = _attn_prep(proj, rope[0], rope[1], p["gq"], p["gk"], tables["group_mean"])
    return (yf, yb, sf, sb), (q, k, v)


def _hyena(proj, p, tables, is_ctx):
    seq = proj.shape[1]
    bias = p["hy_bias"][None]
    vx, x0 = _hy_conv(proj, p["hy_conv_w"], p["hy_conv_b"], not is_ctx)
    if is_ctx:
        f = _hy_taps(*tables["feats_ctx"], p, tables["abs_deltas"], seq, False)
        taps = jnp.concatenate([f[0], jnp.zeros((1, HY_WIDTH), F32), jnp.flip(f[1, 1:], axis=0)], axis=0)
        return _hy_ctx(vx, x0, taps, tables["dft_ctx"][0], tables["dft_ctx"][1], bias)
    b, ng, n1, r, c = vx.shape
    t_fwd, t_inv, fs, fc = tables["fft"]
    f = _hy_taps(*tables["feats"], p, tables["abs_deltas"], seq, True)
    kre, kim = _fft2_filter(*_fft1(f.reshape(2, ng, n1 * r, c), t_fwd, True), fs)
    vx = vx.reshape(b, ng, n1 * r, c)
    are, aim = _fft1(vx, t_fwd, False)
    zre, zim = _fft2(are, aim, kre, kim, fs, fc)
    y = _ifft1(zre, zim, t_inv, vx, x0.reshape(b, ng, n1 * r, c), bias)
    return y.reshape(b, ng, n1, r, c)


def _tail(x, proj, ssd, at, hy, mod, p, final_g, final):
    x = _mix_out(x, ssd[0], ssd[1], proj, at, hy, mod[2], p["ssd_norm_g"][None], p["attn_norm_g"][None],
                 p["hy_norm_g"][None], p["w_out_b"])
    return _ffn(x, mod[3], mod[4], mod[5], p["norm2_g"][None], p["w_gu_b"], p["w_down_b"], final_g[None], final)


def _layer(x, xc, mod_rows, p, tables, final_g, last):
    b = x.shape[0]
    mod_x = [mod_rows[:b, i * D_MODEL:(i + 1) * D_MODEL][:, None, :] for i in range(N_MOD)]
    mod_c = [jnp.broadcast_to(mod_rows[b:b + 1, i * D_MODEL:(i + 1) * D_MODEL][:, None, :], (b, 1, D_MODEL))
             for i in range(N_MOD)]
    g1 = p["norm1_g"][None]
    proj, dt_raw = _proj_in(x, mod_x[0], mod_x[1], g1, p["w_in_r"])
    projc, dt_raw_c = _proj_in(xc, mod_c[0], mod_c[1], g1, p["w_in_r"])
    zeros = jnp.zeros((b, SSD_STATE, SSD_INNER), F32)
    ssd_c, qkv_c = _mixers(projc, dt_raw_c, p, tables, True, (zeros, zeros))
    ssd_x, qkv_x = _mixers(proj, dt_raw, p, tables, False, (ssd_c[2], ssd_c[3]))
    k_all = jnp.concatenate([qkv_c[1], qkv_x[1]], axis=2)
    v_all = jnp.concatenate([qkv_c[2], qkv_x[2]], axis=3)
    at = _flash(qkv_x[0], k_all, v_all)
    hy = _hyena(proj, p, tables, False)
    x = _tail(x, proj, ssd_x, at, hy, mod_x, p, final_g, last)
    if last:
        return x, None
    at_c = _flash(qkv_c[0], qkv_c[1], qkv_c[2])
    hy_c = _hyena(projc, p, tables, True)
    xc = _tail(xc, projc, ssd_c, at_c, hy_c, mod_c, p, final_g, False)
    return x, xc


def kernel(x, c, ctx, c_ctx, w_mod, b_mod, norm1_g, w_in, ssd_conv_w, ssd_conv_b, ssd_a_log, ssd_dt_bias, ssd_d, ssd_norm_g, q_norm_g, k_norm_g, attn_norm_g, hy_conv_w, hy_conv_b, hy_w1, hy_b1, hy_freq, hy_w2, hy_b2, hy_w3, hy_bias, hy_norm_g, w_out, norm2_g, w_gu, w_down, final_g):
    b, seq, _ = x.shape
    ctx_len = ctx.shape[1]
    depth = w_mod.shape[0]
    assert 2 * seq == FFT_R * FFT_R and b + 1 <= SUBLANE
    raw = dict(w_in=w_in, ssd_conv_w=ssd_conv_w, ssd_conv_b=ssd_conv_b, ssd_a_log=ssd_a_log,
               ssd_dt_bias=ssd_dt_bias, ssd_d=ssd_d, ssd_norm_g=ssd_norm_g, q_norm_g=q_norm_g,
               k_norm_g=k_norm_g, attn_norm_g=attn_norm_g, hy_conv_w=hy_conv_w, hy_conv_b=hy_conv_b,
               hy_w1=hy_w1, hy_b1=hy_b1, hy_freq=hy_freq, hy_w2=hy_w2, hy_b2=hy_b2, hy_w3=hy_w3,
               hy_bias=hy_bias, hy_norm_g=hy_norm_g, w_out=w_out, norm2_g=norm2_g, w_gu=w_gu, w_down=w_down,
               norm1_g=norm1_g)
    ones = jnp.ones((ctx_len, LANE), F32)
    tables = dict(rope=_rope_tables(seq), rope_ctx=(ones, jnp.zeros_like(ones)),
                  feats=_hy_feats(seq), feats_ctx=_hy_feats(ctx_len), abs_deltas=_hy_abs_deltas(),
                  fft=_fft_tables(), dft_ctx=_dense_dft_tables(ctx_len),
                  head_expand=_head_expand(), group_mean=_group_mean())
    c_rows = jnp.concatenate([c, c_ctx[None], jnp.zeros((SUBLANE - b - 1, D_MODEL), F32)], axis=0)
    xc = ctx
    for l in range(depth):
        p = _layer_params(l, raw, tables)
        mod_rows = _mod_call(c_rows, w_mod[l], b_mod[l][None])
        x, xc = _layer(x, xc, mod_rows, p, tables, final_g, l == depth - 1)
    return x
```

```python
import functools
import math

import numpy as np
import jax
import jax.numpy as jnp
from jax import lax
from jax.experimental import pallas as pl
from jax.experimental.pallas import tpu as pltpu

F32 = jnp.float32
BF16 = jnp.bfloat16
HI = lax.Precision.HIGHEST

D_MODEL = 1024
GRID_W = 64
EPS = 1e-6
SSD_HEADS = 8
SSD_HEAD_DIM = 64
SSD_INNER = 512
SSD_STATE = 128
SSD_CHUNK = 128
SSD_GN = 256
SSD_XB = SSD_INNER + SSD_GN
ATTN_HEADS = 8
ATTN_KV_HEADS = 2
ATTN_HEAD_DIM = 64
ATTN_INNER = 512
ATTN_KV_INNER = 128
ATTN_SCALE = ATTN_HEAD_DIM ** -0.5
LOG2E = math.log2(math.e)
ATTN_V_ROWS = 80
ROPE_THETA = 10000.0
ROPE_AXIS_DIM = ATTN_HEAD_DIM // 2
HY_WIDTH = 512
HY_BANDS = 16
HY_EMB = 1 + 2 * HY_BANDS
HY_HIDDEN = 64
HY_FAST_DECAY_PCT = 0.3
HY_SLOW_DECAY_PCT = 1.5
HY_TARGET = 1e-2
D_MIX = 1536
D_FF = 2816
N_MOD = 6

OFF_K = 0
OFF_V = OFF_K + ATTN_KV_INNER
OFF_XB = OFF_V + ATTN_KV_INNER
OFF_DT = OFF_XB + SSD_XB
OFF_C = OFF_DT + 2 * SSD_HEADS
OFF_Q = OFF_C + SSD_GN
OFF_Z = OFF_Q + ATTN_INNER
OFF_HY = OFF_Z + SSD_INNER

P_HY = 0
P_Q = 1536
P_X = 2048
P_B = 2560
P_C = 2816
P_Z = 3072
P_K = 3584
P_V = 3712
P_DT = 3840
NP = 4096

LANE = 128
SUBLANE = 8
HALO = 16
VMEM_LIMIT = 48 * 1024 * 1024

FFT_R = 128
FFT_G = 8
FFT_K1 = 72


def _cp(*sem):
    return pltpu.CompilerParams(dimension_semantics=sem, vmem_limit_bytes=VMEM_LIMIT)


def _silu(x):
    return x * (1.0 / (1.0 + jnp.exp(-x)))


def _softplus(x):
    return jnp.maximum(x, 0.0) + jnp.log(1.0 + jnp.exp(-jnp.abs(x)))


def _rms(x, g):
    return x * lax.rsqrt(jnp.mean(x * x, axis=-1, keepdims=True) + EPS) * g


def _dot(a, b):
    return jnp.dot(a, b, preferred_element_type=F32)


def _dot_hi(a, b):
    return jnp.dot(a, b, precision=HI, preferred_element_type=F32)


def _split_bf16(x):
    hi = x.astype(BF16)
    return hi, (x - hi.astype(F32)).astype(BF16)


def _dot3_table(t, x):
    t_hi, t_lo = _split_bf16(t)
    x_hi, x_lo = _split_bf16(x)
    return _dot(t_hi, x_hi) + _dot(t_lo, x_hi) + _dot(t_hi, x_lo)


def _dot1_table(t, x):
    return _dot(t.astype(BF16), x.astype(BF16))


def _mod_kernel(c_ref, w_ref, b_ref, o_ref):
    o_ref[...] = _dot_hi(_silu(c_ref[...]), w_ref[...]) + b_ref[...]


def _mod_call(c_rows, w, b):
    n = w.shape[1]
    tn = 1024
    return pl.pallas_call(
        _mod_kernel,
        grid=(n // tn,),
        in_specs=[pl.BlockSpec((SUBLANE, D_MODEL), lambda j: (0, 0)),
                  pl.BlockSpec((D_MODEL, tn), lambda j: (0, j)),
                  pl.BlockSpec((1, tn), lambda j: (0, j))],
        out_specs=pl.BlockSpec((SUBLANE, tn), lambda j: (0, j)),
        out_shape=jax.ShapeDtypeStruct((SUBLANE, n), F32),
        compiler_params=_cp("parallel"),
        name="adaln_mod",
    )(c_rows, w, b)


def _proj_kernel(x_ref, sh_ref, sc_ref, g_ref, w_ref, o_ref, dt_ref, h_ref, *, dt_tile, dt_off):
    j = pl.program_id(2)

    @pl.when(j == 0)
    def _():
        y = _rms(x_ref[0], g_ref[...])
        h_ref[...] = (y * (1.0 + sc_ref[0]) + sh_ref[0]).astype(BF16)

    r = _dot(h_ref[...], w_ref[...])
    o_ref[0] = r.astype(BF16)

    @pl.when(j == dt_tile)
    def _():
        dt_ref[0] = r[:, dt_off:dt_off + LANE]


def _proj_in(x, sh, sc, g, w):
    b, l, _ = x.shape
    tm = min(l, 1024)
    tn = 1024
    return pl.pallas_call(
        functools.partial(_proj_kernel, dt_tile=P_DT // tn, dt_off=P_DT % tn),
        grid=(b, l // tm, NP // tn),
        in_specs=[pl.BlockSpec((1, tm, D_MODEL), lambda bi, i, j: (bi, i, 0)),
                  pl.BlockSpec((1, 1, D_MODEL), lambda bi, i, j: (bi, 0, 0)),
                  pl.BlockSpec((1, 1, D_MODEL), lambda bi, i, j: (bi, 0, 0)),
                  pl.BlockSpec((1, D_MODEL), lambda bi, i, j: (0, 0)),
                  pl.BlockSpec((D_MODEL, tn), lambda bi, i, j: (0, j))],
        out_specs=[pl.BlockSpec((1, tm, tn), lambda bi, i, j: (bi, i, j)),
                   pl.BlockSpec((1, tm, LANE), lambda bi, i, j: (bi, i, 0))],
        out_shape=[jax.ShapeDtypeStruct((b, l, NP), BF16), jax.ShapeDtypeStruct((b, l, LANE), F32)],
        scratch_shapes=[pltpu.VMEM((tm, D_MODEL), BF16)],
        compiler_params=_cp("parallel", "parallel", "arbitrary"),
        name="proj_in",
    )(x, sh, sc, g, w)


def _dwconv3(u, prev_row, next_row, w, b):
    tm = u.shape[0]
    ri = lax.broadcasted_iota(jnp.int32, u.shape, 0)
    um = jnp.where(ri == 0, prev_row, pltpu.roll(u, 1, 0))
    up = jnp.where(ri == tm - 1, next_row, pltpu.roll(u, tm - 1, 0))
    return um * w[0:1] + u * w[1:2] + up * w[2:3] + b


def _conv_group(refs, i, n_i):
    u_ref, p_ref, n_ref, w_ref, b_ref = refs
    prev_row = jnp.where(i > 0, p_ref[0].astype(F32)[HALO - 1:HALO, :], 0.0)
    next_row = jnp.where(i < n_i - 1, n_ref[0].astype(F32)[0:1, :], 0.0)
    return _dwconv3(u_ref[0].astype(F32), prev_row, next_row, w_ref[...], b_ref[...])


def _hy_conv_kernel(*refs, group_major):
    i, n_i = pl.program_id(1), pl.num_programs(1)
    v = _conv_group(refs[0:5], i, n_i)
    x1 = _conv_group(refs[5:10], i, n_i)
    x0 = _conv_group(refs[10:15], i, n_i)
    vx_ref, x0_ref = refs[15], refs[16]
    vx = v * x1
    vx_ref[0] = _to_group_major(vx) if group_major else vx
    x0_ref[0] = _to_group_major(x0) if group_major else x0


def _to_group_major(u):
    rows, c = u.shape
    return jnp.swapaxes(u.reshape(rows // FFT_R, FFT_R // FFT_G, FFT_G, c), 0, 1)


def _from_group_major(u):
    g, n1, r, c = u.shape
    return jnp.swapaxes(u, 0, 1).reshape(n1 * g * r, c)


def _conv_specs(tm, tc, l, col_block, w_block):
    nrb = l // HALO
    per = tm // HALO
    return [
        pl.BlockSpec((1, tm, tc), lambda bi, i: (bi, i, col_block)),
        pl.BlockSpec((1, HALO, tc), lambda bi, i: (bi, jnp.maximum(i * per - 1, 0), col_block)),
        pl.BlockSpec((1, HALO, tc), lambda bi, i: (bi, jnp.minimum((i + 1) * per, nrb - 1), col_block)),
        pl.BlockSpec((SUBLANE, tc), lambda bi, i: (0, w_block)),
        pl.BlockSpec((1, tc), lambda bi, i: (0, w_block)),
    ]


def _pad_taps(w):
    return jnp.pad(w, ((0, SUBLANE - w.shape[0]), (0, 0)))


def _hy_conv(proj, conv_w, conv_b, group_major):
    b, l, _ = proj.shape
    tm = min(l, 1024)
    tc = HY_WIDTH
    wp = _pad_taps(conv_w)
    bp = conv_b[None]
    specs, args = [], []
    for grp in range(3):
        specs += _conv_specs(tm, tc, l, P_HY // tc + grp, grp)
        args += [proj, proj, proj, wp, bp]
    if group_major:
        ng = FFT_R // FFT_G
        out_spec = pl.BlockSpec((1, ng, tm // FFT_R, FFT_G, tc), lambda bi, i: (bi, 0, i, 0, 0))
        shp = jax.ShapeDtypeStruct((b, ng, l // FFT_R, FFT_G, tc), F32)
    else:
        out_spec = pl.BlockSpec((1, tm, tc), lambda bi, i: (bi, i, 0))
        shp = jax.ShapeDtypeStruct((b, l, tc), F32)
    return pl.pallas_call(
        functools.partial(_hy_conv_kernel, group_major=group_major),
        grid=(b, l // tm),
        in_specs=specs,
        out_specs=[out_spec, out_spec],
        out_shape=[shp, shp],
        compiler_params=_cp("parallel", "parallel"),
        name="hy_conv",
    )(*args)


def _split_pieces(x, n):
    pieces, r = [], x
    for k in range(n):
        pc = r.astype(BF16)
        pieces.append(pc)
        if k + 1 < n:
            r = r - pc.astype(F32)
    return pieces


def _select_left(sel, x, n):
    w = x.shape[1]
    r = _dot(sel, jnp.concatenate(_split_pieces(x, n), axis=1))
    return sum(r[:, k * w:(k + 1) * w] for k in range(n))


def _select_right(x, sel, n):
    m = x.shape[0]
    r = _dot(jnp.concatenate(_split_pieces(x, n), axis=0), sel)
    return sum(r[k * m:(k + 1) * m] for k in range(n))


def _ssd_chunk(xbc, dt_raw, a, bias, d_row, e_d, st, lane0, fwd):
    q = SSD_CHUNK
    xs = xbc[:, :SSD_INNER]
    bm = xbc[:, SSD_INNER:SSD_XB]
    cm = xbc[:, SSD_XB:]
    ri = lax.broadcasted_iota(jnp.int32, (q, q), 0)
    ci = lax.broadcasted_iota(jnp.int32, (q, q), 1)
    dt = _softplus(dt_raw + bias)
    adt = dt * a
    cs = _select_left((ci <= ri).astype(BF16), adt, 3)
    tot = cs[q - 1:q, :]
    if fwd:
        key = cs
        w_c = dt * jnp.exp(tot - key)
        e_c = jnp.exp(key)
        mask = ci <= ri
    else:
        key = cs - adt
        w_c = dt * jnp.exp(key)
        e_c = jnp.exp(tot - key)
        mask = ci >= ri
    key_t = key.T
    dt_t = dt.T
    dec_c = jnp.broadcast_to(jnp.exp(tot), (2 * SUBLANE, LANE))
    spread = _select_right(jnp.concatenate([w_c, e_c, dec_c], axis=0), e_d, 2)
    w_e, e_off, decay = spread[:q], spread[q:2 * q], spread[2 * q:2 * q + 1]
    x_b = xs.astype(BF16)
    x_w = (xs * w_e).astype(BF16)
    st_b = st.astype(BF16)
    y_parts, st_parts, off_parts = [], [], []
    hg = SSD_HEADS // 2
    for g in range(2):
        bg = bm[:, g * SSD_STATE:(g + 1) * SSD_STATE]
        cg = cm[:, g * SSD_STATE:(g + 1) * SSD_STATE].astype(BF16)
        bg_t = bg.T.astype(BF16)
        gmat = _dot(cg, bg_t)
        gs = slice(g * hg * SSD_HEAD_DIM, (g + 1) * hg * SSD_HEAD_DIM)
        off_parts.append(_dot(cg, st_b[:, gs]))
        st_parts.append(_dot(bg_t, x_w[:, gs]))
        for hh in range(hg):
            h = g * hg + hh
            col = key[:, lane0 + h:lane0 + h + 1]
            row = key_t[lane0 + h:lane0 + h + 1, :]
            diff = (col - row) if fwd else (row - col)
            lm = jnp.exp(jnp.where(mask, diff, -1e30)) * dt_t[lane0 + h:lane0 + h + 1, :]
            s = (gmat * lm).astype(BF16)
            y_parts.append(_dot(s, x_b[:, h * SSD_HEAD_DIM:(h + 1) * SSD_HEAD_DIM]))
    y = jnp.concatenate(y_parts, axis=1) + jnp.concatenate(off_parts, axis=1) * e_off
    if fwd:
        y = y + d_row * xs
    st_new = st * decay + jnp.concatenate(st_parts, axis=1)
    return y, st_new


def _ssd_kernel(xf_ref, xfp_ref, xfn_ref, xb_ref, xbp_ref, xbn_ref, dtf_ref, dtb_ref, cw_ref, cb_ref,
                bias_ref, alog_ref, d_ref, e_ref, sf0_ref, sb0_ref, yf_ref, yb_ref, sf_ref, sb_ref, stf, stb):
    c = pl.program_id(1)
    nc = pl.num_programs(1)

    @pl.when(c == 0)
    def _():
        stf[...] = sf0_ref[0]
        stb[...] = sb0_ref[0]

    def conv_silu(u_ref, p_ref, n_ref, chunk):
        prev_row = jnp.where(chunk > 0, p_ref[0].astype(F32)[HALO - 1:HALO, :], 0.0)
        next_row = jnp.where(chunk < nc - 1, n_ref[0].astype(F32)[0:1, :], 0.0)
        return _silu(_dwconv3(u_ref[0].astype(F32), prev_row, next_row, cw_ref[...], cb_ref[...]))

    a = -jnp.exp(alog_ref[...])
    bias = bias_ref[...]
    xf = conv_silu(xf_ref, xfp_ref, xfn_ref, c)
    xb = conv_silu(xb_ref, xbp_ref, xbn_ref, nc - 1 - c)
    yf, sf = _ssd_chunk(xf, dtf_ref[0], a, bias, d_ref[...], e_ref[:, :SSD_INNER], stf[...], 0, True)
    yb, sb = _ssd_chunk(xb, dtb_ref[0], a, bias, d_ref[...], e_ref[:, SSD_INNER:], stb[...], SSD_HEADS, False)
    yf_ref[0] = yf.astype(yf_ref.dtype)
    yb_ref[0] = yb.astype(yb_ref.dtype)
    stf[...] = sf
    stb[...] = sb
    sf_ref[0] = sf
    sb_ref[0] = sb


def _ssd_scan(proj, dt_raw, sf0, sb0, p):
    b, l, _ = proj.shape
    nc = l // SSD_CHUNK
    q = SSD_CHUNK
    w = 2 * SSD_INNER
    xcol = P_X // w
    per = q // HALO
    nrb = l // HALO
    st_spec = pl.BlockSpec((1, SSD_STATE, SSD_INNER), lambda bi, c: (bi, 0, 0))
    y_shape = jax.ShapeDtypeStruct((b, l, SSD_INNER), BF16)
    st_shape = jax.ShapeDtypeStruct((b, SSD_STATE, SSD_INNER), F32)
    row = lambda n: pl.BlockSpec((1, n), lambda bi, c: (0, 0))

    def chunk_specs(chunk_of):
        return [pl.BlockSpec((1, q, w), lambda bi, c: (bi, chunk_of(c), xcol)),
                pl.BlockSpec((1, HALO, w), lambda bi, c: (bi, jnp.maximum(chunk_of(c) * per - 1, 0), xcol)),
                pl.BlockSpec((1, HALO, w), lambda bi, c: (bi, jnp.minimum((chunk_of(c) + 1) * per, nrb - 1), xcol))]

    fwd_of = lambda c: c
    bwd_of = lambda c: nc - 1 - c
    return pl.pallas_call(
        _ssd_kernel,
        grid=(b, nc),
        in_specs=chunk_specs(fwd_of) + chunk_specs(bwd_of) + [
            pl.BlockSpec((1, q, LANE), lambda bi, c: (bi, c, 0)),
            pl.BlockSpec((1, q, LANE), lambda bi, c: (bi, nc - 1 - c, 0)),
            pl.BlockSpec((SUBLANE, w), lambda bi, c: (0, 0)), row(w),
            row(LANE), row(LANE), row(SSD_INNER),
            pl.BlockSpec((LANE, 2 * SSD_INNER), lambda bi, c: (0, 0)),
            st_spec, st_spec],
        out_specs=[pl.BlockSpec((1, q, SSD_INNER), lambda bi, c: (bi, c, 0)),
                   pl.BlockSpec((1, q, SSD_INNER), lambda bi, c: (bi, nc - 1 - c, 0)),
                   st_spec, st_spec],
        out_shape=[y_shape, y_shape, st_shape, st_shape],
        scratch_shapes=[pltpu.VMEM((SSD_STATE, SSD_INNER), F32), pltpu.VMEM((SSD_STATE, SSD_INNER), F32)],
        compiler_params=_cp("parallel", "arbitrary"),
        name="ssd_scan",
    )(proj, proj, proj, proj, proj, proj, dt_raw, dt_raw, _pad_taps(p["ssd_conv_w"]), p["ssd_conv_b"][None],
      p["dt_bias_row"], p["a_log_row"], p["d_row"], p["head_expand"], sf0, sb0)


def _attn_prep_kernel(q_ref, k_ref, v_ref, cos_ref, sin_ref, gq_ref, gk_ref, gm_ref, qo_ref, ko_ref, vo_ref):
    cos = cos_ref[...]
    sin = sin_ref[...]
    gm = gm_ref[...]
    lane = lax.broadcasted_iota(jnp.int32, cos.shape, 1)
    first = jnp.bitwise_and(lane, 31) < 16
    hd = ATTN_HEAD_DIM

    def norm_rope(t, g):
        ms = _select_right(t * t, gm, 2)
        y = t * lax.rsqrt(ms + EPS) * g
        partner = jnp.where(first, pltpu.roll(y, LANE - 16, 1), pltpu.roll(y, 16, 1))
        return y * cos + partner * sin

    for s in range(ATTN_INNER // LANE):
        qs = norm_rope(q_ref[0, :, s * LANE:(s + 1) * LANE].astype(F32), gq_ref[...]) * (ATTN_SCALE * LOG2E)
        qt = qs.T.astype(BF16)
        qo_ref[0, 2 * s] = qt[:hd]
        qo_ref[0, 2 * s + 1] = qt[hd:]
    ks = norm_rope(k_ref[0].astype(F32), gk_ref[...])
    ko_ref[0, 0] = ks[:, :hd].astype(BF16)
    ko_ref[0, 1] = ks[:, hd:].astype(BF16)
    vt = v_ref[0].astype(F32).T.astype(BF16)
    tm = vt.shape[1]
    pad_rows = lax.broadcasted_iota(jnp.int32, (ATTN_V_ROWS - hd, tm), 0)
    tail = jnp.where(pad_rows == 0, 1.0, 0.0).astype(BF16)
    for g in range(ATTN_KV_HEADS):
        vo_ref[0, g, 0:hd, :] = vt[g * hd:(g + 1) * hd]
        vo_ref[0, g, hd:ATTN_V_ROWS, :] = tail


def _attn_prep(proj, cos, sin, gq, gk, gm):
    b, l, _ = proj.shape
    tm = min(l, 1024)
    hd = ATTN_HEAD_DIM
    const = lambda r, c: pl.BlockSpec((r, c), lambda bi, i: (0, 0))
    return pl.pallas_call(
        _attn_prep_kernel,
        grid=(b, l // tm),
        in_specs=[pl.BlockSpec((1, tm, ATTN_INNER), lambda bi, i: (bi, i, P_Q // ATTN_INNER)),
                  pl.BlockSpec((1, tm, LANE), lambda bi, i: (bi, i, P_K // LANE)),
                  pl.BlockSpec((1, tm, LANE), lambda bi, i: (bi, i, P_V // LANE)),
                  pl.BlockSpec((tm, LANE), lambda bi, i: (i, 0)),
                  pl.BlockSpec((tm, LANE), lambda bi, i: (i, 0)),
                  const(1, LANE), const(1, LANE), const(LANE, LANE)],
        out_specs=[pl.BlockSpec((1, ATTN_HEADS, hd, tm), lambda bi, i: (bi, 0, 0, i)),
                   pl.BlockSpec((1, ATTN_KV_HEADS, tm, hd), lambda bi, i: (bi, 0, i, 0)),
                   pl.BlockSpec((1, ATTN_KV_HEADS, ATTN_V_ROWS, tm), lambda bi, i: (bi, 0, 0, i))],
        out_shape=[jax.ShapeDtypeStruct((b, ATTN_HEADS, hd, l), BF16),
                   jax.ShapeDtypeStruct((b, ATTN_KV_HEADS, l, hd), BF16),
                   jax.ShapeDtypeStruct((b, ATTN_KV_HEADS, ATTN_V_ROWS, l), BF16)],
        compiler_params=_cp("parallel", "parallel"),
        name="attn_prep",
    )(proj, proj, proj, cos, sin, gq, gk, gm)


def _flash_kernel(q_ref, k_ref, v_ref, o_ref, s_ref, m_ref, acc_ref, *, tk, nk):
    r = ATTN_HEADS // ATTN_KV_HEADS
    hd = ATTN_HEAD_DIM

    def scores(j, h, slot):
        start = pl.multiple_of(j * tk, tk)
        s_ref[slot] = _dot(k_ref[0, 0, pl.ds(start, tk), :], q_ref[0, h])

    m_ref[...] = jnp.full(m_ref.shape, -1e30, F32)
    acc_ref[...] = jnp.zeros(acc_ref.shape, F32)
    scores(0, 0, 0)

    def body(j, _):
        start = pl.multiple_of(j * tk, tk)
        vs = v_ref[0, 0, :, pl.ds(start, tk)]
        j_next = jnp.minimum(j + 1, nk - 1)
        for h in range(r):
            if h < r - 1:
                scores(j, h + 1, (h + 1) % 2)
            else:
                scores(j_next, 0, 0)
            s = s_ref[h % 2]
            m = m_ref[h]
            mn = jnp.maximum(m, jnp.max(s, axis=0, keepdims=True))
            alpha = jnp.exp2(m - mn)
            p = jnp.exp2(s - mn).astype(BF16)
            m_ref[h] = mn
            acc_ref[h] = alpha * acc_ref[h] + _dot(vs, p)
        return 0

    lax.fori_loop(0, nk, body, 0)
    outs = []
    for h in range(r):
        acc = acc_ref[h]
        outs.append((acc[:hd] * (1.0 / acc[hd:hd + 1])).T)
    o_ref[0] = jnp.concatenate(outs, axis=1).astype(o_ref.dtype)


def _flash(q, k, v):
    b, _, hd, l = q.shape
    lk = k.shape[2]
    r = ATTN_HEADS // ATTN_KV_HEADS
    assert r % 2 == 0
    tq = min(l, 1024)
    tk = 384 if lk % 384 == 0 else 256
    return pl.pallas_call(
        functools.partial(_flash_kernel, tk=tk, nk=lk // tk),
        grid=(b, ATTN_KV_HEADS, l // tq),
        in_specs=[pl.BlockSpec((1, r, hd, tq), lambda bi, g, i: (bi, g, 0, i)),
                  pl.BlockSpec((1, 1, lk, hd), lambda bi, g, i: (bi, g, 0, 0)),
                  pl.BlockSpec((1, 1, ATTN_V_ROWS, lk), lambda bi, g, i: (bi, g, 0, 0))],
        out_specs=pl.BlockSpec((1, tq, r * hd), lambda bi, g, i: (bi, i, g)),
        out_shape=jax.ShapeDtypeStruct((b, l, ATTN_INNER), BF16),
        scratch_shapes=[pltpu.VMEM((2, tk, tq), F32), pltpu.VMEM((r, 1, tq), F32),
                        pltpu.VMEM((r, ATTN_V_ROWS, tq), F32)],
        compiler_params=_cp("parallel", "parallel", "parallel"),
        name="flash_gqa",
    )(q, k, v)


def _taps_kernel(ft_ref, t_ref, w1_ref, b1_ref, fr_ref, w2_ref, b2_ref, w3_ref, adel_ref, o_ref, *, group_major):
    fr = fr_ref[...]
    h = jnp.sin(fr * (_dot_hi(w1_ref[...], ft_ref[...]) + b1_ref[...]))
    h = jnp.sin(fr * (_dot_hi(w2_ref[...], h) + b2_ref[...]))
    y = _dot_hi(h.T, w3_ref[...])
    decay = jnp.exp(-t_ref[...] * adel_ref[...])
    f0 = y[:, :HY_WIDTH] * decay
    f1 = y[:, HY_WIDTH:] * decay
    lag = pl.program_id(0) * f1.shape[0] + lax.broadcasted_iota(jnp.int32, f1.shape, 0)
    f1 = jnp.where(lag == 0, 0.0, f1)
    o_ref[0] = _to_group_major(f0) if group_major else f0
    o_ref[1] = _to_group_major(f1) if group_major else f1


def _hy_taps(feats_t, t_col, p, adel, seq, group_major):
    tm = min(seq, 1024)
    const = lambda r, c: pl.BlockSpec((r, c), lambda i: (0, 0))
    if group_major:
        ng = FFT_R // FFT_G
        out = pl.BlockSpec((2, ng, tm // FFT_R, FFT_G, HY_WIDTH), lambda i: (0, 0, i, 0, 0))
        shp = jax.ShapeDtypeStruct((2, ng, seq // FFT_R, FFT_G, HY_WIDTH), F32)
    else:
        out = pl.BlockSpec((2, tm, HY_WIDTH), lambda i: (0, i, 0))
        shp = jax.ShapeDtypeStruct((2, seq, HY_WIDTH), F32)
    return pl.pallas_call(
        functools.partial(_taps_kernel, group_major=group_major),
        grid=(seq // tm,),
        in_specs=[pl.BlockSpec((LANE, tm), lambda i: (0, i)),
                  pl.BlockSpec((tm, 1), lambda i: (i, 0)),
                  const(HY_HIDDEN, LANE), const(HY_HIDDEN, 1), const(HY_HIDDEN, 1),
                  const(HY_HIDDEN, HY_HIDDEN), const(HY_HIDDEN, 1),
                  const(HY_HIDDEN, 2 * HY_WIDTH), const(1, HY_WIDTH)],
        out_specs=out,
        out_shape=shp,
        compiler_params=_cp("parallel"),
        name="hy_taps",
    )(feats_t, t_col, p["hy_w1p"].T, p["hy_b1"][:, None], p["hy_freq"][:, None], p["hy_w2"].T,
      p["hy_b2"][:, None], p["hy_w3"], adel)


def _fft1_kernel(x_ref, t_ref, are_ref, aim_ref, *, precise):
    x = x_ref[0]
    a = _dot3_table(t_ref[0], x) if precise else _dot1_table(t_ref[0], x)
    rows = FFT_K1 * FFT_G
    are_ref[0] = a[:rows]
    aim_ref[0] = a[rows:]


def _fft1(x, table, precise):
    nb, ng, rows, c = x.shape
    spec_o = pl.BlockSpec((1, None, FFT_K1 * FFT_G, c), lambda g, bi: (bi, g, 0, 0))
    shp = jax.ShapeDtypeStruct((nb, ng, FFT_K1 * FFT_G, c), F32)
    return pl.pallas_call(
        functools.partial(_fft1_kernel, precise=precise),
        grid=(ng, nb),
        in_specs=[pl.BlockSpec((1, None, rows, c), lambda g, bi: (bi, g, 0, 0)),
                  pl.BlockSpec((1,) + table.shape[1:], lambda g, bi: (g, 0, 0))],
        out_specs=[spec_o, spec_o],
        out_shape=[shp, shp],
        compiler_params=_cp("parallel", "parallel"),
        name="hy_fft1",
    )(x, table)


def _k1_rows(ref, b, kk):
    blk = ref[b, :, kk]
    return blk.reshape(blk.shape[0] * blk.shape[1], blk.shape[2])


def _fft2_filter_kernel(are_ref, aim_ref, fs_ref, kre_ref, kim_ref):
    r_ = FFT_R
    for kk in range(SUBLANE):
        xc = _dot3_table(fs_ref[...], jnp.concatenate([_k1_rows(are_ref, 0, kk), _k1_rows(aim_ref, 0, kk)], axis=0))
        xa = _dot3_table(fs_ref[...], jnp.concatenate([_k1_rows(are_ref, 1, kk), _k1_rows(aim_ref, 1, kk)], axis=0))
        kre_ref[kk] = xc[:r_] + xa[:r_]
        kim_ref[kk] = xc[r_:] - xa[r_:]


def _fft2_filter(are, aim, fs):
    r_ = FFT_R
    nb, ng, _, c = are.shape
    view = lambda a: a.reshape(nb, ng, FFT_K1, FFT_G, c)
    spec_i = pl.BlockSpec((nb, ng, SUBLANE, FFT_G, c), lambda kg: (0, 0, kg, 0, 0))
    spec_o = pl.BlockSpec((SUBLANE, r_, c), lambda kg: (kg, 0, 0))
    shp = jax.ShapeDtypeStruct((FFT_K1, r_, c), F32)
    return pl.pallas_call(
        _fft2_filter_kernel,
        grid=(FFT_K1 // SUBLANE,),
        in_specs=[spec_i, spec_i, pl.BlockSpec((2 * r_, 2 * r_), lambda kg: (0, 0))],
        out_specs=[spec_o, spec_o],
        out_shape=[shp, shp],
        compiler_params=_cp("parallel"),
        name="hy_fft2_filter",
    )(view(are), view(aim), fs)


def _fft2_kernel(are_ref, aim_ref, kre_ref, kim_ref, fs_ref, fc_ref, zre_ref, zim_ref):
    r_ = FFT_R
    ng = are_ref.shape[1]
    for kk in range(SUBLANE):
        a = jnp.concatenate([_k1_rows(are_ref, 0, kk), _k1_rows(aim_ref, 0, kk)], axis=0)
        b = _dot1_table(fs_ref[...], a)
        br, bi = b[:r_], b[r_:]
        kr, ki = kre_ref[kk], kim_ref[kk]
        y = jnp.concatenate([br * kr - bi * ki, br * ki + bi * kr], axis=0)
        z = _dot1_table(fc_ref[...], y)
        zre_ref[0, :, kk] = z[:r_].reshape(ng, FFT_G, z.shape[1])
        zim_ref[0, :, kk] = z[r_:].reshape(ng, FFT_G, z.shape[1])


def _fft2(are, aim, kre, kim, fs, fc):
    nb, ng, rows, c = are.shape
    r_ = FFT_R
    view = lambda a: a.reshape(nb, ng, FFT_K1, FFT_G, c)
    spec_a = pl.BlockSpec((1, ng, SUBLANE, FFT_G, c), lambda bi, kg: (bi, 0, kg, 0, 0))
    spec_k = pl.BlockSpec((SUBLANE, r_, c), lambda bi, kg: (kg, 0, 0))
    spec_f = pl.BlockSpec((2 * r_, 2 * r_), lambda bi, kg: (0, 0))
    shp = jax.ShapeDtypeStruct((nb, ng, FFT_K1, FFT_G, c), F32)
    zre, zim = pl.pallas_call(
        _fft2_kernel,
        grid=(nb, FFT_K1 // SUBLANE),
        in_specs=[spec_a, spec_a, spec_k, spec_k, spec_f, spec_f],
        out_specs=[spec_a, spec_a],
        out_shape=[shp, shp],
        compiler_params=_cp("parallel", "parallel"),
        name="hy_fft2",
    )(view(are), view(aim), kre, kim, fs, fc)
    return zre.reshape(are.shape), zim.reshape(are.shape)


def _ifft1_kernel(zre_ref, zim_ref, t_ref, vx_ref, x0_ref, bias_ref, o_ref):
    z = jnp.concatenate([zre_ref[0], zim_ref[0]], axis=0)
    conv = _dot1_table(t_ref[0], z)
    o_ref[0] = x0_ref[0] * (conv + bias_ref[...] * vx_ref[0])


def _ifft1(zre, zim, table, vx, x0, bias):
    nb, ng, rows, c = vx.shape
    spec_z = pl.BlockSpec((1, None, FFT_K1 * FFT_G, c), lambda g, bi: (bi, g, 0, 0))
    spec_x = pl.BlockSpec((1, None, rows, c), lambda g, bi: (bi, g, 0, 0))
    return pl.pallas_call(
        _ifft1_kernel,
        grid=(ng, nb),
        in_specs=[spec_z, spec_z,
                  pl.BlockSpec((1,) + table.shape[1:], lambda g, bi: (g, 0, 0)),
                  spec_x, spec_x,
                  pl.BlockSpec((1, c), lambda g, bi: (0, 0))],
        out_specs=spec_x,
        out_shape=jax.ShapeDtypeStruct(vx.shape, F32),
        compiler_params=_cp("parallel", "parallel"),
        name="hy_ifft1",
    )(zre, zim, table, vx, x0, bias)


def _hy_ctx_kernel(vx_ref, x0_ref, taps_ref, ff_ref, fi_ref, bias_ref, o_ref):
    vx = vx_ref[0]
    seq = vx.shape[0]
    n = 2 * seq
    u = _dot_hi(ff_ref[:, :seq], vx)
    k = _dot_hi(ff_ref[...], taps_ref[...])
    ur, ui = u[:n], u[n:]
    kr, ki = k[:n], k[n:]
    y = jnp.concatenate([ur * kr - ui * ki, ur * ki + ui * kr], axis=0)
    conv = _dot_hi(fi_ref[...], y)
    o_ref[0] = x0_ref[0] * (conv + bias_ref[...] * vx)


def _hy_ctx(vx, x0, taps, ff, fi, bias):
    b, seq, c = vx.shape
    n = 2 * seq
    spec_x = pl.BlockSpec((1, seq, c), lambda bi: (bi, 0, 0))
    const = lambda r, cc: pl.BlockSpec((r, cc), lambda bi: (0, 0))
    return pl.pallas_call(
        _hy_ctx_kernel,
        grid=(b,),
        in_specs=[spec_x, spec_x, const(n, c), const(2 * n, n), const(seq, 2 * n), const(1, c)],
        out_specs=spec_x,
        out_shape=jax.ShapeDtypeStruct((b, seq, c), F32),
        compiler_params=_cp("parallel"),
        name="hy_ctx",
    )(vx, x0, taps, ff, fi, bias)


def _mix_kernel(x_ref, yf_ref, yb_ref, z_ref, at_ref, hy_ref, g1_ref, gs_ref, ga_ref, gh_ref, w_ref, o_ref):
    ys = _rms((yf_ref[0].astype(F32) + yb_ref[0].astype(F32)) * _silu(z_ref[0].astype(F32)), gs_ref[...]).astype(BF16)
    ya = _rms(at_ref[0].astype(F32), ga_ref[...]).astype(BF16)
    hy = hy_ref[0]
    if hy.ndim == 4:
        hy = _from_group_major(hy)
    yh = _rms(hy, gh_ref[...]).astype(BF16)
    r = (_dot(ys, w_ref[0:SSD_INNER, :])
         + _dot(ya, w_ref[SSD_INNER:SSD_INNER + ATTN_INNER, :])
         + _dot(yh, w_ref[SSD_INNER + ATTN_INNER:, :]))
    o_ref[0] = x_ref[0] + g1_ref[0] * r


def _mix_out(x, yf, yb, proj, at, hy, g1, gs, ga, gh, w):
    b, l, _ = x.shape
    tm = min(l, 512)
    c = SSD_INNER
    t512 = lambda col: pl.BlockSpec((1, tm, c), lambda bi, i: (bi, i, col))
    const = lambda r, cc: pl.BlockSpec((r, cc), lambda bi, i: (0, 0))
    xs = pl.BlockSpec((1, tm, D_MODEL), lambda bi, i: (bi, i, 0))
    if hy.ndim == 5:
        hy_spec = pl.BlockSpec((1, hy.shape[1], tm // FFT_R, hy.shape[3], c), lambda bi, i: (bi, 0, i, 0, 0))
    else:
        hy_spec = t512(0)
    return pl.pallas_call(
        _mix_kernel,
        grid=(b, l // tm),
        in_specs=[xs, t512(0), t512(0), t512(P_Z // c), t512(0), hy_spec,
                  pl.BlockSpec((1, 1, D_MODEL), lambda bi, i: (bi, 0, 0)),
                  const(1, c), const(1, c), const(1, c), const(D_MIX, D_MODEL)],
        out_specs=xs,
        out_shape=jax.ShapeDtypeStruct(x.shape, F32),
        compiler_params=_cp("parallel", "parallel"),
        name="mix_out",
    )(x, yf, yb, proj, at, hy, g1, gs, ga, gh, w)


def _ffn_kernel(x_ref, sh_ref, sc_ref, g2_ref, ng_ref, wg_ref, wu_ref, wd_ref, fg_ref, o_ref, h_ref, acc_ref,
                *, final):
    j = pl.program_id(2)

    @pl.when(j == 0)
    def _():
        y = _rms(x_ref[0], ng_ref[...])
        h_ref[...] = (y * (1.0 + sc_ref[0]) + sh_ref[0]).astype(BF16)
        acc_ref[...] = jnp.zeros_like(acc_ref)

    h = h_ref[...]
    act = (_silu(_dot(h, wg_ref[...])) * _dot(h, wu_ref[...])).astype(BF16)
    acc_ref[...] += _dot(act, wd_ref[...])

    @pl.when(j == pl.num_programs(2) - 1)
    def _():
        y = x_ref[0] + g2_ref[0] * acc_ref[...]
        if final:
            y = _rms(y, fg_ref[...])
        o_ref[0] = y


def _ffn(x, sh, sc, g2, ng, w_gu, w_down, fg, final):
    b, l, _ = x.shape
    tm = min(l, 1024)
    tf = 256
    nf = D_FF // tf
    xs = pl.BlockSpec((1, tm, D_MODEL), lambda bi, i, j: (bi, i, 0))
    mod = pl.BlockSpec((1, 1, D_MODEL), lambda bi, i, j: (bi, 0, 0))
    row = pl.BlockSpec((1, D_MODEL), lambda bi, i, j: (0, 0))
    return pl.pallas_call(
        functools.partial(_ffn_kernel, final=final),
        grid=(b, l // tm, nf),
        in_specs=[xs, mod, mod, mod, row,
                  pl.BlockSpec((D_MODEL, tf), lambda bi, i, j: (0, j)),
                  pl.BlockSpec((D_MODEL, tf), lambda bi, i, j: (0, nf + j)),
                  pl.BlockSpec((tf, D_MODEL), lambda bi, i, j: (j, 0)),
                  row],
        out_specs=xs,
        out_shape=jax.ShapeDtypeStruct(x.shape, F32),
        scratch_shapes=[pltpu.VMEM((tm, D_MODEL), BF16), pltpu.VMEM((tm, D_MODEL), F32)],
        compiler_params=_cp("parallel", "parallel", "arbitrary"),
        name="ffn",
    )(x, sh, sc, g2, ng, w_gu, w_gu, w_down, fg)


def _rope_tables(seq):
    t = np.arange(seq)
    inv = ROPE_THETA ** (-np.arange(0, ROPE_AXIS_DIM, 2, dtype=np.float64) / ROPE_AXIS_DIM)
    ang_r = (t // GRID_W)[:, None] * inv
    ang_c = (t % GRID_W)[:, None] * inv
    cos = np.concatenate([np.cos(ang_r), np.cos(ang_r), np.cos(ang_c), np.cos(ang_c)], axis=1)
    sin = np.concatenate([-np.sin(ang_r), np.sin(ang_r), -np.sin(ang_c), np.sin(ang_c)], axis=1)
    return (jnp.asarray(np.tile(cos, (1, 2)), F32), jnp.asarray(np.tile(sin, (1, 2)), F32))


def _hy_feats(seq):
    t = np.linspace(0.0, 1.0, seq)[:, None]
    w = 2.0 * math.pi * np.arange(seq)[:, None] / seq
    f = np.linspace(1e-4, HY_BANDS - 1, HY_BANDS)
    feats = np.concatenate([t, np.cos(f * w), -np.sin(f * w)], axis=1)
    feats_t = np.pad(feats, ((0, 0), (0, LANE - HY_EMB))).T
    return jnp.asarray(feats_t, F32), jnp.asarray(t, F32)


def _hy_abs_deltas():
    lo = math.log(HY_TARGET) / HY_SLOW_DECAY_PCT
    hi = math.log(HY_TARGET) / HY_FAST_DECAY_PCT
    return jnp.asarray(np.abs(np.linspace(lo, hi, HY_WIDTH))[None], F32)


def _fft_tables():
    r_ = FFT_R
    n = r_ * r_
    n2 = np.arange(r_)[:, None, None]
    k1 = np.arange(FFT_K1)[None, :, None]
    n1 = np.arange(r_ // 2)[None, None, :]
    live = (k1 <= r_ // 2).astype(np.float64)
    th = 2.0 * math.pi * ((k1 * (r_ * n1 + n2)) % n) / n
    cos, msin = np.cos(th) * live, -np.sin(th) * live
    wgt = np.where((k1 == 0) | (k1 == r_ // 2), 1.0, 2.0) / n
    fwd = np.concatenate([_group_blocks(cos), _group_blocks(msin)], axis=1)
    inv = np.transpose(np.concatenate([_group_blocks(cos * wgt), _group_blocks(msin * wgt)], axis=1), (0, 2, 1))
    kk = np.arange(r_)
    ph = 2.0 * math.pi * ((kk[:, None] * kk[None, :]) % r_) / r_
    fr, fi = np.cos(ph), -np.sin(ph)
    fs = np.block([[fr, -fi], [fi, fr]])
    fc = np.block([[fr, fi], [-fi, fr]])
    return tuple(jnp.asarray(a, F32) for a in (fwd, inv, fs, fc))


def _group_blocks(t):
    n2, k1, n1 = t.shape
    ng = n2 // FFT_G
    out = np.zeros((ng, k1, FFT_G, n1, FFT_G))
    for r in range(FFT_G):
        out[:, :, r, :, r] = t.reshape(ng, FFT_G, k1, n1)[:, r]
    return out.reshape(ng, k1 * FFT_G, n1 * FFT_G)


def _dense_dft_tables(seq):
    n = 2 * seq
    kk = np.arange(n)
    ph = 2.0 * math.pi * ((kk[:, None] * kk[None, :]) % n) / n
    ff = np.concatenate([np.cos(ph), -np.sin(ph)], axis=0)
    fi = np.concatenate([np.cos(ph[:seq]), -np.sin(ph[:seq])], axis=1) / n
    return jnp.asarray(ff, F32), jnp.asarray(fi, F32)


def _head_expand():
    e = np.zeros((LANE, 2 * SSD_INNER), np.float32)
    for h in range(2 * SSD_HEADS):
        e[h, h * SSD_HEAD_DIM:(h + 1) * SSD_HEAD_DIM] = 1.0
    return jnp.asarray(e, BF16)


def _group_mean():
    g = np.kron(np.eye(LANE // ATTN_HEAD_DIM), np.ones((ATTN_HEAD_DIM, ATTN_HEAD_DIM))) / ATTN_HEAD_DIM
    return jnp.asarray(g, BF16)


def _relayout_w_in(w):
    cols = [w[:, OFF_HY:OFF_HY + 3 * HY_WIDTH], w[:, OFF_Q:OFF_Q + ATTN_INNER],
            w[:, OFF_XB:OFF_XB + SSD_XB], w[:, OFF_C:OFF_C + SSD_GN], w[:, OFF_Z:OFF_Z + SSD_INNER],
            w[:, OFF_K:OFF_K + ATTN_KV_INNER], w[:, OFF_V:OFF_V + ATTN_KV_INNER],
            w[:, OFF_DT:OFF_DT + 2 * SSD_HEADS]]
    wr = jnp.concatenate(cols, axis=1)
    return jnp.pad(wr, ((0, 0), (0, NP - wr.shape[1]))).astype(BF16)


def _pad_row(v):
    v = v.reshape(1, -1)
    return jnp.pad(v, ((0, 0), (0, LANE - v.shape[1])))


def _layer_params(l, raw, tables):
    p = {k: v[l] for k, v in raw.items()}
    p["w_in_r"] = _relayout_w_in(p["w_in"])
    p["dt_bias_row"] = _pad_row(p["ssd_dt_bias"])
    p["a_log_row"] = _pad_row(p["ssd_a_log"])
    p["d_row"] = jnp.repeat(p["ssd_d"], SSD_HEAD_DIM)[None]
    p["head_expand"] = tables["head_expand"]
    p["gq"] = jnp.tile(p["q_norm_g"], LANE // ATTN_HEAD_DIM)[None]
    p["gk"] = jnp.tile(p["k_norm_g"], LANE // ATTN_HEAD_DIM)[None]
    p["hy_w1p"] = jnp.pad(p["hy_w1"], ((0, LANE - HY_EMB), (0, 0)))
    p["w_out_b"] = p["w_out"].astype(BF16)
    p["w_gu_b"] = p["w_gu"].astype(BF16)
    p["w_down_b"] = p["w_down"].astype(BF16)
    return p


def _mixers(proj, dt_raw, p, tables, is_ctx, ssd_init):
    yf, yb, sf, sb = _ssd_scan(proj, dt_raw, ssd_init[0], ssd_init[1], p)
    rope = tables["rope_ctx"] if is_ctx else tables["rope"]
    q, k, v = _attn_prep(proj, rope[0], rope[1], p["gq"], p["gk"], tables["group_mean"])
    return (yf, yb, sf, sb), (q, k, v)


def _hyena(proj, p, tables, is_ctx):
    seq = proj.shape[1]
    bias = p["hy_bias"][None]
    vx, x0 = _hy_conv(proj, p["hy_conv_w"], p["hy_conv_b"], not is_ctx)
    if is_ctx:
        f = _hy_taps(*tables["feats_ctx"], p, tables["abs_deltas"], seq, False)
        taps = jnp.concatenate([f[0], jnp.zeros((1, HY_WIDTH), F32), jnp.flip(f[1, 1:], axis=0)], axis=0)
        return _hy_ctx(vx, x0, taps, tables["dft_ctx"][0], tables["dft_ctx"][1], bias)
    b, ng, n1, r, c = vx.shape
    t_fwd, t_inv, fs, fc = tables["fft"]
    f = _hy_taps(*tables["feats"], p, tables["abs_deltas"], seq, True)
    kre, kim = _fft2_filter(*_fft1(f.reshape(2, ng, n1 * r, c), t_fwd, True), fs)
    vx = vx.reshape(b, ng, n1 * r, c)
    are, aim = _fft1(vx, t_fwd, False)
    zre, zim = _fft2(are, aim, kre, kim, fs, fc)
    y = _ifft1(zre, zim, t_inv, vx, x0.reshape(b, ng, n1 * r, c), bias)
    return y.reshape(b, ng, n1, r, c)


def _tail(x, proj, ssd, at, hy, mod, p, final_g, final):
    x = _mix_out(x, ssd[0], ssd[1], proj, at, hy, mod[2], p["ssd_norm_g"][None], p["attn_norm_g"][None],
                 p["hy_norm_g"][None], p["w_out_b"])
    return _ffn(x, mod[3], mod[4], mod[5], p["norm2_g"][None], p["w_gu_b"], p["w_down_b"], final_g[None], final)


def _layer(x, xc, mod_rows, p, tables, final_g, last):
    b = x.shape[0]
    mod_x = [mod_rows[:b, i * D_MODEL:(i + 1) * D_MODEL][:, None, :] for i in range(N_MOD)]
    mod_c = [jnp.broadcast_to(mod_rows[b:b + 1, i * D_MODEL:(i + 1) * D_MODEL][:, None, :], (b, 1, D_MODEL))
             for i in range(N_MOD)]
    g1 = p["norm1_g"][None]
    proj, dt_raw = _proj_in(x, mod_x[0], mod_x[1], g1, p["w_in_r"])
    lc = xc.shape[1]
    rows = lambda a: a.reshape(1, b * lc, a.shape[-1])
    mod_c1 = [m[:1] for m in mod_c]
    projc, dt_raw_c = _proj_in(rows(xc), mod_c1[0], mod_c1[1], g1, p["w_in_r"])
    projc, dt_raw_c = projc.reshape(b, lc, -1), dt_raw_c.reshape(b, lc, -1)
    zeros = jnp.zeros((b, SSD_STATE, SSD_INNER), F32)
    ssd_c, qkv_c = _mixers(projc, dt_raw_c, p, tables, True, (zeros, zeros))
    ssd_x, qkv_x = _mixers(proj, dt_raw, p, tables, False, (ssd_c[2], ssd_c[3]))
    k_all = jnp.concatenate([qkv_c[1], qkv_x[1]], axis=2)
    v_all = jnp.concatenate([qkv_c[2], qkv_x[2]], axis=3)
    at = _flash(qkv_x[0], k_all, v_all)
    hy = _hyena(proj, p, tables, False)
    x = _tail(x, proj, ssd_x, at, hy, mod_x, p, final_g, last)
    if last:
        return x, None
    at_c = _flash(qkv_c[0], qkv_c[1], qkv_c[2])
    hy_c = _hyena(projc, p, tables, True)
    xc = _tail(rows(xc), rows(projc), (rows(ssd_c[0]), rows(ssd_c[1])), rows(at_c), rows(hy_c), mod_c1, p,
               final_g, False)
    return x, xc.reshape(b, lc, -1)


def kernel(x, c, ctx, c_ctx, w_mod, b_mod, norm1_g, w_in, ssd_conv_w, ssd_conv_b, ssd_a_log, ssd_dt_bias, ssd_d, ssd_norm_g, q_norm_g, k_norm_g, attn_norm_g, hy_conv_w, hy_conv_b, hy_w1, hy_b1, hy_freq, hy_w2, hy_b2, hy_w3, hy_bias, hy_norm_g, w_out, norm2_g, w_gu, w_down, final_g):
    b, seq, _ = x.shape
    ctx_len = ctx.shape[1]
    depth = w_mod.shape[0]
    assert 2 * seq == FFT_R * FFT_R and b + 1 <= SUBLANE
    raw = dict(w_in=w_in, ssd_conv_w=ssd_conv_w, ssd_conv_b=ssd_conv_b, ssd_a_log=ssd_a_log,
               ssd_dt_bias=ssd_dt_bias, ssd_d=ssd_d, ssd_norm_g=ssd_norm_g, q_norm_g=q_norm_g,
               k_norm_g=k_norm_g, attn_norm_g=attn_norm_g, hy_conv_w=hy_conv_w, hy_conv_b=hy_conv_b,
               hy_w1=hy_w1, hy_b1=hy_b1, hy_freq=hy_freq, hy_w2=hy_w2, hy_b2=hy_b2, hy_w3=hy_w3,
               hy_bias=hy_bias, hy_norm_g=hy_norm_g, w_out=w_out, norm2_g=norm2_g, w_gu=w_gu, w_down=w_down,
               norm1_g=norm1_g)
    ones = jnp.ones((ctx_len, LANE), F32)
    tables = dict(rope=_rope_tables(seq), rope_ctx=(ones, jnp.zeros_like(ones)),
                  feats=_hy_feats(seq), feats_ctx=_hy_feats(ctx_len), abs_deltas=_hy_abs_deltas(),
                  fft=_fft_tables(), dft_ctx=_dense_dft_tables(ctx_len),
                  head_expand=_head_expand(), group_mean=_group_mean())
    c_rows = jnp.concatenate([c, c_ctx[None], jnp.zeros((SUBLANE - b - 1, D_MODEL), F32)], axis=0)
    xc = ctx
    for l in range(depth):
        p = _layer_params(l, raw, tables)
        mod_rows = _mod_call(c_rows, w_mod[l], b_mod[l][None])
        x, xc = _layer(x, xc, mod_rows, p, tables, final_g, l == depth - 1)
    return x
```

```python
import functools
import math

import numpy as np
import jax
import jax.numpy as jnp
from jax import lax
from jax.experimental import pallas as pl
from jax.experimental.pallas import tpu as pltpu

F32 = jnp.float32
BF16 = jnp.bfloat16
HI = lax.Precision.HIGHEST

D_MODEL = 1024
GRID_W = 64
EPS = 1e-6
SSD_HEADS = 8
SSD_HEAD_DIM = 64
SSD_INNER = 512
SSD_STATE = 128
SSD_CHUNK = 128
SSD_GN = 256
SSD_XB = SSD_INNER + SSD_GN
ATTN_HEADS = 8
ATTN_KV_HEADS = 2
ATTN_HEAD_DIM = 64
ATTN_INNER = 512
ATTN_KV_INNER = 128
ATTN_SCALE = ATTN_HEAD_DIM ** -0.5
LOG2E = math.log2(math.e)
ATTN_V_ROWS = 80
ROPE_THETA = 10000.0
ROPE_AXIS_DIM = ATTN_HEAD_DIM // 2
HY_WIDTH = 512
HY_BANDS = 16
HY_EMB = 1 + 2 * HY_BANDS
HY_HIDDEN = 64
HY_FAST_DECAY_PCT = 0.3
HY_SLOW_DECAY_PCT = 1.5
HY_TARGET = 1e-2
D_MIX = 1536
D_FF = 2816
N_MOD = 6

OFF_K = 0
OFF_V = OFF_K + ATTN_KV_INNER
OFF_XB = OFF_V + ATTN_KV_INNER
OFF_DT = OFF_XB + SSD_XB
OFF_C = OFF_DT + 2 * SSD_HEADS
OFF_Q = OFF_C + SSD_GN
OFF_Z = OFF_Q + ATTN_INNER
OFF_HY = OFF_Z + SSD_INNER

P_HY = 0
P_Q = 1536
P_X = 2048
P_B = 2560
P_C = 2816
P_Z = 3072
P_K = 3584
P_V = 3712
P_DT = 3840
NP = 4096

LANE = 128
SUBLANE = 8
HALO = 16
VMEM_LIMIT = 48 * 1024 * 1024

FFT_R = 128
FFT_G = 8
FFT_K1 = 72


def _cp(*sem):
    return pltpu.CompilerParams(dimension_semantics=sem, vmem_limit_bytes=VMEM_LIMIT)


def _silu(x):
    return x * (1.0 / (1.0 + jnp.exp(-x)))


def _softplus(x):
    return jnp.maximum(x, 0.0) + jnp.log(1.0 + jnp.exp(-jnp.abs(x)))


def _rms(x, g):
    return x * lax.rsqrt(jnp.mean(x * x, axis=-1, keepdims=True) + EPS) * g


def _dot(a, b):
    return jnp.dot(a, b, preferred_element_type=F32)


def _dot_hi(a, b):
    return jnp.dot(a, b, precision=HI, preferred_element_type=F32)


def _split_bf16(x):
    hi = x.astype(BF16)
    return hi, (x - hi.astype(F32)).astype(BF16)


def _dot3_table(t, x):
    t_hi, t_lo = _split_bf16(t)
    x_hi, x_lo = _split_bf16(x)
    return _dot(t_hi, x_hi) + _dot(t_lo, x_hi) + _dot(t_hi, x_lo)


def _dot1_table(t, x):
    return _dot(t.astype(BF16), x.astype(BF16))


def _mod_kernel(c_ref, w_ref, b_ref, o_ref):
    o_ref[...] = _dot_hi(_silu(c_ref[...]), w_ref[...]) + b_ref[...]


def _mod_call(c_rows, w, b):
    n = w.shape[1]
    tn = 1024
    return pl.pallas_call(
        _mod_kernel,
        grid=(n // tn,),
        in_specs=[pl.BlockSpec((SUBLANE, D_MODEL), lambda j: (0, 0)),
                  pl.BlockSpec((D_MODEL, tn), lambda j: (0, j)),
                  pl.BlockSpec((1, tn), lambda j: (0, j))],
        out_specs=pl.BlockSpec((SUBLANE, tn), lambda j: (0, j)),
        out_shape=jax.ShapeDtypeStruct((SUBLANE, n), F32),
        compiler_params=_cp("parallel"),
        name="adaln_mod",
    )(c_rows, w, b)


def _proj_kernel(x_ref, sh_ref, sc_ref, g_ref, w_ref, o_ref, dt_ref, h_ref, *, dt_tile, dt_off):
    j = pl.program_id(2)

    @pl.when(j == 0)
    def _():
        y = _rms(x_ref[0], g_ref[...])
        h_ref[...] = (y * (1.0 + sc_ref[0]) + sh_ref[0]).astype(BF16)

    r = _dot(h_ref[...], w_ref[...])
    o_ref[0] = r.astype(BF16)

    @pl.when(j == dt_tile)
    def _():
        dt_ref[0] = r[:, dt_off:dt_off + LANE]


def _proj_in(x, sh, sc, g, w):
    b, l, _ = x.shape
    tm = min(l, 1024)
    tn = 1024
    return pl.pallas_call(
        functools.partial(_proj_kernel, dt_tile=P_DT // tn, dt_off=P_DT % tn),
        grid=(b, l // tm, NP // tn),
        in_specs=[pl.BlockSpec((1, tm, D_MODEL), lambda bi, i, j: (bi, i, 0)),
                  pl.BlockSpec((1, 1, D_MODEL), lambda bi, i, j: (bi, 0, 0)),
                  pl.BlockSpec((1, 1, D_MODEL), lambda bi, i, j: (bi, 0, 0)),
                  pl.BlockSpec((1, D_MODEL), lambda bi, i, j: (0, 0)),
                  pl.BlockSpec((D_MODEL, tn), lambda bi, i, j: (0, j))],
        out_specs=[pl.BlockSpec((1, tm, tn), lambda bi, i, j: (bi, i, j)),
                   pl.BlockSpec((1, tm, LANE), lambda bi, i, j: (bi, i, 0))],
        out_shape=[jax.ShapeDtypeStruct((b, l, NP), BF16), jax.ShapeDtypeStruct((b, l, LANE), F32)],
        scratch_shapes=[pltpu.VMEM((tm, D_MODEL), BF16)],
        compiler_params=_cp("parallel", "parallel", "arbitrary"),
        name="proj_in",
    )(x, sh, sc, g, w)


def _dwconv3(u, prev_row, next_row, w, b):
    tm = u.shape[0]
    ri = lax.broadcasted_iota(jnp.int32, u.shape, 0)
    um = jnp.where(ri == 0, prev_row, pltpu.roll(u, 1, 0))
    up = jnp.where(ri == tm - 1, next_row, pltpu.roll(u, tm - 1, 0))
    return um * w[0:1] + u * w[1:2] + up * w[2:3] + b


def _conv_group(refs, i, n_i):
    u_ref, p_ref, n_ref, w_ref, b_ref = refs
    prev_row = jnp.where(i > 0, p_ref[0].astype(F32)[HALO - 1:HALO, :], 0.0)
    next_row = jnp.where(i < n_i - 1, n_ref[0].astype(F32)[0:1, :], 0.0)
    return _dwconv3(u_ref[0].astype(F32), prev_row, next_row, w_ref[...], b_ref[...])


def _hy_conv_kernel(*refs, group_major):
    i, n_i = pl.program_id(1), pl.num_programs(1)
    v = _conv_group(refs[0:5], i, n_i)
    x1 = _conv_group(refs[5:10], i, n_i)
    x0 = _conv_group(refs[10:15], i, n_i)
    vx_ref, x0_ref = refs[15], refs[16]
    vx = v * x1
    if group_major:
        for ref, u in ((vx_ref, vx), (x0_ref, x0)):
            g = _to_group_major(u)
            ref[0] = g.reshape(g.shape[0], g.shape[1] // 2, 2 * g.shape[2], g.shape[3]).astype(ref.dtype)
    else:
        vx_ref[0] = vx
        x0_ref[0] = x0


def _to_group_major(u):
    rows, c = u.shape
    return jnp.swapaxes(u.reshape(rows // FFT_R, FFT_R // FFT_G, FFT_G, c), 0, 1)


def _from_group_major(u):
    g, n1, r, c = u.shape
    return jnp.swapaxes(u, 0, 1).reshape(n1 * g * r, c)


def _conv_specs(tm, tc, l, col_block, w_block):
    nrb = l // HALO
    per = tm // HALO
    return [
        pl.BlockSpec((1, tm, tc), lambda bi, i: (bi, i, col_block)),
        pl.BlockSpec((1, HALO, tc), lambda bi, i: (bi, jnp.maximum(i * per - 1, 0), col_block)),
        pl.BlockSpec((1, HALO, tc), lambda bi, i: (bi, jnp.minimum((i + 1) * per, nrb - 1), col_block)),
        pl.BlockSpec((SUBLANE, tc), lambda bi, i: (0, w_block)),
        pl.BlockSpec((1, tc), lambda bi, i: (0, w_block)),
    ]


def _pad_taps(w):
    return jnp.pad(w, ((0, SUBLANE - w.shape[0]), (0, 0)))


def _hy_conv(proj, conv_w, conv_b, group_major):
    b, l, _ = proj.shape
    tm = min(l, 1024)
    tc = HY_WIDTH
    wp = _pad_taps(conv_w)
    bp = conv_b[None]
    specs, args = [], []
    for grp in range(3):
        specs += _conv_specs(tm, tc, l, P_HY // tc + grp, grp)
        args += [proj, proj, proj, wp, bp]
    if group_major:
        ng = FFT_R // FFT_G
        out_spec = pl.BlockSpec((1, ng, tm // FFT_R // 2, 2 * FFT_G, tc), lambda bi, i: (bi, 0, i, 0, 0))
        shp = jax.ShapeDtypeStruct((b, ng, l // FFT_R // 2, 2 * FFT_G, tc), BF16)
    else:
        out_spec = pl.BlockSpec((1, tm, tc), lambda bi, i: (bi, i, 0))
        shp = jax.ShapeDtypeStruct((b, l, tc), F32)
    return pl.pallas_call(
        functools.partial(_hy_conv_kernel, group_major=group_major),
        grid=(b, l // tm),
        in_specs=specs,
        out_specs=[out_spec, out_spec],
        out_shape=[shp, shp],
        compiler_params=_cp("parallel", "parallel"),
        name="hy_conv",
    )(*args)


def _split_pieces(x, n):
    pieces, r = [], x
    for k in range(n):
        pc = r.astype(BF16)
        pieces.append(pc)
        if k + 1 < n:
            r = r - pc.astype(F32)
    return pieces


def _select_left(sel, x, n):
    w = x.shape[1]
    r = _dot(sel, jnp.concatenate(_split_pieces(x, n), axis=1))
    return sum(r[:, k * w:(k + 1) * w] for k in range(n))


def _select_right(x, sel, n):
    m = x.shape[0]
    r = _dot(jnp.concatenate(_split_pieces(x, n), axis=0), sel)
    return sum(r[k * m:(k + 1) * m] for k in range(n))


def _ssd_chunk(xbc, dt_raw, a, bias, d_row, e_d, st, lane0, fwd):
    q = SSD_CHUNK
    xs = xbc[:, :SSD_INNER]
    bm = xbc[:, SSD_INNER:SSD_XB]
    cm = xbc[:, SSD_XB:]
    ri = lax.broadcasted_iota(jnp.int32, (q, q), 0)
    ci = lax.broadcasted_iota(jnp.int32, (q, q), 1)
    dt = _softplus(dt_raw + bias)
    adt = dt * a
    cs = _select_left((ci <= ri).astype(BF16), adt, 3)
    tot = cs[q - 1:q, :]
    if fwd:
        key = cs
        w_c = dt * jnp.exp(tot - key)
        e_c = jnp.exp(key)
        mask = ci <= ri
    else:
        key = cs - adt
        w_c = dt * jnp.exp(key)
        e_c = jnp.exp(tot - key)
        mask = ci >= ri
    key_t = key.T
    dt_t = dt.T
    dec_c = jnp.broadcast_to(jnp.exp(tot), (2 * SUBLANE, LANE))
    spread = _select_right(jnp.concatenate([w_c, e_c, dec_c], axis=0), e_d, 2)
    w_e, e_off, decay = spread[:q], spread[q:2 * q], spread[2 * q:2 * q + 1]
    x_b = xs.astype(BF16)
    x_w = (xs * w_e).astype(BF16)
    st_b = st.astype(BF16)
    y_parts, st_parts, off_parts = [], [], []
    hg = SSD_HEADS // 2
    for g in range(2):
        bg = bm[:, g * SSD_STATE:(g + 1) * SSD_STATE]
        cg = cm[:, g * SSD_STATE:(g + 1) * SSD_STATE].astype(BF16)
        bg_t = bg.T.astype(BF16)
        gmat = _dot(cg, bg_t)
        gs = slice(g * hg * SSD_HEAD_DIM, (g + 1) * hg * SSD_HEAD_DIM)
        off_parts.append(_dot(cg, st_b[:, gs]))
        st_parts.append(_dot(bg_t, x_w[:, gs]))
        for hh in range(hg):
            h = g * hg + hh
            col = key[:, lane0 + h:lane0 + h + 1]
            row = key_t[lane0 + h:lane0 + h + 1, :]
            diff = (col - row) if fwd else (row - col)
            lm = jnp.exp(jnp.where(mask, diff, -1e30)) * dt_t[lane0 + h:lane0 + h + 1, :]
            s = (gmat * lm).astype(BF16)
            y_parts.append(_dot(s, x_b[:, h * SSD_HEAD_DIM:(h + 1) * SSD_HEAD_DIM]))
    y = jnp.concatenate(y_parts, axis=1) + jnp.concatenate(off_parts, axis=1) * e_off
    if fwd:
        y = y + d_row * xs
    st_new = st * decay + jnp.concatenate(st_parts, axis=1)
    return y, st_new


def _ssd_kernel(xf_ref, xfp_ref, xfn_ref, xb_ref, xbp_ref, xbn_ref, dtf_ref, dtb_ref, cw_ref, cb_ref,
                bias_ref, alog_ref, d_ref, e_ref, sf0_ref, sb0_ref, yf_ref, yb_ref, sf_ref, sb_ref, stf, stb):
    c = pl.program_id(1)
    nc = pl.num_programs(1)

    @pl.when(c == 0)
    def _():
        stf[...] = sf0_ref[0]
        stb[...] = sb0_ref[0]

    def conv_silu(u_ref, p_ref, n_ref, chunk):
        prev_row = jnp.where(chunk > 0, p_ref[0].astype(F32)[HALO - 1:HALO, :], 0.0)
        next_row = jnp.where(chunk < nc - 1, n_ref[0].astype(F32)[0:1, :], 0.0)
        return _silu(_dwconv3(u_ref[0].astype(F32), prev_row, next_row, cw_ref[...], cb_ref[...]))

    a = -jnp.exp(alog_ref[...])
    bias = bias_ref[...]
    xf = conv_silu(xf_ref, xfp_ref, xfn_ref, c)
    xb = conv_silu(xb_ref, xbp_ref, xbn_ref, nc - 1 - c)
    yf, sf = _ssd_chunk(xf, dtf_ref[0], a, bias, d_ref[...], e_ref[:, :SSD_INNER], stf[...], 0, True)
    yb, sb = _ssd_chunk(xb, dtb_ref[0], a, bias, d_ref[...], e_ref[:, SSD_INNER:], stb[...], SSD_HEADS, False)
    yf_ref[0] = yf.astype(yf_ref.dtype)
    yb_ref[0] = yb.astype(yb_ref.dtype)
    stf[...] = sf
    stb[...] = sb
    sf_ref[0] = sf
    sb_ref[0] = sb


def _ssd_scan(proj, dt_raw, sf0, sb0, p):
    b, l, _ = proj.shape
    nc = l // SSD_CHUNK
    q = SSD_CHUNK
    w = 2 * SSD_INNER
    xcol = P_X // w
    per = q // HALO
    nrb = l // HALO
    st_spec = pl.BlockSpec((1, SSD_STATE, SSD_INNER), lambda bi, c: (bi, 0, 0))
    y_shape = jax.ShapeDtypeStruct((b, l, SSD_INNER), BF16)
    st_shape = jax.ShapeDtypeStruct((b, SSD_STATE, SSD_INNER), F32)
    row = lambda n: pl.BlockSpec((1, n), lambda bi, c: (0, 0))

    def chunk_specs(chunk_of):
        return [pl.BlockSpec((1, q, w), lambda bi, c: (bi, chunk_of(c), xcol)),
                pl.BlockSpec((1, HALO, w), lambda bi, c: (bi, jnp.maximum(chunk_of(c) * per - 1, 0), xcol)),
                pl.BlockSpec((1, HALO, w), lambda bi, c: (bi, jnp.minimum((chunk_of(c) + 1) * per, nrb - 1), xcol))]

    fwd_of = lambda c: c
    bwd_of = lambda c: nc - 1 - c
    return pl.pallas_call(
        _ssd_kernel,
        grid=(b, nc),
        in_specs=chunk_specs(fwd_of) + chunk_specs(bwd_of) + [
            pl.BlockSpec((1, q, LANE), lambda bi, c: (bi, c, 0)),
            pl.BlockSpec((1, q, LANE), lambda bi, c: (bi, nc - 1 - c, 0)),
            pl.BlockSpec((SUBLANE, w), lambda bi, c: (0, 0)), row(w),
            row(LANE), row(LANE), row(SSD_INNER),
            pl.BlockSpec((LANE, 2 * SSD_INNER), lambda bi, c: (0, 0)),
            st_spec, st_spec],
        out_specs=[pl.BlockSpec((1, q, SSD_INNER), lambda bi, c: (bi, c, 0)),
                   pl.BlockSpec((1, q, SSD_INNER), lambda bi, c: (bi, nc - 1 - c, 0)),
                   st_spec, st_spec],
        out_shape=[y_shape, y_shape, st_shape, st_shape],
        scratch_shapes=[pltpu.VMEM((SSD_STATE, SSD_INNER), F32), pltpu.VMEM((SSD_STATE, SSD_INNER), F32)],
        compiler_params=_cp("parallel", "arbitrary"),
        name="ssd_scan",
    )(proj, proj, proj, proj, proj, proj, dt_raw, dt_raw, _pad_taps(p["ssd_conv_w"]), p["ssd_conv_b"][None],
      p["dt_bias_row"], p["a_log_row"], p["d_row"], p["head_expand"], sf0, sb0)


def _attn_prep_kernel(q_ref, k_ref, v_ref, cos_ref, sin_ref, gq_ref, gk_ref, gm_ref, qo_ref, ko_ref, vo_ref):
    cos = cos_ref[...]
    sin = sin_ref[...]
    gm = gm_ref[...]
    lane = lax.broadcasted_iota(jnp.int32, cos.shape, 1)
    first = jnp.bitwise_and(lane, 31) < 16
    hd = ATTN_HEAD_DIM

    def norm_rope(t, g):
        ms = _select_right(t * t, gm, 2)
        y = t * lax.rsqrt(ms + EPS) * g
        partner = jnp.where(first, pltpu.roll(y, LANE - 16, 1), pltpu.roll(y, 16, 1))
        return y * cos + partner * sin

    for s in range(ATTN_INNER // LANE):
        qs = norm_rope(q_ref[0, :, s * LANE:(s + 1) * LANE].astype(F32), gq_ref[...]) * (ATTN_SCALE * LOG2E)
        qt = qs.T.astype(BF16)
        qo_ref[0, 2 * s] = qt[:hd]
        qo_ref[0, 2 * s + 1] = qt[hd:]
    ks = norm_rope(k_ref[0].astype(F32), gk_ref[...])
    ko_ref[0, 0] = ks[:, :hd].astype(BF16)
    ko_ref[0, 1] = ks[:, hd:].astype(BF16)
    vt = v_ref[0].astype(F32).T.astype(BF16)
    tm = vt.shape[1]
    pad_rows = lax.broadcasted_iota(jnp.int32, (ATTN_V_ROWS - hd, tm), 0)
    tail = jnp.where(pad_rows == 0, 1.0, 0.0).astype(BF16)
    for g in range(ATTN_KV_HEADS):
        vo_ref[0, g, 0:hd, :] = vt[g * hd:(g + 1) * hd]
        vo_ref[0, g, hd:ATTN_V_ROWS, :] = tail


def _attn_prep(proj, cos, sin, gq, gk, gm):
    b, l, _ = proj.shape
    tm = min(l, 1024)
    hd = ATTN_HEAD_DIM
    const = lambda r, c: pl.BlockSpec((r, c), lambda bi, i: (0, 0))
    return pl.pallas_call(
        _attn_prep_kernel,
        grid=(b, l // tm),
        in_specs=[pl.BlockSpec((1, tm, ATTN_INNER), lambda bi, i: (bi, i, P_Q // ATTN_INNER)),
                  pl.BlockSpec((1, tm, LANE), lambda bi, i: (bi, i, P_K // LANE)),
                  pl.BlockSpec((1, tm, LANE), lambda bi, i: (bi, i, P_V // LANE)),
                  pl.BlockSpec((tm, LANE), lambda bi, i: (i, 0)),
                  pl.BlockSpec((tm, LANE), lambda bi, i: (i, 0)),
                  const(1, LANE), const(1, LANE), const(LANE, LANE)],
        out_specs=[pl.BlockSpec((1, ATTN_HEADS, hd, tm), lambda bi, i: (bi, 0, 0, i)),
                   pl.BlockSpec((1, ATTN_KV_HEADS, tm, hd), lambda bi, i: (bi, 0, i, 0)),
                   pl.BlockSpec((1, ATTN_KV_HEADS, ATTN_V_ROWS, tm), lambda bi, i: (bi, 0, 0, i))],
        out_shape=[jax.ShapeDtypeStruct((b, ATTN_HEADS, hd, l), BF16),
                   jax.ShapeDtypeStruct((b, ATTN_KV_HEADS, l, hd), BF16),
                   jax.ShapeDtypeStruct((b, ATTN_KV_HEADS, ATTN_V_ROWS, l), BF16)],
        compiler_params=_cp("parallel", "parallel"),
        name="attn_prep",
    )(proj, proj, proj, cos, sin, gq, gk, gm)


def _flash_kernel(q_ref, k_ref, v_ref, o_ref, s_ref, m_ref, acc_ref, *, tk, nk):
    r = ATTN_HEADS // ATTN_KV_HEADS
    hd = ATTN_HEAD_DIM

    def scores(j, h, slot):
        start = pl.multiple_of(j * tk, tk)
        s_ref[slot] = _dot(k_ref[0, 0, pl.ds(start, tk), :], q_ref[0, h])

    m_ref[...] = jnp.full(m_ref.shape, -1e30, F32)
    acc_ref[...] = jnp.zeros(acc_ref.shape, F32)
    scores(0, 0, 0)

    def body(j, _):
        start = pl.multiple_of(j * tk, tk)
        vs = v_ref[0, 0, :, pl.ds(start, tk)]
        j_next = jnp.minimum(j + 1, nk - 1)
        for h in range(r):
            if h < r - 1:
                scores(j, h + 1, (h + 1) % 2)
            else:
                scores(j_next, 0, 0)
            s = s_ref[h % 2]
            m = m_ref[h]
            mn = jnp.maximum(m, jnp.max(s, axis=0, keepdims=True))
            alpha = jnp.exp2(m - mn)
            p = jnp.exp2(s - mn).astype(BF16)
            m_ref[h] = mn
            acc_ref[h] = alpha * acc_ref[h] + _dot(vs, p)
        return 0

    lax.fori_loop(0, nk, body, 0)
    outs = []
    for h in range(r):
        acc = acc_ref[h]
        outs.append((acc[:hd] * (1.0 / acc[hd:hd + 1])).T)
    o_ref[0] = jnp.concatenate(outs, axis=1).astype(o_ref.dtype)


def _flash(q, k, v):
    b, _, hd, l = q.shape
    lk = k.shape[2]
    r = ATTN_HEADS // ATTN_KV_HEADS
    assert r % 2 == 0
    tq = min(l, 1024)
    tk = 384 if lk % 384 == 0 else 256
    return pl.pallas_call(
        functools.partial(_flash_kernel, tk=tk, nk=lk // tk),
        grid=(b, ATTN_KV_HEADS, l // tq),
        in_specs=[pl.BlockSpec((1, r, hd, tq), lambda bi, g, i: (bi, g, 0, i)),
                  pl.BlockSpec((1, 1, lk, hd), lambda bi, g, i: (bi, g, 0, 0)),
                  pl.BlockSpec((1, 1, ATTN_V_ROWS, lk), lambda bi, g, i: (bi, g, 0, 0))],
        out_specs=pl.BlockSpec((1, tq, r * hd), lambda bi, g, i: (bi, i, g)),
        out_shape=jax.ShapeDtypeStruct((b, l, ATTN_INNER), BF16),
        scratch_shapes=[pltpu.VMEM((2, tk, tq), F32), pltpu.VMEM((r, 1, tq), F32),
                        pltpu.VMEM((r, ATTN_V_ROWS, tq), F32)],
        compiler_params=_cp("parallel", "parallel", "parallel"),
        name="flash_gqa",
    )(q, k, v)


def _taps_kernel(ft_ref, t_ref, w1_ref, b1_ref, fr_ref, w2_ref, b2_ref, w3_ref, adel_ref, o_ref, *, group_major):
    fr = fr_ref[...]
    h = jnp.sin(fr * (_dot_hi(w1_ref[...], ft_ref[...]) + b1_ref[...]))
    h = jnp.sin(fr * (_dot_hi(w2_ref[...], h) + b2_ref[...]))
    y = _dot_hi(h.T, w3_ref[...])
    decay = jnp.exp(-t_ref[...] * adel_ref[...])
    f0 = y[:, :HY_WIDTH] * decay
    f1 = y[:, HY_WIDTH:] * decay
    lag = pl.program_id(0) * f1.shape[0] + lax.broadcasted_iota(jnp.int32, f1.shape, 0)
    f1 = jnp.where(lag == 0, 0.0, f1)
    o_ref[0] = _to_group_major(f0) if group_major else f0
    o_ref[1] = _to_group_major(f1) if group_major else f1


def _hy_taps(feats_t, t_col, p, adel, seq, group_major):
    tm = min(seq, 1024)
    const = lambda r, c: pl.BlockSpec((r, c), lambda i: (0, 0))
    if group_major:
        ng = FFT_R // FFT_G
        out = pl.BlockSpec((2, ng, tm // FFT_R, FFT_G, HY_WIDTH), lambda i: (0, 0, i, 0, 0))
        shp = jax.ShapeDtypeStruct((2, ng, seq // FFT_R, FFT_G, HY_WIDTH), F32)
    else:
        out = pl.BlockSpec((2, tm, HY_WIDTH), lambda i: (0, i, 0))
        shp = jax.ShapeDtypeStruct((2, seq, HY_WIDTH), F32)
    return pl.pallas_call(
        functools.partial(_taps_kernel, group_major=group_major),
        grid=(seq // tm,),
        in_specs=[pl.BlockSpec((LANE, tm), lambda i: (0, i)),
                  pl.BlockSpec((tm, 1), lambda i: (i, 0)),
                  const(HY_HIDDEN, LANE), const(HY_HIDDEN, 1), const(HY_HIDDEN, 1),
                  const(HY_HIDDEN, HY_HIDDEN), const(HY_HIDDEN, 1),
                  const(HY_HIDDEN, 2 * HY_WIDTH), const(1, HY_WIDTH)],
        out_specs=out,
        out_shape=shp,
        compiler_params=_cp("parallel"),
        name="hy_taps",
    )(feats_t, t_col, p["hy_w1p"].T, p["hy_b1"][:, None], p["hy_freq"][:, None], p["hy_w2"].T,
      p["hy_b2"][:, None], p["hy_w3"], adel)


def _fft1_kernel(x_ref, t_ref, are_ref, aim_ref, *, precise):
    x = x_ref[0]
    a = _dot3_table(t_ref[0], x) if precise else _dot1_table(t_ref[0], x)
    rows = FFT_K1 * FFT_G
    are_ref[0] = a[:rows].astype(are_ref.dtype)
    aim_ref[0] = a[rows:].astype(aim_ref.dtype)


def _fft1(x, table, precise):
    nb, ng, rows, c = x.shape
    spec_o = pl.BlockSpec((1, None, FFT_K1 * FFT_G, c), lambda g, bi: (bi, g, 0, 0))
    shp = jax.ShapeDtypeStruct((nb, ng, FFT_K1 * FFT_G, c), F32 if precise else BF16)
    return pl.pallas_call(
        functools.partial(_fft1_kernel, precise=precise),
        grid=(ng, nb),
        in_specs=[pl.BlockSpec((1, None, rows, c), lambda g, bi: (bi, g, 0, 0)),
                  pl.BlockSpec((1,) + table.shape[1:], lambda g, bi: (g, 0, 0))],
        out_specs=[spec_o, spec_o],
        out_shape=[shp, shp],
        compiler_params=_cp("parallel", "parallel"),
        name="hy_fft1",
    )(x, table)


def _k1_rows(ref, b, kk):
    blk = ref[b, :, kk]
    return blk.reshape(blk.shape[0] * blk.shape[1], blk.shape[2])


def _fft2_filter_kernel(are_ref, aim_ref, fs_ref, kre_ref, kim_ref):
    r_ = FFT_R
    for kk in range(SUBLANE):
        xc = _dot3_table(fs_ref[...], jnp.concatenate([_k1_rows(are_ref, 0, kk), _k1_rows(aim_ref, 0, kk)], axis=0))
        xa = _dot3_table(fs_ref[...], jnp.concatenate([_k1_rows(are_ref, 1, kk), _k1_rows(aim_ref, 1, kk)], axis=0))
        kre_ref[kk] = xc[:r_] + xa[:r_]
        kim_ref[kk] = xc[r_:] - xa[r_:]


def _fft2_filter(are, aim, fs):
    r_ = FFT_R
    nb, ng, _, c = are.shape
    view = lambda a: a.reshape(nb, ng, FFT_K1, FFT_G, c)
    spec_i = pl.BlockSpec((nb, ng, SUBLANE, FFT_G, c), lambda kg: (0, 0, kg, 0, 0))
    spec_o = pl.BlockSpec((SUBLANE, r_, c), lambda kg: (kg, 0, 0))
    shp = jax.ShapeDtypeStruct((FFT_K1, r_, c), F32)
    return pl.pallas_call(
        _fft2_filter_kernel,
        grid=(FFT_K1 // SUBLANE,),
        in_specs=[spec_i, spec_i, pl.BlockSpec((2 * r_, 2 * r_), lambda kg: (0, 0))],
        out_specs=[spec_o, spec_o],
        out_shape=[shp, shp],
        compiler_params=_cp("parallel"),
        name="hy_fft2_filter",
    )(view(are), view(aim), fs)


def _fft2_kernel(are_ref, aim_ref, kre_ref, kim_ref, fs_ref, fc_ref, zre_ref, zim_ref):
    r_ = FFT_R
    nb, ng, c = are_ref.shape[0], are_ref.shape[1], are_ref.shape[4]
    for b in range(nb):
        for kp in range(SUBLANE // 2):
            a_re = are_ref[b, :, kp].astype(F32)
            a_im = aim_ref[b, :, kp].astype(F32)
            z_re, z_im = [], []
            for par in range(2):
                kk = 2 * kp + par
                rows = slice(par * FFT_G, (par + 1) * FFT_G)
                a = jnp.concatenate([a_re[:, rows].reshape(r_, c), a_im[:, rows].reshape(r_, c)], axis=0)
                bb = _dot1_table(fs_ref[...], a)
                br, bi = bb[:r_], bb[r_:]
                kr, ki = kre_ref[kk], kim_ref[kk]
                y = jnp.concatenate([br * kr - bi * ki, br * ki + bi * kr], axis=0)
                z = _dot1_table(fc_ref[...], y)
                z_re.append(z[:r_].reshape(ng, FFT_G, c))
                z_im.append(z[r_:].reshape(ng, FFT_G, c))
            zre_ref[b, :, kp] = jnp.concatenate(z_re, axis=1).astype(zre_ref.dtype)
            zim_ref[b, :, kp] = jnp.concatenate(z_im, axis=1).astype(zim_ref.dtype)


def _fft2(are, aim, kre, kim, fs, fc):
    nb, ng, rows, c = are.shape
    r_ = FFT_R
    view = lambda a: a.reshape(nb, ng, FFT_K1 // 2, 2 * FFT_G, c)
    spec_a = pl.BlockSpec((nb, ng, SUBLANE // 2, 2 * FFT_G, c), lambda kg: (0, 0, kg, 0, 0))
    spec_k = pl.BlockSpec((SUBLANE, r_, c), lambda kg: (kg, 0, 0))
    spec_f = pl.BlockSpec((2 * r_, 2 * r_), lambda kg: (0, 0))
    shp = jax.ShapeDtypeStruct((nb, ng, FFT_K1 // 2, 2 * FFT_G, c), BF16)
    zre, zim = pl.pallas_call(
        _fft2_kernel,
        grid=(FFT_K1 // SUBLANE,),
        in_specs=[spec_a, spec_a, spec_k, spec_k, spec_f, spec_f],
        out_specs=[spec_a, spec_a],
        out_shape=[shp, shp],
        compiler_params=_cp("parallel"),
        name="hy_fft2",
    )(view(are), view(aim), kre, kim, fs, fc)
    return zre.reshape(are.shape), zim.reshape(are.shape)


def _ifft1_kernel(zre_ref, zim_ref, t_ref, vx_ref, x0_ref, bias_ref, o_ref):
    z = jnp.concatenate([zre_ref[0], zim_ref[0]], axis=0)
    conv = _dot1_table(t_ref[0], z)
    o_ref[0] = (x0_ref[0].astype(F32) * (conv + bias_ref[...] * vx_ref[0].astype(F32))).astype(o_ref.dtype)


def _ifft1(zre, zim, table, vx, x0, bias):
    nb, ng, rows, c = vx.shape
    spec_z = pl.BlockSpec((1, None, FFT_K1 * FFT_G, c), lambda g, bi: (bi, g, 0, 0))
    spec_x = pl.BlockSpec((1, None, rows, c), lambda g, bi: (bi, g, 0, 0))
    return pl.pallas_call(
        _ifft1_kernel,
        grid=(ng, nb),
        in_specs=[spec_z, spec_z,
                  pl.BlockSpec((1,) + table.shape[1:], lambda g, bi: (g, 0, 0)),
                  spec_x, spec_x,
                  pl.BlockSpec((1, c), lambda g, bi: (0, 0))],
        out_specs=spec_x,
        out_shape=jax.ShapeDtypeStruct(vx.shape, BF16),
        compiler_params=_cp("parallel", "parallel"),
        name="hy_ifft1",
    )(zre, zim, table, vx, x0, bias)


def _hy_ctx_kernel(vx_ref, x0_ref, taps_ref, ff_ref, fi_ref, bias_ref, o_ref):
    vx = vx_ref[0]
    seq = vx.shape[0]
    n = 2 * seq
    u = _dot_hi(ff_ref[:, :seq], vx)
    k = _dot_hi(ff_ref[...], taps_ref[...])
    ur, ui = u[:n], u[n:]
    kr, ki = k[:n], k[n:]
    y = jnp.concatenate([ur * kr - ui * ki, ur * ki + ui * kr], axis=0)
    conv = _dot_hi(fi_ref[...], y)
    o_ref[0] = x0_ref[0] * (conv + bias_ref[...] * vx)


def _hy_ctx(vx, x0, taps, ff, fi, bias):
    b, seq, c = vx.shape
    n = 2 * seq
    spec_x = pl.BlockSpec((1, seq, c), lambda bi: (bi, 0, 0))
    const = lambda r, cc: pl.BlockSpec((r, cc), lambda bi: (0, 0))
    return pl.pallas_call(
        _hy_ctx_kernel,
        grid=(b,),
        in_specs=[spec_x, spec_x, const(n, c), const(2 * n, n), const(seq, 2 * n), const(1, c)],
        out_specs=spec_x,
        out_shape=jax.ShapeDtypeStruct((b, seq, c), F32),
        compiler_params=_cp("parallel"),
        name="hy_ctx",
    )(vx, x0, taps, ff, fi, bias)


def _mix_kernel(x_ref, yf_ref, yb_ref, z_ref, at_ref, hy_ref, g1_ref, gs_ref, ga_ref, gh_ref, w_ref, o_ref):
    ys = _rms((yf_ref[0].astype(F32) + yb_ref[0].astype(F32)) * _silu(z_ref[0].astype(F32)), gs_ref[...]).astype(BF16)
    ya = _rms(at_ref[0].astype(F32), ga_ref[...]).astype(BF16)
    hy = hy_ref[0].astype(F32)
    if hy.ndim == 4:
        hy = _from_group_major(hy.reshape(hy.shape[0], -1, FFT_G, hy.shape[3]))
    yh = _rms(hy, gh_ref[...]).astype(BF16)
    r = (_dot(ys, w_ref[0:SSD_INNER, :])
         + _dot(ya, w_ref[SSD_INNER:SSD_INNER + ATTN_INNER, :])
         + _dot(yh, w_ref[SSD_INNER + ATTN_INNER:, :]))
    o_ref[0] = x_ref[0] + g1_ref[0] * r


def _mix_out(x, yf, yb, proj, at, hy, g1, gs, ga, gh, w):
    b, l, _ = x.shape
    tm = min(l, 512)
    c = SSD_INNER
    t512 = lambda col: pl.BlockSpec((1, tm, c), lambda bi, i: (bi, i, col))
    const = lambda r, cc: pl.BlockSpec((r, cc), lambda bi, i: (0, 0))
    xs = pl.BlockSpec((1, tm, D_MODEL), lambda bi, i: (bi, i, 0))
    if hy.ndim == 5:
        hy_spec = pl.BlockSpec((1, hy.shape[1], tm // FFT_R // 2, hy.shape[3], c), lambda bi, i: (bi, 0, i, 0, 0))
    else:
        hy_spec = t512(0)
    return pl.pallas_call(
        _mix_kernel,
        grid=(b, l // tm),
        in_specs=[xs, t512(0), t512(0), t512(P_Z // c), t512(0), hy_spec,
                  pl.BlockSpec((1, 1, D_MODEL), lambda bi, i: (bi, 0, 0)),
                  const(1, c), const(1, c), const(1, c), const(D_MIX, D_MODEL)],
        out_specs=xs,
        out_shape=jax.ShapeDtypeStruct(x.shape, F32),
        compiler_params=_cp("parallel", "parallel"),
        name="mix_out",
    )(x, yf, yb, proj, at, hy, g1, gs, ga, gh, w)


def _ffn_kernel(x_ref, sh_ref, sc_ref, g2_ref, ng_ref, wg_ref, wu_ref, wd_ref, fg_ref, o_ref, h_ref, acc_ref,
                *, final):
    j = pl.program_id(2)

    @pl.when(j == 0)
    def _():
        y = _rms(x_ref[0], ng_ref[...])
        h_ref[...] = (y * (1.0 + sc_ref[0]) + sh_ref[0]).astype(BF16)
        acc_ref[...] = jnp.zeros_like(acc_ref)

    h = h_ref[...]
    act = (_silu(_dot(h, wg_ref[...])) * _dot(h, wu_ref[...])).astype(BF16)
    acc_ref[...] += _dot(act, wd_ref[...])

    @pl.when(j == pl.num_programs(2) - 1)
    def _():
        y = x_ref[0] + g2_ref[0] * acc_ref[...]
        if final:
            y = _rms(y, fg_ref[...])
        o_ref[0] = y


def _ffn(x, sh, sc, g2, ng, w_gu, w_down, fg, final):
    b, l, _ = x.shape
    tm = min(l, 1024)
    tf = 256
    nf = D_FF // tf
    xs = pl.BlockSpec((1, tm, D_MODEL), lambda bi, i, j: (bi, i, 0))
    mod = pl.BlockSpec((1, 1, D_MODEL), lambda bi, i, j: (bi, 0, 0))
    row = pl.BlockSpec((1, D_MODEL), lambda bi, i, j: (0, 0))
    return pl.pallas_call(
        functools.partial(_ffn_kernel, final=final),
        grid=(b, l // tm, nf),
        in_specs=[xs, mod, mod, mod, row,
                  pl.BlockSpec((D_MODEL, tf), lambda bi, i, j: (0, j)),
                  pl.BlockSpec((D_MODEL, tf), lambda bi, i, j: (0, nf + j)),
                  pl.BlockSpec((tf, D_MODEL), lambda bi, i, j: (j, 0)),
                  row],
        out_specs=xs,
        out_shape=jax.ShapeDtypeStruct(x.shape, F32),
        scratch_shapes=[pltpu.VMEM((tm, D_MODEL), BF16), pltpu.VMEM((tm, D_MODEL), F32)],
        compiler_params=_cp("parallel", "parallel", "arbitrary"),
        name="ffn",
    )(x, sh, sc, g2, ng, w_gu, w_gu, w_down, fg)


def _rope_tables(seq):
    t = np.arange(seq)
    inv = ROPE_THETA ** (-np.arange(0, ROPE_AXIS_DIM, 2, dtype=np.float64) / ROPE_AXIS_DIM)
    ang_r = (t // GRID_W)[:, None] * inv
    ang_c = (t % GRID_W)[:, None] * inv
    cos = np.concatenate([np.cos(ang_r), np.cos(ang_r), np.cos(ang_c), np.cos(ang_c)], axis=1)
    sin = np.concatenate([-np.sin(ang_r), np.sin(ang_r), -np.sin(ang_c), np.sin(ang_c)], axis=1)
    return (jnp.asarray(np.tile(cos, (1, 2)), F32), jnp.asarray(np.tile(sin, (1, 2)), F32))


def _hy_feats(seq):
    t = np.linspace(0.0, 1.0, seq)[:, None]
    w = 2.0 * math.pi * np.arange(seq)[:, None] / seq
    f = np.linspace(1e-4, HY_BANDS - 1, HY_BANDS)
    feats = np.concatenate([t, np.cos(f * w), -np.sin(f * w)], axis=1)
    feats_t = np.pad(feats, ((0, 0), (0, LANE - HY_EMB))).T
    return jnp.asarray(feats_t, F32), jnp.asarray(t, F32)


def _hy_abs_deltas():
    lo = math.log(HY_TARGET) / HY_SLOW_DECAY_PCT
    hi = math.log(HY_TARGET) / HY_FAST_DECAY_PCT
    return jnp.asarray(np.abs(np.linspace(lo, hi, HY_WIDTH))[None], F32)


def _fft_tables():
    r_ = FFT_R
    n = r_ * r_
    n2 = np.arange(r_)[:, None, None]
    k1 = np.arange(FFT_K1)[None, :, None]
    n1 = np.arange(r_ // 2)[None, None, :]
    live = (k1 <= r_ // 2).astype(np.float64)
    th = 2.0 * math.pi * ((k1 * (r_ * n1 + n2)) % n) / n
    cos, msin = np.cos(th) * live, -np.sin(th) * live
    wgt = np.where((k1 == 0) | (k1 == r_ // 2), 1.0, 2.0) / n
    fwd = np.concatenate([_group_blocks(cos), _group_blocks(msin)], axis=1)
    inv = np.transpose(np.concatenate([_group_blocks(cos * wgt), _group_blocks(msin * wgt)], axis=1), (0, 2, 1))
    kk = np.arange(r_)
    ph = 2.0 * math.pi * ((kk[:, None] * kk[None, :]) % r_) / r_
    fr, fi = np.cos(ph), -np.sin(ph)
    fs = np.block([[fr, -fi], [fi, fr]])
    fc = np.block([[fr, fi], [-fi, fr]])
    return tuple(jnp.asarray(a, F32) for a in (fwd, inv, fs, fc))


def _group_blocks(t):
    n2, k1, n1 = t.shape
    ng = n2 // FFT_G
    out = np.zeros((ng, k1, FFT_G, n1, FFT_G))
    for r in range(FFT_G):
        out[:, :, r, :, r] = t.reshape(ng, FFT_G, k1, n1)[:, r]
    return out.reshape(ng, k1 * FFT_G, n1 * FFT_G)


def _dense_dft_tables(seq):
    n = 2 * seq
    kk = np.arange(n)
    ph = 2.0 * math.pi * ((kk[:, None] * kk[None, :]) % n) / n
    ff = np.concatenate([np.cos(ph), -np.sin(ph)], axis=0)
    fi = np.concatenate([np.cos(ph[:seq]), -np.sin(ph[:seq])], axis=1) / n
    return jnp.asarray(ff, F32), jnp.asarray(fi, F32)


def _head_expand():
    e = np.zeros((LANE, 2 * SSD_INNER), np.float32)
    for h in range(2 * SSD_HEADS):
        e[h, h * SSD_HEAD_DIM:(h + 1) * SSD_HEAD_DIM] = 1.0
    return jnp.asarray(e, BF16)


def _group_mean():
    g = np.kron(np.eye(LANE // ATTN_HEAD_DIM), np.ones((ATTN_HEAD_DIM, ATTN_HEAD_DIM))) / ATTN_HEAD_DIM
    return jnp.asarray(g, BF16)


def _relayout_w_in(w):
    cols = [w[:, OFF_HY:OFF_HY + 3 * HY_WIDTH], w[:, OFF_Q:OFF_Q + ATTN_INNER],
            w[:, OFF_XB:OFF_XB + SSD_XB], w[:, OFF_C:OFF_C + SSD_GN], w[:, OFF_Z:OFF_Z + SSD_INNER],
            w[:, OFF_K:OFF_K + ATTN_KV_INNER], w[:, OFF_V:OFF_V + ATTN_KV_INNER],
            w[:, OFF_DT:OFF_DT + 2 * SSD_HEADS]]
    wr = jnp.concatenate(cols, axis=1)
    return jnp.pad(wr, ((0, 0), (0, NP - wr.shape[1]))).astype(BF16)


def _pad_row(v):
    v = v.reshape(1, -1)
    return jnp.pad(v, ((0, 0), (0, LANE - v.shape[1])))


def _layer_params(l, raw, tables):
    p = {k: v[l] for k, v in raw.items()}
    p["w_in_r"] = _relayout_w_in(p["w_in"])
    p["dt_bias_row"] = _pad_row(p["ssd_dt_bias"])
    p["a_log_row"] = _pad_row(p["ssd_a_log"])
    p["d_row"] = jnp.repeat(p["ssd_d"], SSD_HEAD_DIM)[None]
    p["head_expand"] = tables["head_expand"]
    p["gq"] = jnp.tile(p["q_norm_g"], LANE // ATTN_HEAD_DIM)[None]
    p["gk"] = jnp.tile(p["k_norm_g"], LANE // ATTN_HEAD_DIM)[None]
    p["hy_w1p"] = jnp.pad(p["hy_w1"], ((0, LANE - HY_EMB), (0, 0)))
    p["w_out_b"] = p["w_out"].astype(BF16)
    p["w_gu_b"] = p["w_gu"].astype(BF16)
    p["w_down_b"] = p["w_down"].astype(BF16)
    return p


def _mixers(proj, dt_raw, p, tables, is_ctx, ssd_init):
    yf, yb, sf, sb = _ssd_scan(proj, dt_raw, ssd_init[0], ssd_init[1], p)
    rope = tables["rope_ctx"] if is_ctx else tables["rope"]
    q, k, v = _attn_prep(proj, rope[0], rope[1], p["gq"], p["gk"], tables["group_mean"])
    return (yf, yb, sf, sb), (q, k, v)


def _hyena(proj, p, tables, is_ctx):
    seq = proj.shape[1]
    bias = p["hy_bias"][None]
    vx, x0 = _hy_conv(proj, p["hy_conv_w"], p["hy_conv_b"], not is_ctx)
    if is_ctx:
        f = _hy_taps(*tables["feats_ctx"], p, tables["abs_deltas"], seq, False)
        taps = jnp.concatenate([f[0], jnp.zeros((1, HY_WIDTH), F32), jnp.flip(f[1, 1:], axis=0)], axis=0)
        return _hy_ctx(vx, x0, taps, tables["dft_ctx"][0], tables["dft_ctx"][1], bias)
    b, ng, _, _, c = vx.shape
    rows = seq // ng
    t_fwd, t_inv, fs, fc = tables["fft"]
    f = _hy_taps(*tables["feats"], p, tables["abs_deltas"], seq, True)
    kre, kim = _fft2_filter(*_fft1(f.reshape(2, ng, rows, c), t_fwd, True), fs)
    vx = vx.reshape(b, ng, rows, c)
    are, aim = _fft1(vx, t_fwd, False)
    zre, zim = _fft2(are, aim, kre, kim, fs, fc)
    y = _ifft1(zre, zim, t_inv, vx, x0.reshape(b, ng, rows, c), bias)
    return y.reshape(b, ng, rows // (2 * FFT_G), 2 * FFT_G, c)


def _tail(x, proj, ssd, at, hy, mod, p, final_g, final):
    x = _mix_out(x, ssd[0], ssd[1], proj, at, hy, mod[2], p["ssd_norm_g"][None], p["attn_norm_g"][None],
                 p["hy_norm_g"][None], p["w_out_b"])
    return _ffn(x, mod[3], mod[4], mod[5], p["norm2_g"][None], p["w_gu_b"], p["w_down_b"], final_g[None], final)


def _layer(x, xc, mod_rows, p, tables, final_g, last):
    b = x.shape[0]
    mod_x = [mod_rows[:b, i * D_MODEL:(i + 1) * D_MODEL][:, None, :] for i in range(N_MOD)]
    mod_c = [jnp.broadcast_to(mod_rows[b:b + 1, i * D_MODEL:(i + 1) * D_MODEL][:, None, :], (b, 1, D_MODEL))
             for i in range(N_MOD)]
    g1 = p["norm1_g"][None]
    proj, dt_raw = _proj_in(x, mod_x[0], mod_x[1], g1, p["w_in_r"])
    lc = xc.shape[1]
    rows = lambda a: a.reshape(1, b * lc, a.shape[-1])
    mod_c1 = [m[:1] for m in mod_c]
    projc, dt_raw_c = _proj_in(rows(xc), mod_c1[0], mod_c1[1], g1, p["w_in_r"])
    projc, dt_raw_c = projc.reshape(b, lc, -1), dt_raw_c.reshape(b, lc, -1)
    zeros = jnp.zeros((b, SSD_STATE, SSD_INNER), F32)
    ssd_c, qkv_c = _mixers(projc, dt_raw_c, p, tables, True, (zeros, zeros))
    ssd_x, qkv_x = _mixers(proj, dt_raw, p, tables, False, (ssd_c[2], ssd_c[3]))
    k_all = jnp.concatenate([qkv_c[1], qkv_x[1]], axis=2)
    v_all = jnp.concatenate([qkv_c[2], qkv_x[2]], axis=3)
    at = _flash(qkv_x[0], k_all, v_all)
    hy = _hyena(proj, p, tables, False)
    x = _tail(x, proj, ssd_x, at, hy, mod_x, p, final_g, last)
    if last:
        return x, None
    at_c = _flash(qkv_c[0], qkv_c[1], qkv_c[2])
    hy_c = _hyena(projc, p, tables, True)
    xc = _tail(rows(xc), rows(projc), (rows(ssd_c[0]), rows(ssd_c[1])), rows(at_c), rows(hy_c), mod_c1, p,
               final_g, False)
    return x, xc.reshape(b, lc, -1)


def kernel(x, c, ctx, c_ctx, w_mod, b_mod, norm1_g, w_in, ssd_conv_w, ssd_conv_b, ssd_a_log, ssd_dt_bias, ssd_d, ssd_norm_g, q_norm_g, k_norm_g, attn_norm_g, hy_conv_w, hy_conv_b, hy_w1, hy_b1, hy_freq, hy_w2, hy_b2, hy_w3, hy_bias, hy_norm_g, w_out, norm2_g, w_gu, w_down, final_g):
    b, seq, _ = x.shape
    ctx_len = ctx.shape[1]
    depth = w_mod.shape[0]
    assert 2 * seq == FFT_R * FFT_R and b + 1 <= SUBLANE
    raw = dict(w_in=w_in, ssd_conv_w=ssd_conv_w, ssd_conv_b=ssd_conv_b, ssd_a_log=ssd_a_log,
               ssd_dt_bias=ssd_dt_bias, ssd_d=ssd_d, ssd_norm_g=ssd_norm_g, q_norm_g=q_norm_g,
               k_norm_g=k_norm_g, attn_norm_g=attn_norm_g, hy_conv_w=hy_conv_w, hy_conv_b=hy_conv_b,
               hy_w1=hy_w1, hy_b1=hy_b1, hy_freq=hy_freq, hy_w2=hy_w2, hy_b2=hy_b2, hy_w3=hy_w3,
               hy_bias=hy_bias, hy_norm_g=hy_norm_g, w_out=w_out, norm2_g=norm2_g, w_gu=w_gu, w_down=w_down,
               norm1_g=norm1_g)
    ones = jnp.ones((ctx_len, LANE), F32)
    tables = dict(rope=_rope_tables(seq), rope_ctx=(ones, jnp.zeros_like(ones)),
                  feats=_hy_feats(seq), feats_ctx=_hy_feats(ctx_len), abs_deltas=_hy_abs_deltas(),
                  fft=_fft_tables(), dft_ctx=_dense_dft_tables(ctx_len),
                  head_expand=_head_expand(), group_mean=_group_mean())
    c_rows = jnp.concatenate([c, c_ctx[None], jnp.zeros((SUBLANE - b - 1, D_MODEL), F32)], axis=0)
    xc = ctx
    for l in range(depth):
        p = _layer_params(l, raw, tables)
        mod_rows = _mod_call(c_rows, w_mod[l], b_mod[l][None])
        x, xc = _layer(x, xc, mod_rows, p, tables, final_g, l == depth - 1)
    return x
```

```python
import functools
import math

import numpy as np
import jax
import jax.numpy as jnp
from jax import lax
from jax.experimental import pallas as pl
from jax.experimental.pallas import tpu as pltpu

F32 = jnp.float32
BF16 = jnp.bfloat16
HI = lax.Precision.HIGHEST

D_MODEL = 1024
GRID_W = 64
EPS = 1e-6
SSD_HEADS = 8
SSD_HEAD_DIM = 64
SSD_INNER = 512
SSD_STATE = 128
SSD_CHUNK = 128
SSD_GN = 256
SSD_XB = SSD_INNER + SSD_GN
ATTN_HEADS = 8
ATTN_KV_HEADS = 2
ATTN_HEAD_DIM = 64
ATTN_INNER = 512
ATTN_KV_INNER = 128
ATTN_SCALE = ATTN_HEAD_DIM ** -0.5
LOG2E = math.log2(math.e)
ATTN_V_ROWS = 80
ROPE_THETA = 10000.0
ROPE_AXIS_DIM = ATTN_HEAD_DIM // 2
HY_WIDTH = 512
HY_BANDS = 16
HY_EMB = 1 + 2 * HY_BANDS
HY_HIDDEN = 64
HY_FAST_DECAY_PCT = 0.3
HY_SLOW_DECAY_PCT = 1.5
HY_TARGET = 1e-2
D_MIX = 1536
D_FF = 2816
N_MOD = 6

OFF_K = 0
OFF_V = OFF_K + ATTN_KV_INNER
OFF_XB = OFF_V + ATTN_KV_INNER
OFF_DT = OFF_XB + SSD_XB
OFF_C = OFF_DT + 2 * SSD_HEADS
OFF_Q = OFF_C + SSD_GN
OFF_Z = OFF_Q + ATTN_INNER
OFF_HY = OFF_Z + SSD_INNER

P_HY = 0
P_Q = 1536
P_X = 2048
P_B = 2560
P_C = 2816
P_Z = 3072
P_K = 3584
P_V = 3712
P_DT = 3840
NP = 4096

LANE = 128
SUBLANE = 8
HALO = 16
VMEM_LIMIT = 48 * 1024 * 1024

FFT_R = 128
FFT_G = 8
FFT_K1 = 72


def _cp(*sem):
    return pltpu.CompilerParams(dimension_semantics=sem, vmem_limit_bytes=VMEM_LIMIT)


def _silu(x):
    return x * (1.0 / (1.0 + jnp.exp(-x)))


def _softplus(x):
    return jnp.maximum(x, 0.0) + jnp.log(1.0 + jnp.exp(-jnp.abs(x)))


def _rms(x, g):
    return x * lax.rsqrt(jnp.mean(x * x, axis=-1, keepdims=True) + EPS) * g


def _dot(a, b):
    return jnp.dot(a, b, preferred_element_type=F32)


def _dot_hi(a, b):
    return jnp.dot(a, b, precision=HI, preferred_element_type=F32)


def _split_bf16(x):
    hi = x.astype(BF16)
    return hi, (x - hi.astype(F32)).astype(BF16)


def _dot3_table(t, x):
    t_hi, t_lo = _split_bf16(t)
    x_hi, x_lo = _split_bf16(x)
    return _dot(t_hi, x_hi) + _dot(t_lo, x_hi) + _dot(t_hi, x_lo)


def _dot1_table(t, x):
    return _dot(t.astype(BF16), x.astype(BF16))


def _mod_kernel(c_ref, w_ref, b_ref, o_ref):
    o_ref[...] = _dot_hi(_silu(c_ref[...]), w_ref[...]) + b_ref[...]


def _mod_call(c_rows, w, b):
    n = w.shape[1]
    tn = 1024
    return pl.pallas_call(
        _mod_kernel,
        grid=(n // tn,),
        in_specs=[pl.BlockSpec((SUBLANE, D_MODEL), lambda j: (0, 0)),
                  pl.BlockSpec((D_MODEL, tn), lambda j: (0, j)),
                  pl.BlockSpec((1, tn), lambda j: (0, j))],
        out_specs=pl.BlockSpec((SUBLANE, tn), lambda j: (0, j)),
        out_shape=jax.ShapeDtypeStruct((SUBLANE, n), F32),
        compiler_params=_cp("parallel"),
        name="adaln_mod",
    )(c_rows, w, b)


def _proj_kernel(x_ref, sh_ref, sc_ref, g_ref, w_ref, o_ref, dt_ref, h_ref, *, dt_tile, dt_off):
    j = pl.program_id(2)

    @pl.when(j == 0)
    def _():
        y = _rms(x_ref[0], g_ref[...])
        h_ref[...] = (y * (1.0 + sc_ref[0]) + sh_ref[0]).astype(BF16)

    r = _dot(h_ref[...], w_ref[...])
    o_ref[0] = r.astype(BF16)

    @pl.when(j == dt_tile)
    def _():
        dt_ref[0] = r[:, dt_off:dt_off + LANE]


def _proj_in(x, sh, sc, g, w):
    b, l, _ = x.shape
    tm = min(l, 1024)
    tn = 1024
    return pl.pallas_call(
        functools.partial(_proj_kernel, dt_tile=P_DT // tn, dt_off=P_DT % tn),
        grid=(b, l // tm, NP // tn),
        in_specs=[pl.BlockSpec((1, tm, D_MODEL), lambda bi, i, j: (bi, i, 0)),
                  pl.BlockSpec((1, 1, D_MODEL), lambda bi, i, j: (bi, 0, 0)),
                  pl.BlockSpec((1, 1, D_MODEL), lambda bi, i, j: (bi, 0, 0)),
                  pl.BlockSpec((1, D_MODEL), lambda bi, i, j: (0, 0)),
                  pl.BlockSpec((D_MODEL, tn), lambda bi, i, j: (0, j))],
        out_specs=[pl.BlockSpec((1, tm, tn), lambda bi, i, j: (bi, i, j)),
                   pl.BlockSpec((1, tm, LANE), lambda bi, i, j: (bi, i, 0))],
        out_shape=[jax.ShapeDtypeStruct((b, l, NP), BF16), jax.ShapeDtypeStruct((b, l, LANE), F32)],
        scratch_shapes=[pltpu.VMEM((tm, D_MODEL), BF16)],
        compiler_params=_cp("parallel", "parallel", "arbitrary"),
        name="proj_in",
    )(x, sh, sc, g, w)


def _dwconv3(u, prev_row, next_row, w, b):
    tm = u.shape[0]
    ri = lax.broadcasted_iota(jnp.int32, u.shape, 0)
    um = jnp.where(ri == 0, prev_row, pltpu.roll(u, 1, 0))
    up = jnp.where(ri == tm - 1, next_row, pltpu.roll(u, tm - 1, 0))
    return um * w[0:1] + u * w[1:2] + up * w[2:3] + b


def _conv_group(refs, i, n_i):
    u_ref, p_ref, n_ref, w_ref, b_ref = refs
    prev_row = jnp.where(i > 0, p_ref[0].astype(F32)[HALO - 1:HALO, :], 0.0)
    next_row = jnp.where(i < n_i - 1, n_ref[0].astype(F32)[0:1, :], 0.0)
    return _dwconv3(u_ref[0].astype(F32), prev_row, next_row, w_ref[...], b_ref[...])


def _hy_conv_kernel(*refs, group_major):
    i, n_i = pl.program_id(1), pl.num_programs(1)
    v = _conv_group(refs[0:5], i, n_i)
    x1 = _conv_group(refs[5:10], i, n_i)
    x0 = _conv_group(refs[10:15], i, n_i)
    vx_ref, x0_ref = refs[15], refs[16]
    vx = v * x1
    if group_major:
        for ref, u in ((vx_ref, vx), (x0_ref, x0)):
            g = _to_group_major(u)
            ref[0] = g.reshape(g.shape[0], g.shape[1] // 2, 2 * g.shape[2], g.shape[3]).astype(ref.dtype)
    else:
        vx_ref[0] = vx
        x0_ref[0] = x0


def _to_group_major(u):
    rows, c = u.shape
    return jnp.swapaxes(u.reshape(rows // FFT_R, FFT_R // FFT_G, FFT_G, c), 0, 1)


def _from_group_major(u):
    g, n1, r, c = u.shape
    return jnp.swapaxes(u, 0, 1).reshape(n1 * g * r, c)


def _conv_specs(tm, tc, l, col_block, w_block):
    nrb = l // HALO
    per = tm // HALO
    return [
        pl.BlockSpec((1, tm, tc), lambda bi, i: (bi, i, col_block)),
        pl.BlockSpec((1, HALO, tc), lambda bi, i: (bi, jnp.maximum(i * per - 1, 0), col_block)),
        pl.BlockSpec((1, HALO, tc), lambda bi, i: (bi, jnp.minimum((i + 1) * per, nrb - 1), col_block)),
        pl.BlockSpec((SUBLANE, tc), lambda bi, i: (0, w_block)),
        pl.BlockSpec((1, tc), lambda bi, i: (0, w_block)),
    ]


def _pad_taps(w):
    return jnp.pad(w, ((0, SUBLANE - w.shape[0]), (0, 0)))


def _hy_conv(proj, conv_w, conv_b, group_major):
    b, l, _ = proj.shape
    tm = min(l, 1024)
    tc = HY_WIDTH
    wp = _pad_taps(conv_w)
    bp = conv_b[None]
    specs, args = [], []
    for grp in range(3):
        specs += _conv_specs(tm, tc, l, P_HY // tc + grp, grp)
        args += [proj, proj, proj, wp, bp]
    if group_major:
        ng = FFT_R // FFT_G
        out_spec = pl.BlockSpec((1, ng, tm // FFT_R // 2, 2 * FFT_G, tc), lambda bi, i: (bi, 0, i, 0, 0))
        shp = jax.ShapeDtypeStruct((b, ng, l // FFT_R // 2, 2 * FFT_G, tc), BF16)
    else:
        out_spec = pl.BlockSpec((1, tm, tc), lambda bi, i: (bi, i, 0))
        shp = jax.ShapeDtypeStruct((b, l, tc), F32)
    return pl.pallas_call(
        functools.partial(_hy_conv_kernel, group_major=group_major),
        grid=(b, l // tm),
        in_specs=specs,
        out_specs=[out_spec, out_spec],
        out_shape=[shp, shp],
        compiler_params=_cp("parallel", "parallel"),
        name="hy_conv",
    )(*args)


def _split_pieces(x, n):
    pieces, r = [], x
    for k in range(n):
        pc = r.astype(BF16)
        pieces.append(pc)
        if k + 1 < n:
            r = r - pc.astype(F32)
    return pieces


def _select_left(sel, x, n):
    w = x.shape[1]
    r = _dot(sel, jnp.concatenate(_split_pieces(x, n), axis=1))
    return sum(r[:, k * w:(k + 1) * w] for k in range(n))


def _select_right(x, sel, n):
    m = x.shape[0]
    r = _dot(jnp.concatenate(_split_pieces(x, n), axis=0), sel)
    return sum(r[k * m:(k + 1) * m] for k in range(n))


def _ssd_chunk(xbc, dt_raw, a, bias, d_row, e_d, st, lane0, fwd):
    q = SSD_CHUNK
    xs = xbc[:, :SSD_INNER]
    bm = xbc[:, SSD_INNER:SSD_XB]
    cm = xbc[:, SSD_XB:]
    ri = lax.broadcasted_iota(jnp.int32, (q, q), 0)
    ci = lax.broadcasted_iota(jnp.int32, (q, q), 1)
    dt = _softplus(dt_raw + bias)
    adt = dt * a
    cs = _select_left((ci <= ri).astype(BF16), adt, 3)
    tot = cs[q - 1:q, :]
    if fwd:
        key = cs
        w_c = dt * jnp.exp(tot - key)
        e_c = jnp.exp(key)
        mask = ci <= ri
    else:
        key = cs - adt
        w_c = dt * jnp.exp(key)
        e_c = jnp.exp(tot - key)
        mask = ci >= ri
    key_t = key.T
    dt_t = dt.T
    dec_c = jnp.broadcast_to(jnp.exp(tot), (2 * SUBLANE, LANE))
    spread = _select_right(jnp.concatenate([w_c, e_c, dec_c], axis=0), e_d, 2)
    w_e, e_off, decay = spread[:q], spread[q:2 * q], spread[2 * q:2 * q + 1]
    x_b = xs.astype(BF16)
    x_w = (xs * w_e).astype(BF16)
    st_b = st.astype(BF16)
    y_parts, st_parts, off_parts = [], [], []
    hg = SSD_HEADS // 2
    for g in range(2):
        bg = bm[:, g * SSD_STATE:(g + 1) * SSD_STATE]
        cg = cm[:, g * SSD_STATE:(g + 1) * SSD_STATE].astype(BF16)
        bg_t = bg.T.astype(BF16)
        gmat = _dot(cg, bg_t)
        gs = slice(g * hg * SSD_HEAD_DIM, (g + 1) * hg * SSD_HEAD_DIM)
        off_parts.append(_dot(cg, st_b[:, gs]))
        st_parts.append(_dot(bg_t, x_w[:, gs]))
        for hh in range(hg):
            h = g * hg + hh
            col = key[:, lane0 + h:lane0 + h + 1]
            row = key_t[lane0 + h:lane0 + h + 1, :]
            diff = (col - row) if fwd else (row - col)
            lm = jnp.exp(jnp.where(mask, diff, -1e30)) * dt_t[lane0 + h:lane0 + h + 1, :]
            s = (gmat * lm).astype(BF16)
            y_parts.append(_dot(s, x_b[:, h * SSD_HEAD_DIM:(h + 1) * SSD_HEAD_DIM]))
    y = jnp.concatenate(y_parts, axis=1) + jnp.concatenate(off_parts, axis=1) * e_off
    if fwd:
        y = y + d_row * xs
    st_new = st * decay + jnp.concatenate(st_parts, axis=1)
    return y, st_new


def _ssd_kernel(xf_ref, xfp_ref, xfn_ref, xb_ref, xbp_ref, xbn_ref, dtf_ref, dtb_ref, cw_ref, cb_ref,
                bias_ref, alog_ref, d_ref, e_ref, sf0_ref, sb0_ref, yf_ref, yb_ref, sf_ref, sb_ref, stf, stb):
    c = pl.program_id(0)
    nc = pl.num_programs(0)

    @pl.when(c == 0)
    def _():
        stf[...] = sf0_ref[...]
        stb[...] = sb0_ref[...]

    def conv_silu(u_ref, p_ref, n_ref, bi, chunk):
        prev_row = jnp.where(chunk > 0, p_ref[bi].astype(F32)[HALO - 1:HALO, :], 0.0)
        next_row = jnp.where(chunk < nc - 1, n_ref[bi].astype(F32)[0:1, :], 0.0)
        return _silu(_dwconv3(u_ref[bi].astype(F32), prev_row, next_row, cw_ref[...], cb_ref[...]))

    a = -jnp.exp(alog_ref[...])
    bias = bias_ref[...]
    for bi in range(xf_ref.shape[0]):
        xf = conv_silu(xf_ref, xfp_ref, xfn_ref, bi, c)
        xb = conv_silu(xb_ref, xbp_ref, xbn_ref, bi, nc - 1 - c)
        yf, sf = _ssd_chunk(xf, dtf_ref[bi], a, bias, d_ref[...], e_ref[:, :SSD_INNER], stf[bi], 0, True)
        yb, sb = _ssd_chunk(xb, dtb_ref[bi], a, bias, d_ref[...], e_ref[:, SSD_INNER:], stb[bi], SSD_HEADS, False)
        yf_ref[bi] = yf.astype(yf_ref.dtype)
        yb_ref[bi] = yb.astype(yb_ref.dtype)
        stf[bi] = sf
        stb[bi] = sb
        sf_ref[bi] = sf
        sb_ref[bi] = sb


def _ssd_scan(proj, dt_raw, sf0, sb0, p):
    b, l, _ = proj.shape
    nc = l // SSD_CHUNK
    q = SSD_CHUNK
    w = 2 * SSD_INNER
    xcol = P_X // w
    per = q // HALO
    nrb = l // HALO
    st_spec = pl.BlockSpec((b, SSD_STATE, SSD_INNER), lambda c: (0, 0, 0))
    y_shape = jax.ShapeDtypeStruct((b, l, SSD_INNER), BF16)
    st_shape = jax.ShapeDtypeStruct((b, SSD_STATE, SSD_INNER), F32)
    row = lambda n: pl.BlockSpec((1, n), lambda c: (0, 0))

    def chunk_specs(chunk_of):
        return [pl.BlockSpec((b, q, w), lambda c: (0, chunk_of(c), xcol)),
                pl.BlockSpec((b, HALO, w), lambda c: (0, jnp.maximum(chunk_of(c) * per - 1, 0), xcol)),
                pl.BlockSpec((b, HALO, w), lambda c: (0, jnp.minimum((chunk_of(c) + 1) * per, nrb - 1), xcol))]

    fwd_of = lambda c: c
    bwd_of = lambda c: nc - 1 - c
    return pl.pallas_call(
        _ssd_kernel,
        grid=(nc,),
        in_specs=chunk_specs(fwd_of) + chunk_specs(bwd_of) + [
            pl.BlockSpec((b, q, LANE), lambda c: (0, c, 0)),
            pl.BlockSpec((b, q, LANE), lambda c: (0, nc - 1 - c, 0)),
            pl.BlockSpec((SUBLANE, w), lambda c: (0, 0)), row(w),
            row(LANE), row(LANE), row(SSD_INNER),
            pl.BlockSpec((LANE, 2 * SSD_INNER), lambda c: (0, 0)),
            st_spec, st_spec],
        out_specs=[pl.BlockSpec((b, q, SSD_INNER), lambda c: (0, c, 0)),
                   pl.BlockSpec((b, q, SSD_INNER), lambda c: (0, nc - 1 - c, 0)),
                   st_spec, st_spec],
        out_shape=[y_shape, y_shape, st_shape, st_shape],
        scratch_shapes=[pltpu.VMEM((b, SSD_STATE, SSD_INNER), F32), pltpu.VMEM((b, SSD_STATE, SSD_INNER), F32)],
        compiler_params=_cp("arbitrary"),
        name="ssd_scan",
    )(proj, proj, proj, proj, proj, proj, dt_raw, dt_raw, _pad_taps(p["ssd_conv_w"]), p["ssd_conv_b"][None],
      p["dt_bias_row"], p["a_log_row"], p["d_row"], p["head_expand"], sf0, sb0)


def _attn_prep_kernel(q_ref, k_ref, v_ref, cos_ref, sin_ref, gq_ref, gk_ref, gm_ref, qo_ref, ko_ref, vo_ref):
    cos = cos_ref[...]
    sin = sin_ref[...]
    gm = gm_ref[...]
    lane = lax.broadcasted_iota(jnp.int32, cos.shape, 1)
    first = jnp.bitwise_and(lane, 31) < 16
    hd = ATTN_HEAD_DIM

    def norm_rope(t, g):
        ms = _select_right(t * t, gm, 2)
        y = t * lax.rsqrt(ms + EPS) * g
        partner = jnp.where(first, pltpu.roll(y, LANE - 16, 1), pltpu.roll(y, 16, 1))
        return y * cos + partner * sin

    for s in range(ATTN_INNER // LANE):
        qs = norm_rope(q_ref[0, :, s * LANE:(s + 1) * LANE].astype(F32), gq_ref[...]) * (ATTN_SCALE * LOG2E)
        qt = qs.T.astype(BF16)
        qo_ref[0, 2 * s] = qt[:hd]
        qo_ref[0, 2 * s + 1] = qt[hd:]
    ks = norm_rope(k_ref[0].astype(F32), gk_ref[...])
    ko_ref[0, 0] = ks[:, :hd].astype(BF16)
    ko_ref[0, 1] = ks[:, hd:].astype(BF16)
    vt = v_ref[0].astype(F32).T.astype(BF16)
    tm = vt.shape[1]
    pad_rows = lax.broadcasted_iota(jnp.int32, (ATTN_V_ROWS - hd, tm), 0)
    tail = jnp.where(pad_rows == 0, 1.0, 0.0).astype(BF16)
    for g in range(ATTN_KV_HEADS):
        vo_ref[0, g, 0:hd, :] = vt[g * hd:(g + 1) * hd]
        vo_ref[0, g, hd:ATTN_V_ROWS, :] = tail


def _attn_prep(proj, cos, sin, gq, gk, gm):
    b, l, _ = proj.shape
    tm = min(l, 1024)
    hd = ATTN_HEAD_DIM
    const = lambda r, c: pl.BlockSpec((r, c), lambda bi, i: (0, 0))
    return pl.pallas_call(
        _attn_prep_kernel,
        grid=(b, l // tm),
        in_specs=[pl.BlockSpec((1, tm, ATTN_INNER), lambda bi, i: (bi, i, P_Q // ATTN_INNER)),
                  pl.BlockSpec((1, tm, LANE), lambda bi, i: (bi, i, P_K // LANE)),
                  pl.BlockSpec((1, tm, LANE), lambda bi, i: (bi, i, P_V // LANE)),
                  pl.BlockSpec((tm, LANE), lambda bi, i: (i, 0)),
                  pl.BlockSpec((tm, LANE), lambda bi, i: (i, 0)),
                  const(1, LANE), const(1, LANE), const(LANE, LANE)],
        out_specs=[pl.BlockSpec((1, ATTN_HEADS, hd, tm), lambda bi, i: (bi, 0, 0, i)),
                   pl.BlockSpec((1, ATTN_KV_HEADS, tm, hd), lambda bi, i: (bi, 0, i, 0)),
                   pl.BlockSpec((1, ATTN_KV_HEADS, ATTN_V_ROWS, tm), lambda bi, i: (bi, 0, 0, i))],
        out_shape=[jax.ShapeDtypeStruct((b, ATTN_HEADS, hd, l), BF16),
                   jax.ShapeDtypeStruct((b, ATTN_KV_HEADS, l, hd), BF16),
                   jax.ShapeDtypeStruct((b, ATTN_KV_HEADS, ATTN_V_ROWS, l), BF16)],
        compiler_params=_cp("parallel", "parallel"),
        name="attn_prep",
    )(proj, proj, proj, cos, sin, gq, gk, gm)


def _flash_kernel(q_ref, k_ref, v_ref, o_ref, s_ref, m_ref, acc_ref, *, tk, nk):
    r = ATTN_HEADS // ATTN_KV_HEADS
    hd = ATTN_HEAD_DIM

    def scores(j, h, slot):
        start = pl.multiple_of(j * tk, tk)
        s_ref[slot] = _dot(k_ref[0, 0, pl.ds(start, tk), :], q_ref[0, h])

    m_ref[...] = jnp.full(m_ref.shape, -1e30, F32)
    acc_ref[...] = jnp.zeros(acc_ref.shape, F32)
    scores(0, 0, 0)

    def body(j, _):
        start = pl.multiple_of(j * tk, tk)
        vs = v_ref[0, 0, :, pl.ds(start, tk)]
        j_next = jnp.minimum(j + 1, nk - 1)
        for h in range(r):
            if h < r - 1:
                scores(j, h + 1, (h + 1) % 2)
            else:
                scores(j_next, 0, 0)
            s = s_ref[h % 2]
            m = m_ref[h]
            mn = jnp.maximum(m, jnp.max(s, axis=0, keepdims=True))
            alpha = jnp.exp2(m - mn)
            p = jnp.exp2(s - mn).astype(BF16)
            m_ref[h] = mn
            acc_ref[h] = alpha * acc_ref[h] + _dot(vs, p)
        return 0

    lax.fori_loop(0, nk, body, 0)
    outs = []
    for h in range(r):
        acc = acc_ref[h]
        outs.append((acc[:hd] * (1.0 / acc[hd:hd + 1])).T)
    o_ref[0] = jnp.concatenate(outs, axis=1).astype(o_ref.dtype)


def _flash(q, k, v):
    b, _, hd, l = q.shape
    lk = k.shape[2]
    r = ATTN_HEADS // ATTN_KV_HEADS
    assert r % 2 == 0
    tq = min(l, 1024)
    tk = 384 if lk % 384 == 0 else 256
    return pl.pallas_call(
        functools.partial(_flash_kernel, tk=tk, nk=lk // tk),
        grid=(b, ATTN_KV_HEADS, l // tq),
        in_specs=[pl.BlockSpec((1, r, hd, tq), lambda bi, g, i: (bi, g, 0, i)),
                  pl.BlockSpec((1, 1, lk, hd), lambda bi, g, i: (bi, g, 0, 0)),
                  pl.BlockSpec((1, 1, ATTN_V_ROWS, lk), lambda bi, g, i: (bi, g, 0, 0))],
        out_specs=pl.BlockSpec((1, tq, r * hd), lambda bi, g, i: (bi, i, g)),
        out_shape=jax.ShapeDtypeStruct((b, l, ATTN_INNER), BF16),
        scratch_shapes=[pltpu.VMEM((2, tk, tq), F32), pltpu.VMEM((r, 1, tq), F32),
                        pltpu.VMEM((r, ATTN_V_ROWS, tq), F32)],
        compiler_params=_cp("parallel", "parallel", "parallel"),
        name="flash_gqa",
    )(q, k, v)


def _taps_kernel(ft_ref, t_ref, w1_ref, b1_ref, fr_ref, w2_ref, b2_ref, w3_ref, adel_ref, o_ref, *, group_major):
    fr = fr_ref[...]
    h = jnp.sin(fr * (_dot_hi(w1_ref[...], ft_ref[...]) + b1_ref[...]))
    h = jnp.sin(fr * (_dot_hi(w2_ref[...], h) + b2_ref[...]))
    y = _dot_hi(h.T, w3_ref[...])
    decay = jnp.exp(-t_ref[...] * adel_ref[...])
    f0 = y[:, :HY_WIDTH] * decay
    f1 = y[:, HY_WIDTH:] * decay
    lag = pl.program_id(0) * f1.shape[0] + lax.broadcasted_iota(jnp.int32, f1.shape, 0)
    f1 = jnp.where(lag == 0, 0.0, f1)
    o_ref[0] = _to_group_major(f0) if group_major else f0
    o_ref[1] = _to_group_major(f1) if group_major else f1


def _hy_taps(feats_t, t_col, p, adel, seq, group_major):
    tm = min(seq, 1024)
    const = lambda r, c: pl.BlockSpec((r, c), lambda i: (0, 0))
    if group_major:
        ng = FFT_R // FFT_G
        out = pl.BlockSpec((2, ng, tm // FFT_R, FFT_G, HY_WIDTH), lambda i: (0, 0, i, 0, 0))
        shp = jax.ShapeDtypeStruct((2, ng, seq // FFT_R, FFT_G, HY_WIDTH), F32)
    else:
        out = pl.BlockSpec((2, tm, HY_WIDTH), lambda i: (0, i, 0))
        shp = jax.ShapeDtypeStruct((2, seq, HY_WIDTH), F32)
    return pl.pallas_call(
        functools.partial(_taps_kernel, group_major=group_major),
        grid=(seq // tm,),
        in_specs=[pl.BlockSpec((LANE, tm), lambda i: (0, i)),
                  pl.BlockSpec((tm, 1), lambda i: (i, 0)),
                  const(HY_HIDDEN, LANE), const(HY_HIDDEN, 1), const(HY_HIDDEN, 1),
                  const(HY_HIDDEN, HY_HIDDEN), const(HY_HIDDEN, 1),
                  const(HY_HIDDEN, 2 * HY_WIDTH), const(1, HY_WIDTH)],
        out_specs=out,
        out_shape=shp,
        compiler_params=_cp("parallel"),
        name="hy_taps",
    )(feats_t, t_col, p["hy_w1p"].T, p["hy_b1"][:, None], p["hy_freq"][:, None], p["hy_w2"].T,
      p["hy_b2"][:, None], p["hy_w3"], adel)


def _fft1_kernel(x_ref, t_ref, are_ref, aim_ref, *, precise):
    x = x_ref[0]
    a = _dot3_table(t_ref[0], x) if precise else _dot1_table(t_ref[0], x)
    rows = FFT_K1 * FFT_G
    are_ref[0] = a[:rows].astype(are_ref.dtype)
    aim_ref[0] = a[rows:].astype(aim_ref.dtype)


def _fft1(x, table, precise):
    nb, ng, rows, c = x.shape
    spec_o = pl.BlockSpec((1, None, FFT_K1 * FFT_G, c), lambda g, bi: (bi, g, 0, 0))
    shp = jax.ShapeDtypeStruct((nb, ng, FFT_K1 * FFT_G, c), F32 if precise else BF16)
    return pl.pallas_call(
        functools.partial(_fft1_kernel, precise=precise),
        grid=(ng, nb),
        in_specs=[pl.BlockSpec((1, None, rows, c), lambda g, bi: (bi, g, 0, 0)),
                  pl.BlockSpec((1,) + table.shape[1:], lambda g, bi: (g, 0, 0))],
        out_specs=[spec_o, spec_o],
        out_shape=[shp, shp],
        compiler_params=_cp("parallel", "parallel"),
        name="hy_fft1",
    )(x, table)


def _k1_rows(ref, b, kk):
    blk = ref[b, :, kk]
    return blk.reshape(blk.shape[0] * blk.shape[1], blk.shape[2])


def _fft2_filter_kernel(are_ref, aim_ref, fs_ref, kre_ref, kim_ref):
    r_ = FFT_R
    for kk in range(SUBLANE):
        xc = _dot3_table(fs_ref[...], jnp.concatenate([_k1_rows(are_ref, 0, kk), _k1_rows(aim_ref, 0, kk)], axis=0))
        xa = _dot3_table(fs_ref[...], jnp.concatenate([_k1_rows(are_ref, 1, kk), _k1_rows(aim_ref, 1, kk)], axis=0))
        kre_ref[kk] = xc[:r_] + xa[:r_]
        kim_ref[kk] = xc[r_:] - xa[r_:]


def _fft2_filter(are, aim, fs):
    r_ = FFT_R
    nb, ng, _, c = are.shape
    view = lambda a: a.reshape(nb, ng, FFT_K1, FFT_G, c)
    spec_i = pl.BlockSpec((nb, ng, SUBLANE, FFT_G, c), lambda kg: (0, 0, kg, 0, 0))
    spec_o = pl.BlockSpec((SUBLANE, r_, c), lambda kg: (kg, 0, 0))
    shp = jax.ShapeDtypeStruct((FFT_K1, r_, c), F32)
    return pl.pallas_call(
        _fft2_filter_kernel,
        grid=(FFT_K1 // SUBLANE,),
        in_specs=[spec_i, spec_i, pl.BlockSpec((2 * r_, 2 * r_), lambda kg: (0, 0))],
        out_specs=[spec_o, spec_o],
        out_shape=[shp, shp],
        compiler_params=_cp("parallel"),
        name="hy_fft2_filter",
    )(view(are), view(aim), fs)


def _fft2_kernel(are_ref, aim_ref, kre_ref, kim_ref, fs_ref, fc_ref, zre_ref, zim_ref):
    r_ = FFT_R
    nb, ng, c = are_ref.shape[0], are_ref.shape[1], are_ref.shape[4]
    for b in range(nb):
        for kp in range(SUBLANE // 2):
            a_re = are_ref[b, :, kp].astype(F32)
            a_im = aim_ref[b, :, kp].astype(F32)
            z_re, z_im = [], []
            for par in range(2):
                kk = 2 * kp + par
                rows = slice(par * FFT_G, (par + 1) * FFT_G)
                a = jnp.concatenate([a_re[:, rows].reshape(r_, c), a_im[:, rows].reshape(r_, c)], axis=0)
                bb = _dot1_table(fs_ref[...], a)
                br, bi = bb[:r_], bb[r_:]
                kr, ki = kre_ref[kk], kim_ref[kk]
                y = jnp.concatenate([br * kr - bi * ki, br * ki + bi * kr], axis=0)
                z = _dot1_table(fc_ref[...], y)
                z_re.append(z[:r_].reshape(ng, FFT_G, c))
                z_im.append(z[r_:].reshape(ng, FFT_G, c))
            zre_ref[b, :, kp] = jnp.concatenate(z_re, axis=1).astype(zre_ref.dtype)
            zim_ref[b, :, kp] = jnp.concatenate(z_im, axis=1).astype(zim_ref.dtype)


def _fft2(are, aim, kre, kim, fs, fc):
    nb, ng, rows, c = are.shape
    r_ = FFT_R
    view = lambda a: a.reshape(nb, ng, FFT_K1 // 2, 2 * FFT_G, c)
    spec_a = pl.BlockSpec((nb, ng, SUBLANE // 2, 2 * FFT_G, c), lambda kg: (0, 0, kg, 0, 0))
    spec_k = pl.BlockSpec((SUBLANE, r_, c), lambda kg: (kg, 0, 0))
    spec_f = pl.BlockSpec((2 * r_, 2 * r_), lambda kg: (0, 0))
    shp = jax.ShapeDtypeStruct((nb, ng, FFT_K1 // 2, 2 * FFT_G, c), BF16)
    zre, zim = pl.pallas_call(
        _fft2_kernel,
        grid=(FFT_K1 // SUBLANE,),
        in_specs=[spec_a, spec_a, spec_k, spec_k, spec_f, spec_f],
        out_specs=[spec_a, spec_a],
        out_shape=[shp, shp],
        compiler_params=_cp("parallel"),
        name="hy_fft2",
    )(view(are), view(aim), kre, kim, fs, fc)
    return zre.reshape(are.shape), zim.reshape(are.shape)


def _ifft1_kernel(zre_ref, zim_ref, t_ref, vx_ref, x0_ref, bias_ref, o_ref):
    z = jnp.concatenate([zre_ref[0], zim_ref[0]], axis=0)
    conv = _dot1_table(t_ref[0], z)
    o_ref[0] = (x0_ref[0].astype(F32) * (conv + bias_ref[...] * vx_ref[0].astype(F32))).astype(o_ref.dtype)


def _ifft1(zre, zim, table, vx, x0, bias):
    nb, ng, rows, c = vx.shape
    spec_z = pl.BlockSpec((1, None, FFT_K1 * FFT_G, c), lambda g, bi: (bi, g, 0, 0))
    spec_x = pl.BlockSpec((1, None, rows, c), lambda g, bi: (bi, g, 0, 0))
    return pl.pallas_call(
        _ifft1_kernel,
        grid=(ng, nb),
        in_specs=[spec_z, spec_z,
                  pl.BlockSpec((1,) + table.shape[1:], lambda g, bi: (g, 0, 0)),
                  spec_x, spec_x,
                  pl.BlockSpec((1, c), lambda g, bi: (0, 0))],
        out_specs=spec_x,
        out_shape=jax.ShapeDtypeStruct(vx.shape, BF16),
        compiler_params=_cp("parallel", "parallel"),
        name="hy_ifft1",
    )(zre, zim, table, vx, x0, bias)


def _hy_ctx_kernel(vx_ref, x0_ref, taps_ref, ff_ref, fi_ref, bias_ref, o_ref):
    vx = vx_ref[0]
    seq = vx.shape[0]
    n = 2 * seq
    u = _dot_hi(ff_ref[:, :seq], vx)
    k = _dot_hi(ff_ref[...], taps_ref[...])
    ur, ui = u[:n], u[n:]
    kr, ki = k[:n], k[n:]
    y = jnp.concatenate([ur * kr - ui * ki, ur * ki + ui * kr], axis=0)
    conv = _dot_hi(fi_ref[...], y)
    o_ref[0] = x0_ref[0] * (conv + bias_ref[...] * vx)


def _hy_ctx(vx, x0, taps, ff, fi, bias):
    b, seq, c = vx.shape
    n = 2 * seq
    spec_x = pl.BlockSpec((1, seq, c), lambda bi: (bi, 0, 0))
    const = lambda r, cc: pl.BlockSpec((r, cc), lambda bi: (0, 0))
    return pl.pallas_call(
        _hy_ctx_kernel,
        grid=(b,),
        in_specs=[spec_x, spec_x, const(n, c), const(2 * n, n), const(seq, 2 * n), const(1, c)],
        out_specs=spec_x,
        out_shape=jax.ShapeDtypeStruct((b, seq, c), F32),
        compiler_params=_cp("parallel"),
        name="hy_ctx",
    )(vx, x0, taps, ff, fi, bias)


def _mix_kernel(x_ref, yf_ref, yb_ref, z_ref, at_ref, hy_ref, g1_ref, gs_ref, ga_ref, gh_ref, w_ref, o_ref):
    ys = _rms((yf_ref[0].astype(F32) + yb_ref[0].astype(F32)) * _silu(z_ref[0].astype(F32)), gs_ref[...]).astype(BF16)
    ya = _rms(at_ref[0].astype(F32), ga_ref[...]).astype(BF16)
    hy = hy_ref[0].astype(F32)
    if hy.ndim == 4:
        hy = _from_group_major(hy.reshape(hy.shape[0], -1, FFT_G, hy.shape[3]))
    yh = _rms(hy, gh_ref[...]).astype(BF16)
    r = (_dot(ys, w_ref[0:SSD_INNER, :])
         + _dot(ya, w_ref[SSD_INNER:SSD_INNER + ATTN_INNER, :])
         + _dot(yh, w_ref[SSD_INNER + ATTN_INNER:, :]))
    o_ref[0] = x_ref[0] + g1_ref[0] * r


def _mix_out(x, yf, yb, proj, at, hy, g1, gs, ga, gh, w):
    b, l, _ = x.shape
    tm = min(l, 512)
    c = SSD_INNER
    t512 = lambda col: pl.BlockSpec((1, tm, c), lambda bi, i: (bi, i, col))
    const = lambda r, cc: pl.BlockSpec((r, cc), lambda bi, i: (0, 0))
    xs = pl.BlockSpec((1, tm, D_MODEL), lambda bi, i: (bi, i, 0))
    if hy.ndim == 5:
        hy_spec = pl.BlockSpec((1, hy.shape[1], tm // FFT_R // 2, hy.shape[3], c), lambda bi, i: (bi, 0, i, 0, 0))
    else:
        hy_spec = t512(0)
    return pl.pallas_call(
        _mix_kernel,
        grid=(b, l // tm),
        in_specs=[xs, t512(0), t512(0), t512(P_Z // c), t512(0), hy_spec,
                  pl.BlockSpec((1, 1, D_MODEL), lambda bi, i: (bi, 0, 0)),
                  const(1, c), const(1, c), const(1, c), const(D_MIX, D_MODEL)],
        out_specs=xs,
        out_shape=jax.ShapeDtypeStruct(x.shape, F32),
        compiler_params=_cp("parallel", "parallel"),
        name="mix_out",
    )(x, yf, yb, proj, at, hy, g1, gs, ga, gh, w)


def _ffn_kernel(x_ref, sh_ref, sc_ref, g2_ref, ng_ref, wg_ref, wu_ref, wd_ref, fg_ref, o_ref, h_ref, acc_ref,
                *, final):
    j = pl.program_id(2)

    @pl.when(j == 0)
    def _():
        y = _rms(x_ref[0], ng_ref[...])
        h_ref[...] = (y * (1.0 + sc_ref[0]) + sh_ref[0]).astype(BF16)
        acc_ref[...] = jnp.zeros_like(acc_ref)

    h = h_ref[...]
    act = (_silu(_dot(h, wg_ref[...])) * _dot(h, wu_ref[...])).astype(BF16)
    acc_ref[...] += _dot(act, wd_ref[...])

    @pl.when(j == pl.num_programs(2) - 1)
    def _():
        y = x_ref[0] + g2_ref[0] * acc_ref[...]
        if final:
            y = _rms(y, fg_ref[...])
        o_ref[0] = y


def _ffn(x, sh, sc, g2, ng, w_gu, w_down, fg, final):
    b, l, _ = x.shape
    tm = min(l, 1024)
    tf = 256
    nf = D_FF // tf
    xs = pl.BlockSpec((1, tm, D_MODEL), lambda bi, i, j: (bi, i, 0))
    mod = pl.BlockSpec((1, 1, D_MODEL), lambda bi, i, j: (bi, 0, 0))
    row = pl.BlockSpec((1, D_MODEL), lambda bi, i, j: (0, 0))
    return pl.pallas_call(
        functools.partial(_ffn_kernel, final=final),
        grid=(b, l // tm, nf),
        in_specs=[xs, mod, mod, mod, row,
                  pl.BlockSpec((D_MODEL, tf), lambda bi, i, j: (0, j)),
                  pl.BlockSpec((D_MODEL, tf), lambda bi, i, j: (0, nf + j)),
                  pl.BlockSpec((tf, D_MODEL), lambda bi, i, j: (j, 0)),
                  row],
        out_specs=xs,
        out_shape=jax.ShapeDtypeStruct(x.shape, F32),
        scratch_shapes=[pltpu.VMEM((tm, D_MODEL), BF16), pltpu.VMEM((tm, D_MODEL), F32)],
        compiler_params=_cp("parallel", "parallel", "arbitrary"),
        name="ffn",
    )(x, sh, sc, g2, ng, w_gu, w_gu, w_down, fg)


def _rope_tables(seq):
    t = np.arange(seq)
    inv = ROPE_THETA ** (-np.arange(0, ROPE_AXIS_DIM, 2, dtype=np.float64) / ROPE_AXIS_DIM)
    ang_r = (t // GRID_W)[:, None] * inv
    ang_c = (t % GRID_W)[:, None] * inv
    cos = np.concatenate([np.cos(ang_r), np.cos(ang_r), np.cos(ang_c), np.cos(ang_c)], axis=1)
    sin = np.concatenate([-np.sin(ang_r), np.sin(ang_r), -np.sin(ang_c), np.sin(ang_c)], axis=1)
    return (jnp.asarray(np.tile(cos, (1, 2)), F32), jnp.asarray(np.tile(sin, (1, 2)), F32))


def _hy_feats(seq):
    t = np.linspace(0.0, 1.0, seq)[:, None]
    w = 2.0 * math.pi * np.arange(seq)[:, None] / seq
    f = np.linspace(1e-4, HY_BANDS - 1, HY_BANDS)
    feats = np.concatenate([t, np.cos(f * w), -np.sin(f * w)], axis=1)
    feats_t = np.pad(feats, ((0, 0), (0, LANE - HY_EMB))).T
    return jnp.asarray(feats_t, F32), jnp.asarray(t, F32)


def _hy_abs_deltas():
    lo = math.log(HY_TARGET) / HY_SLOW_DECAY_PCT
    hi = math.log(HY_TARGET) / HY_FAST_DECAY_PCT
    return jnp.asarray(np.abs(np.linspace(lo, hi, HY_WIDTH))[None], F32)


def _fft_tables():
    r_ = FFT_R
    n = r_ * r_
    n2 = np.arange(r_)[:, None, None]
    k1 = np.arange(FFT_K1)[None, :, None]
    n1 = np.arange(r_ // 2)[None, None, :]
    live = (k1 <= r_ // 2).astype(np.float64)
    th = 2.0 * math.pi * ((k1 * (r_ * n1 + n2)) % n) / n
    cos, msin = np.cos(th) * live, -np.sin(th) * live
    wgt = np.where((k1 == 0) | (k1 == r_ // 2), 1.0, 2.0) / n
    fwd = np.concatenate([_group_blocks(cos), _group_blocks(msin)], axis=1)
    inv = np.transpose(np.concatenate([_group_blocks(cos * wgt), _group_blocks(msin * wgt)], axis=1), (0, 2, 1))
    kk = np.arange(r_)
    ph = 2.0 * math.pi * ((kk[:, None] * kk[None, :]) % r_) / r_
    fr, fi = np.cos(ph), -np.sin(ph)
    fs = np.block([[fr, -fi], [fi, fr]])
    fc = np.block([[fr, fi], [-fi, fr]])
    return tuple(jnp.asarray(a, F32) for a in (fwd, inv, fs, fc))


def _group_blocks(t):
    n2, k1, n1 = t.shape
    ng = n2 // FFT_G
    out = np.zeros((ng, k1, FFT_G, n1, FFT_G))
    for r in range(FFT_G):
        out[:, :, r, :, r] = t.reshape(ng, FFT_G, k1, n1)[:, r]
    return out.reshape(ng, k1 * FFT_G, n1 * FFT_G)


def _dense_dft_tables(seq):
    n = 2 * seq
    kk = np.arange(n)
    ph = 2.0 * math.pi * ((kk[:, None] * kk[None, :]) % n) / n
    ff = np.concatenate([np.cos(ph), -np.sin(ph)], axis=0)
    fi = np.concatenate([np.cos(ph[:seq]), -np.sin(ph[:seq])], axis=1) / n
    return jnp.asarray(ff, F32), jnp.asarray(fi, F32)


def _head_expand():
    e = np.zeros((LANE, 2 * SSD_INNER), np.float32)
    for h in range(2 * SSD_HEADS):
        e[h, h * SSD_HEAD_DIM:(h + 1) * SSD_HEAD_DIM] = 1.0
    return jnp.asarray(e, BF16)


def _group_mean():
    g = np.kron(np.eye(LANE // ATTN_HEAD_DIM), np.ones((ATTN_HEAD_DIM, ATTN_HEAD_DIM))) / ATTN_HEAD_DIM
    return jnp.asarray(g, BF16)


def _relayout_w_in(w):
    cols = [w[:, OFF_HY:OFF_HY + 3 * HY_WIDTH], w[:, OFF_Q:OFF_Q + ATTN_INNER],
            w[:, OFF_XB:OFF_XB + SSD_XB], w[:, OFF_C:OFF_C + SSD_GN], w[:, OFF_Z:OFF_Z + SSD_INNER],
            w[:, OFF_K:OFF_K + ATTN_KV_INNER], w[:, OFF_V:OFF_V + ATTN_KV_INNER],
            w[:, OFF_DT:OFF_DT + 2 * SSD_HEADS]]
    wr = jnp.concatenate(cols, axis=1)
    return jnp.pad(wr, ((0, 0), (0, NP - wr.shape[1]))).astype(BF16)


def _pad_row(v):
    v = v.reshape(1, -1)
    return jnp.pad(v, ((0, 0), (0, LANE - v.shape[1])))


def _layer_params(l, raw, tables):
    p = {k: v[l] for k, v in raw.items()}
    p["w_in_r"] = _relayout_w_in(p["w_in"])
    p["dt_bias_row"] = _pad_row(p["ssd_dt_bias"])
    p["a_log_row"] = _pad_row(p["ssd_a_log"])
    p["d_row"] = jnp.repeat(p["ssd_d"], SSD_HEAD_DIM)[None]
    p["head_expand"] = tables["head_expand"]
    p["gq"] = jnp.tile(p["q_norm_g"], LANE // ATTN_HEAD_DIM)[None]
    p["gk"] = jnp.tile(p["k_norm_g"], LANE // ATTN_HEAD_DIM)[None]
    p["hy_w1p"] = jnp.pad(p["hy_w1"], ((0, LANE - HY_EMB), (0, 0)))
    p["w_out_b"] = p["w_out"].astype(BF16)
    p["w_gu_b"] = p["w_gu"].astype(BF16)
    p["w_down_b"] = p["w_down"].astype(BF16)
    return p


def _mixers(proj, dt_raw, p, tables, is_ctx, ssd_init):
    yf, yb, sf, sb = _ssd_scan(proj, dt_raw, ssd_init[0], ssd_init[1], p)
    rope = tables["rope_ctx"] if is_ctx else tables["rope"]
    q, k, v = _attn_prep(proj, rope[0], rope[1], p["gq"], p["gk"], tables["group_mean"])
    return (yf, yb, sf, sb), (q, k, v)


def _hyena(proj, p, tables, is_ctx):
    seq = proj.shape[1]
    bias = p["hy_bias"][None]
    vx, x0 = _hy_conv(proj, p["hy_conv_w"], p["hy_conv_b"], not is_ctx)
    if is_ctx:
        f = _hy_taps(*tables["feats_ctx"], p, tables["abs_deltas"], seq, False)
        taps = jnp.concatenate([f[0], jnp.zeros((1, HY_WIDTH), F32), jnp.flip(f[1, 1:], axis=0)], axis=0)
        return _hy_ctx(vx, x0, taps, tables["dft_ctx"][0], tables["dft_ctx"][1], bias)
    b, ng, _, _, c = vx.shape
    rows = seq // ng
    t_fwd, t_inv, fs, fc = tables["fft"]
    f = _hy_taps(*tables["feats"], p, tables["abs_deltas"], seq, True)
    kre, kim = _fft2_filter(*_fft1(f.reshape(2, ng, rows, c), t_fwd, True), fs)
    vx = vx.reshape(b, ng, rows, c)
    are, aim = _fft1(vx, t_fwd, False)
    zre, zim = _fft2(are, aim, kre, kim, fs, fc)
    y = _ifft1(zre, zim, t_inv, vx, x0.reshape(b, ng, rows, c), bias)
    return y.reshape(b, ng, rows // (2 * FFT_G), 2 * FFT_G, c)


def _tail(x, proj, ssd, at, hy, mod, p, final_g, final):
    x = _mix_out(x, ssd[0], ssd[1], proj, at, hy, mod[2], p["ssd_norm_g"][None], p["attn_norm_g"][None],
                 p["hy_norm_g"][None], p["w_out_b"])
    return _ffn(x, mod[3], mod[4], mod[5], p["norm2_g"][None], p["w_gu_b"], p["w_down_b"], final_g[None], final)


def _layer(x, xc, mod_rows, p, tables, final_g, last):
    b = x.shape[0]
    mod_x = [mod_rows[:b, i * D_MODEL:(i + 1) * D_MODEL][:, None, :] for i in range(N_MOD)]
    mod_c = [jnp.broadcast_to(mod_rows[b:b + 1, i * D_MODEL:(i + 1) * D_MODEL][:, None, :], (b, 1, D_MODEL))
             for i in range(N_MOD)]
    g1 = p["norm1_g"][None]
    proj, dt_raw = _proj_in(x, mod_x[0], mod_x[1], g1, p["w_in_r"])
    lc = xc.shape[1]
    rows = lambda a: a.reshape(1, b * lc, a.shape[-1])
    mod_c1 = [m[:1] for m in mod_c]
    projc, dt_raw_c = _proj_in(rows(xc), mod_c1[0], mod_c1[1], g1, p["w_in_r"])
    projc, dt_raw_c = projc.reshape(b, lc, -1), dt_raw_c.reshape(b, lc, -1)
    zeros = jnp.zeros((b, SSD_STATE, SSD_INNER), F32)
    ssd_c, qkv_c = _mixers(projc, dt_raw_c, p, tables, True, (zeros, zeros))
    ssd_x, qkv_x = _mixers(proj, dt_raw, p, tables, False, (ssd_c[2], ssd_c[3]))
    k_all = jnp.concatenate([qkv_c[1], qkv_x[1]], axis=2)
    v_all = jnp.concatenate([qkv_c[2], qkv_x[2]], axis=3)
    at = _flash(qkv_x[0], k_all, v_all)
    hy = _hyena(proj, p, tables, False)
    x = _tail(x, proj, ssd_x, at, hy, mod_x, p, final_g, last)
    if last:
        return x, None
    at_c = _flash(qkv_c[0], qkv_c[1], qkv_c[2])
    hy_c = _hyena(projc, p, tables, True)
    xc = _tail(rows(xc), rows(projc), (rows(ssd_c[0]), rows(ssd_c[1])), rows(at_c), rows(hy_c), mod_c1, p,
               final_g, False)
    return x, xc.reshape(b, lc, -1)


def kernel(x, c, ctx, c_ctx, w_mod, b_mod, norm1_g, w_in, ssd_conv_w, ssd_conv_b, ssd_a_log, ssd_dt_bias, ssd_d, ssd_norm_g, q_norm_g, k_norm_g, attn_norm_g, hy_conv_w, hy_conv_b, hy_w1, hy_b1, hy_freq, hy_w2, hy_b2, hy_w3, hy_bias, hy_norm_g, w_out, norm2_g, w_gu, w_down, final_g):
    b, seq, _ = x.shape
    ctx_len = ctx.shape[1]
    depth = w_mod.shape[0]
    assert 2 * seq == FFT_R * FFT_R and b + 1 <= SUBLANE
    raw = dict(w_in=w_in, ssd_conv_w=ssd_conv_w, ssd_conv_b=ssd_conv_b, ssd_a_log=ssd_a_log,
               ssd_dt_bias=ssd_dt_bias, ssd_d=ssd_d, ssd_norm_g=ssd_norm_g, q_norm_g=q_norm_g,
               k_norm_g=k_norm_g, attn_norm_g=attn_norm_g, hy_conv_w=hy_conv_w, hy_conv_b=hy_conv_b,
               hy_w1=hy_w1, hy_b1=hy_b1, hy_freq=hy_freq, hy_w2=hy_w2, hy_b2=hy_b2, hy_w3=hy_w3,
               hy_bias=hy_bias, hy_norm_g=hy_norm_g, w_out=w_out, norm2_g=norm2_g, w_gu=w_gu, w_down=w_down,
               norm1_g=norm1_g)
    ones = jnp.ones((ctx_len, LANE), F32)
    tables = dict(rope=_rope_tables(seq), rope_ctx=(ones, jnp.zeros_like(ones)),
                  feats=_hy_feats(seq), feats_ctx=_hy_feats(ctx_len), abs_deltas=_hy_abs_deltas(),
                  fft=_fft_tables(), dft_ctx=_dense_dft_tables(ctx_len),
                  head_expand=_head_expand(), group_mean=_group_mean())
    c_rows = jnp.concatenate([c, c_ctx[None], jnp.zeros((SUBLANE - b - 1, D_MODEL), F32)], axis=0)
    xc = ctx
    for l in range(depth):
        p = _layer_params(l, raw, tables)
        mod_rows = _mod_call(c_rows, w_mod[l], b_mod[l][None])
        x, xc = _layer(x, xc, mod_rows, p, tables, final_g, l == depth - 1)
    return x
```

```python
import functools
import math

import numpy as np
import jax
import jax.numpy as jnp
from jax import lax
from jax.experimental import pallas as pl
from jax.experimental.pallas import tpu as pltpu

F32 = jnp.float32
BF16 = jnp.bfloat16
HI = lax.Precision.HIGHEST

D_MODEL = 1024
GRID_W = 64
EPS = 1e-6
SSD_HEADS = 8
SSD_HEAD_DIM = 64
SSD_INNER = 512
SSD_STATE = 128
SSD_CHUNK = 128
SSD_GN = 256
SSD_XB = SSD_INNER + SSD_GN
ATTN_HEADS = 8
ATTN_KV_HEADS = 2
ATTN_HEAD_DIM = 64
ATTN_INNER = 512
ATTN_KV_INNER = 128
ATTN_SCALE = ATTN_HEAD_DIM ** -0.5
LOG2E = math.log2(math.e)
ATTN_V_ROWS = 80
ROPE_THETA = 10000.0
ROPE_AXIS_DIM = ATTN_HEAD_DIM // 2
HY_WIDTH = 512
HY_BANDS = 16
HY_EMB = 1 + 2 * HY_BANDS
HY_HIDDEN = 64
HY_FAST_DECAY_PCT = 0.3
HY_SLOW_DECAY_PCT = 1.5
HY_TARGET = 1e-2
D_MIX = 1536
D_FF = 2816
N_MOD = 6

OFF_K = 0
OFF_V = OFF_K + ATTN_KV_INNER
OFF_XB = OFF_V + ATTN_KV_INNER
OFF_DT = OFF_XB + SSD_XB
OFF_C = OFF_DT + 2 * SSD_HEADS
OFF_Q = OFF_C + SSD_GN
OFF_Z = OFF_Q + ATTN_INNER
OFF_HY = OFF_Z + SSD_INNER

P_HY = 0
P_Q = 1536
P_X = 2048
P_B = 2560
P_C = 2816
P_Z = 3072
P_K = 3584
P_V = 3712
P_DT = 3840
NP = 4096

LANE = 128
SUBLANE = 8
HALO = 16
VMEM_LIMIT = 48 * 1024 * 1024

FFT_R = 128
FFT_G = 8
FFT_K1 = 72


def _cp(*sem):
    return pltpu.CompilerParams(dimension_semantics=sem, vmem_limit_bytes=VMEM_LIMIT)


def _silu(x):
    return x * (1.0 / (1.0 + jnp.exp(-x)))


def _softplus(x):
    return jnp.maximum(x, 0.0) + jnp.log(1.0 + jnp.exp(-jnp.abs(x)))


def _rms(x, g):
    return x * lax.rsqrt(jnp.mean(x * x, axis=-1, keepdims=True) + EPS) * g


def _dot(a, b):
    return jnp.dot(a, b, preferred_element_type=F32)


def _dot_hi(a, b):
    return jnp.dot(a, b, precision=HI, preferred_element_type=F32)


def _dot1_table(t, x):
    return _dot(t.astype(BF16), x.astype(BF16))


def _mod_kernel(c_ref, w_ref, b_ref, o_ref):
    o_ref[...] = _dot_hi(_silu(c_ref[...]), w_ref[...]) + b_ref[...]


def _mod_call(c_rows, w, b):
    n = w.shape[1]
    tn = 1024
    return pl.pallas_call(
        _mod_kernel,
        grid=(n // tn,),
        in_specs=[pl.BlockSpec((SUBLANE, D_MODEL), lambda j: (0, 0)),
                  pl.BlockSpec((D_MODEL, tn), lambda j: (0, j)),
                  pl.BlockSpec((1, tn), lambda j: (0, j))],
        out_specs=pl.BlockSpec((SUBLANE, tn), lambda j: (0, j)),
        out_shape=jax.ShapeDtypeStruct((SUBLANE, n), F32),
        compiler_params=_cp("parallel"),
        name="adaln_mod",
    )(c_rows, w, b)


def _proj_kernel(x_ref, sh_ref, sc_ref, g_ref, w_ref, o_ref, dt_ref, h_ref, *, dt_tile, dt_off):
    j = pl.program_id(2)

    @pl.when(j == 0)
    def _():
        y = _rms(x_ref[0], g_ref[...])
        h_ref[...] = (y * (1.0 + sc_ref[0]) + sh_ref[0]).astype(BF16)

    r = _dot(h_ref[...], w_ref[...])
    o_ref[0] = r.astype(BF16)

    @pl.when(j == dt_tile)
    def _():
        dt_ref[0] = r[:, dt_off:dt_off + LANE]


def _proj_in(x, sh, sc, g, w):
    b, l, _ = x.shape
    tm = min(l, 1024)
    tn = 1024
    return pl.pallas_call(
        functools.partial(_proj_kernel, dt_tile=P_DT // tn, dt_off=P_DT % tn),
        grid=(b, l // tm, NP // tn),
        in_specs=[pl.BlockSpec((1, tm, D_MODEL), lambda bi, i, j: (bi, i, 0)),
                  pl.BlockSpec((1, 1, D_MODEL), lambda bi, i, j: (bi, 0, 0)),
                  pl.BlockSpec((1, 1, D_MODEL), lambda bi, i, j: (bi, 0, 0)),
                  pl.BlockSpec((1, D_MODEL), lambda bi, i, j: (0, 0)),
                  pl.BlockSpec((D_MODEL, tn), lambda bi, i, j: (0, j))],
        out_specs=[pl.BlockSpec((1, tm, tn), lambda bi, i, j: (bi, i, j)),
                   pl.BlockSpec((1, tm, LANE), lambda bi, i, j: (bi, i, 0))],
        out_shape=[jax.ShapeDtypeStruct((b, l, NP), BF16), jax.ShapeDtypeStruct((b, l, LANE), F32)],
        scratch_shapes=[pltpu.VMEM((tm, D_MODEL), BF16)],
        compiler_params=_cp("parallel", "parallel", "arbitrary"),
        name="proj_in",
    )(x, sh, sc, g, w)


def _dwconv3(u, prev_row, next_row, w, b):
    tm = u.shape[0]
    ri = lax.broadcasted_iota(jnp.int32, u.shape, 0)
    um = jnp.where(ri == 0, prev_row, pltpu.roll(u, 1, 0))
    up = jnp.where(ri == tm - 1, next_row, pltpu.roll(u, tm - 1, 0))
    return um * w[0:1] + u * w[1:2] + up * w[2:3] + b


def _conv_group(refs, i, n_i):
    u_ref, p_ref, n_ref, w_ref, b_ref = refs
    prev_row = jnp.where(i > 0, p_ref[0].astype(F32)[HALO - 1:HALO, :], 0.0)
    next_row = jnp.where(i < n_i - 1, n_ref[0].astype(F32)[0:1, :], 0.0)
    return _dwconv3(u_ref[0].astype(F32), prev_row, next_row, w_ref[...], b_ref[...])


def _hy_conv_kernel(*refs, group_major):
    i, n_i = pl.program_id(1), pl.num_programs(1)
    v = _conv_group(refs[0:5], i, n_i)
    x1 = _conv_group(refs[5:10], i, n_i)
    x0 = _conv_group(refs[10:15], i, n_i)
    vx_ref, x0_ref = refs[15], refs[16]
    vx = v * x1
    if group_major:
        for ref, u in ((vx_ref, vx), (x0_ref, x0)):
            g = _to_group_major(u)
            ref[0] = g.reshape(g.shape[0], g.shape[1] // 2, 2 * g.shape[2], g.shape[3]).astype(ref.dtype)
    else:
        vx_ref[0] = vx
        x0_ref[0] = x0


def _to_group_major(u):
    rows, c = u.shape
    return jnp.swapaxes(u.reshape(rows // FFT_R, FFT_R // FFT_G, FFT_G, c), 0, 1)


def _from_group_major(u):
    g, n1, r, c = u.shape
    return jnp.swapaxes(u, 0, 1).reshape(n1 * g * r, c)


def _conv_specs(tm, tc, l, col_block, w_block):
    nrb = l // HALO
    per = tm // HALO
    return [
        pl.BlockSpec((1, tm, tc), lambda bi, i: (bi, i, col_block)),
        pl.BlockSpec((1, HALO, tc), lambda bi, i: (bi, jnp.maximum(i * per - 1, 0), col_block)),
        pl.BlockSpec((1, HALO, tc), lambda bi, i: (bi, jnp.minimum((i + 1) * per, nrb - 1), col_block)),
        pl.BlockSpec((SUBLANE, tc), lambda bi, i: (0, w_block)),
        pl.BlockSpec((1, tc), lambda bi, i: (0, w_block)),
    ]


def _pad_taps(w):
    return jnp.pad(w, ((0, SUBLANE - w.shape[0]), (0, 0)))


def _hy_conv(proj, conv_w, conv_b, group_major):
    b, l, _ = proj.shape
    tm = min(l, 1024)
    tc = HY_WIDTH
    wp = _pad_taps(conv_w)
    bp = conv_b[None]
    specs, args = [], []
    for grp in range(3):
        specs += _conv_specs(tm, tc, l, P_HY // tc + grp, grp)
        args += [proj, proj, proj, wp, bp]
    if group_major:
        ng = FFT_R // FFT_G
        out_spec = pl.BlockSpec((1, ng, tm // FFT_R // 2, 2 * FFT_G, tc), lambda bi, i: (bi, 0, i, 0, 0))
        shp = jax.ShapeDtypeStruct((b, ng, l // FFT_R // 2, 2 * FFT_G, tc), BF16)
    else:
        out_spec = pl.BlockSpec((1, tm, tc), lambda bi, i: (bi, i, 0))
        shp = jax.ShapeDtypeStruct((b, l, tc), F32)
    return pl.pallas_call(
        functools.partial(_hy_conv_kernel, group_major=group_major),
        grid=(b, l // tm),
        in_specs=specs,
        out_specs=[out_spec, out_spec],
        out_shape=[shp, shp],
        compiler_params=_cp("parallel", "parallel"),
        name="hy_conv",
    )(*args)


def _split_pieces(x, n):
    pieces, r = [], x
    for k in range(n):
        pc = r.astype(BF16)
        pieces.append(pc)
        if k + 1 < n:
            r = r - pc.astype(F32)
    return pieces


def _select_left(sel, x, n):
    w = x.shape[1]
    r = _dot(sel, jnp.concatenate(_split_pieces(x, n), axis=1))
    return sum(r[:, k * w:(k + 1) * w] for k in range(n))


def _select_right(x, sel, n):
    m = x.shape[0]
    r = _dot(jnp.concatenate(_split_pieces(x, n), axis=0), sel)
    return sum(r[k * m:(k + 1) * m] for k in range(n))


def _ssd_chunk(xbc, dt_raw, a, bias, d_row, e_d, st, lane0, fwd):
    q = SSD_CHUNK
    xs = xbc[:, :SSD_INNER]
    bm = xbc[:, SSD_INNER:SSD_XB]
    cm = xbc[:, SSD_XB:]
    ri = lax.broadcasted_iota(jnp.int32, (q, q), 0)
    ci = lax.broadcasted_iota(jnp.int32, (q, q), 1)
    dt = _softplus(dt_raw + bias)
    adt = dt * a
    cs = _select_left((ci <= ri).astype(BF16), adt, 3)
    tot = cs[q - 1:q, :]
    if fwd:
        key = cs
        w_c = dt * jnp.exp(tot - key)
        e_c = jnp.exp(key)
        mask = ci <= ri
    else:
        key = cs - adt
        w_c = dt * jnp.exp(key)
        e_c = jnp.exp(tot - key)
        mask = ci >= ri
    key_t = key.T
    dt_t = dt.T
    dec_c = jnp.broadcast_to(jnp.exp(tot), (2 * SUBLANE, LANE))
    spread = _select_right(jnp.concatenate([w_c, e_c, dec_c], axis=0), e_d, 2)
    w_e, e_off, decay = spread[:q], spread[q:2 * q], spread[2 * q:2 * q + 1]
    x_b = xs.astype(BF16)
    x_w = (xs * w_e).astype(BF16)
    st_b = st.astype(BF16)
    y_parts, st_parts, off_parts = [], [], []
    hg = SSD_HEADS // 2
    for g in range(2):
        bg = bm[:, g * SSD_STATE:(g + 1) * SSD_STATE]
        cg = cm[:, g * SSD_STATE:(g + 1) * SSD_STATE].astype(BF16)
        bg_t = bg.T.astype(BF16)
        gmat = _dot(cg, bg_t)
        gs = slice(g * hg * SSD_HEAD_DIM, (g + 1) * hg * SSD_HEAD_DIM)
        off_parts.append(_dot(cg, st_b[:, gs]))
        st_parts.append(_dot(bg_t, x_w[:, gs]))
        for hh in range(hg):
            h = g * hg + hh
            col = key[:, lane0 + h:lane0 + h + 1]
            row = key_t[lane0 + h:lane0 + h + 1, :]
            diff = (col - row) if fwd else (row - col)
            lm = jnp.exp(jnp.where(mask, diff, -1e30)) * dt_t[lane0 + h:lane0 + h + 1, :]
            s = (gmat * lm).astype(BF16)
            y_parts.append(_dot(s, x_b[:, h * SSD_HEAD_DIM:(h + 1) * SSD_HEAD_DIM]))
    y = jnp.concatenate(y_parts, axis=1) + jnp.concatenate(off_parts, axis=1) * e_off
    if fwd:
        y = y + d_row * xs
    st_new = st * decay + jnp.concatenate(st_parts, axis=1)
    return y, st_new


def _ssd_kernel(xf_ref, xfp_ref, xfn_ref, xb_ref, xbp_ref, xbn_ref, dtf_ref, dtb_ref, cw_ref, cb_ref,
                bias_ref, alog_ref, d_ref, e_ref, sf0_ref, sb0_ref, yf_ref, yb_ref, sf_ref, sb_ref, stf, stb):
    c = pl.program_id(0)
    nc = pl.num_programs(0)

    @pl.when(c == 0)
    def _():
        stf[...] = sf0_ref[...]
        stb[...] = sb0_ref[...]

    def conv_silu(u_ref, p_ref, n_ref, bi, chunk):
        prev_row = jnp.where(chunk > 0, p_ref[bi].astype(F32)[HALO - 1:HALO, :], 0.0)
        next_row = jnp.where(chunk < nc - 1, n_ref[bi].astype(F32)[0:1, :], 0.0)
        return _silu(_dwconv3(u_ref[bi].astype(F32), prev_row, next_row, cw_ref[...], cb_ref[...]))

    a = -jnp.exp(alog_ref[...])
    bias = bias_ref[...]
    for bi in range(xf_ref.shape[0]):
        xf = conv_silu(xf_ref, xfp_ref, xfn_ref, bi, c)
        xb = conv_silu(xb_ref, xbp_ref, xbn_ref, bi, nc - 1 - c)
        yf, sf = _ssd_chunk(xf, dtf_ref[bi], a, bias, d_ref[...], e_ref[:, :SSD_INNER], stf[bi], 0, True)
        yb, sb = _ssd_chunk(xb, dtb_ref[bi], a, bias, d_ref[...], e_ref[:, SSD_INNER:], stb[bi], SSD_HEADS, False)
        yf_ref[bi] = yf.astype(yf_ref.dtype)
        yb_ref[bi] = yb.astype(yb_ref.dtype)
        stf[bi] = sf
        stb[bi] = sb
        sf_ref[bi] = sf
        sb_ref[bi] = sb


def _ssd_scan(proj, dt_raw, sf0, sb0, p):
    b, l, _ = proj.shape
    nc = l // SSD_CHUNK
    q = SSD_CHUNK
    w = 2 * SSD_INNER
    xcol = P_X // w
    per = q // HALO
    nrb = l // HALO
    st_spec = pl.BlockSpec((b, SSD_STATE, SSD_INNER), lambda c: (0, 0, 0))
    y_shape = jax.ShapeDtypeStruct((b, l, SSD_INNER), BF16)
    st_shape = jax.ShapeDtypeStruct((b, SSD_STATE, SSD_INNER), F32)
    row = lambda n: pl.BlockSpec((1, n), lambda c: (0, 0))

    def chunk_specs(chunk_of):
        return [pl.BlockSpec((b, q, w), lambda c: (0, chunk_of(c), xcol)),
                pl.BlockSpec((b, HALO, w), lambda c: (0, jnp.maximum(chunk_of(c) * per - 1, 0), xcol)),
                pl.BlockSpec((b, HALO, w), lambda c: (0, jnp.minimum((chunk_of(c) + 1) * per, nrb - 1), xcol))]

    fwd_of = lambda c: c
    bwd_of = lambda c: nc - 1 - c
    return pl.pallas_call(
        _ssd_kernel,
        grid=(nc,),
        in_specs=chunk_specs(fwd_of) + chunk_specs(bwd_of) + [
            pl.BlockSpec((b, q, LANE), lambda c: (0, c, 0)),
            pl.BlockSpec((b, q, LANE), lambda c: (0, nc - 1 - c, 0)),
            pl.BlockSpec((SUBLANE, w), lambda c: (0, 0)), row(w),
            row(LANE), row(LANE), row(SSD_INNER),
            pl.BlockSpec((LANE, 2 * SSD_INNER), lambda c: (0, 0)),
            st_spec, st_spec],
        out_specs=[pl.BlockSpec((b, q, SSD_INNER), lambda c: (0, c, 0)),
                   pl.BlockSpec((b, q, SSD_INNER), lambda c: (0, nc - 1 - c, 0)),
                   st_spec, st_spec],
        out_shape=[y_shape, y_shape, st_shape, st_shape],
        scratch_shapes=[pltpu.VMEM((b, SSD_STATE, SSD_INNER), F32), pltpu.VMEM((b, SSD_STATE, SSD_INNER), F32)],
        compiler_params=_cp("arbitrary"),
        name="ssd_scan",
    )(proj, proj, proj, proj, proj, proj, dt_raw, dt_raw, _pad_taps(p["ssd_conv_w"]), p["ssd_conv_b"][None],
      p["dt_bias_row"], p["a_log_row"], p["d_row"], p["head_expand"], sf0, sb0)


def _attn_prep_kernel(q_ref, k_ref, v_ref, cos_ref, sin_ref, gq_ref, gk_ref, gm_ref, qo_ref, ko_ref, vo_ref):
    cos = cos_ref[...]
    sin = sin_ref[...]
    gm = gm_ref[...]
    lane = lax.broadcasted_iota(jnp.int32, cos.shape, 1)
    first = jnp.bitwise_and(lane, 31) < 16
    hd = ATTN_HEAD_DIM

    def norm_rope(t, g):
        ms = _select_right(t * t, gm, 2)
        y = t * lax.rsqrt(ms + EPS) * g
        partner = jnp.where(first, pltpu.roll(y, LANE - 16, 1), pltpu.roll(y, 16, 1))
        return y * cos + partner * sin

    for s in range(ATTN_INNER // LANE):
        qs = norm_rope(q_ref[0, :, s * LANE:(s + 1) * LANE].astype(F32), gq_ref[...]) * (ATTN_SCALE * LOG2E)
        qt = qs.T.astype(BF16)
        qo_ref[0, 2 * s] = qt[:hd]
        qo_ref[0, 2 * s + 1] = qt[hd:]
    ks = norm_rope(k_ref[0].astype(F32), gk_ref[...])
    ko_ref[0, 0] = ks[:, :hd].astype(BF16)
    ko_ref[0, 1] = ks[:, hd:].astype(BF16)
    vt = v_ref[0].astype(F32).T.astype(BF16)
    tm = vt.shape[1]
    pad_rows = lax.broadcasted_iota(jnp.int32, (ATTN_V_ROWS - hd, tm), 0)
    tail = jnp.where(pad_rows == 0, 1.0, 0.0).astype(BF16)
    for g in range(ATTN_KV_HEADS):
        vo_ref[0, g, 0:hd, :] = vt[g * hd:(g + 1) * hd]
        vo_ref[0, g, hd:ATTN_V_ROWS, :] = tail


def _attn_prep(proj, cos, sin, gq, gk, gm):
    b, l, _ = proj.shape
    tm = min(l, 1024)
    hd = ATTN_HEAD_DIM
    const = lambda r, c: pl.BlockSpec((r, c), lambda bi, i: (0, 0))
    return pl.pallas_call(
        _attn_prep_kernel,
        grid=(b, l // tm),
        in_specs=[pl.BlockSpec((1, tm, ATTN_INNER), lambda bi, i: (bi, i, P_Q // ATTN_INNER)),
                  pl.BlockSpec((1, tm, LANE), lambda bi, i: (bi, i, P_K // LANE)),
                  pl.BlockSpec((1, tm, LANE), lambda bi, i: (bi, i, P_V // LANE)),
                  pl.BlockSpec((tm, LANE), lambda bi, i: (i, 0)),
                  pl.BlockSpec((tm, LANE), lambda bi, i: (i, 0)),
                  const(1, LANE), const(1, LANE), const(LANE, LANE)],
        out_specs=[pl.BlockSpec((1, ATTN_HEADS, hd, tm), lambda bi, i: (bi, 0, 0, i)),
                   pl.BlockSpec((1, ATTN_KV_HEADS, tm, hd), lambda bi, i: (bi, 0, i, 0)),
                   pl.BlockSpec((1, ATTN_KV_HEADS, ATTN_V_ROWS, tm), lambda bi, i: (bi, 0, 0, i))],
        out_shape=[jax.ShapeDtypeStruct((b, ATTN_HEADS, hd, l), BF16),
                   jax.ShapeDtypeStruct((b, ATTN_KV_HEADS, l, hd), BF16),
                   jax.ShapeDtypeStruct((b, ATTN_KV_HEADS, ATTN_V_ROWS, l), BF16)],
        compiler_params=_cp("parallel", "parallel"),
        name="attn_prep",
    )(proj, proj, proj, cos, sin, gq, gk, gm)


def _flash_kernel(q_ref, k_ref, v_ref, o_ref, s_ref, m_ref, acc_ref, *, tk, nk):
    r = ATTN_HEADS // ATTN_KV_HEADS
    hd = ATTN_HEAD_DIM

    def scores(j, h, slot):
        start = pl.multiple_of(j * tk, tk)
        s_ref[slot] = _dot(k_ref[0, 0, pl.ds(start, tk), :], q_ref[0, h])

    m_ref[...] = jnp.full(m_ref.shape, -1e30, F32)
    acc_ref[...] = jnp.zeros(acc_ref.shape, F32)
    scores(0, 0, 0)

    def body(j, _):
        start = pl.multiple_of(j * tk, tk)
        vs = v_ref[0, 0, :, pl.ds(start, tk)]
        j_next = jnp.minimum(j + 1, nk - 1)
        for h in range(r):
            if h < r - 1:
                scores(j, h + 1, (h + 1) % 2)
            else:
                scores(j_next, 0, 0)
            s = s_ref[h % 2]
            m = m_ref[h]
            mn = jnp.maximum(m, jnp.max(s, axis=0, keepdims=True))
            alpha = jnp.exp2(m - mn)
            p = jnp.exp2(s - mn).astype(BF16)
            m_ref[h] = mn
            acc_ref[h] = alpha * acc_ref[h] + _dot(vs, p)
        return 0

    lax.fori_loop(0, nk, body, 0)
    outs = []
    for h in range(r):
        acc = acc_ref[h]
        outs.append((acc[:hd] * (1.0 / acc[hd:hd + 1])).T)
    o_ref[0] = jnp.concatenate(outs, axis=1).astype(o_ref.dtype)


def _flash(q, k, v):
    b, _, hd, l = q.shape
    lk = k.shape[2]
    r = ATTN_HEADS // ATTN_KV_HEADS
    assert r % 2 == 0
    tq = min(l, 1024)
    tk = 384 if lk % 384 == 0 else 256
    return pl.pallas_call(
        functools.partial(_flash_kernel, tk=tk, nk=lk // tk),
        grid=(b, ATTN_KV_HEADS, l // tq),
        in_specs=[pl.BlockSpec((1, r, hd, tq), lambda bi, g, i: (bi, g, 0, i)),
                  pl.BlockSpec((1, 1, lk, hd), lambda bi, g, i: (bi, g, 0, 0)),
                  pl.BlockSpec((1, 1, ATTN_V_ROWS, lk), lambda bi, g, i: (bi, g, 0, 0))],
        out_specs=pl.BlockSpec((1, tq, r * hd), lambda bi, g, i: (bi, i, g)),
        out_shape=jax.ShapeDtypeStruct((b, l, ATTN_INNER), BF16),
        scratch_shapes=[pltpu.VMEM((2, tk, tq), F32), pltpu.VMEM((r, 1, tq), F32),
                        pltpu.VMEM((r, ATTN_V_ROWS, tq), F32)],
        compiler_params=_cp("parallel", "parallel", "parallel"),
        name="flash_gqa",
    )(q, k, v)


def _taps_kernel(ft_ref, t_ref, w1_ref, b1_ref, fr_ref, w2_ref, b2_ref, w3_ref, adel_ref, o_ref, *, group_major):
    fr = fr_ref[...]
    h = jnp.sin(fr * (_dot_hi(w1_ref[...], ft_ref[...]) + b1_ref[...]))
    h = jnp.sin(fr * (_dot_hi(w2_ref[...], h) + b2_ref[...]))
    y = _dot_hi(h.T, w3_ref[...])
    decay = jnp.exp(-t_ref[...] * adel_ref[...])
    f0 = y[:, :HY_WIDTH] * decay
    f1 = y[:, HY_WIDTH:] * decay
    lag = pl.program_id(0) * f1.shape[0] + lax.broadcasted_iota(jnp.int32, f1.shape, 0)
    f1 = jnp.where(lag == 0, 0.0, f1)
    o_ref[0] = _to_group_major(f0) if group_major else f0
    o_ref[1] = _to_group_major(f1) if group_major else f1


def _hy_taps(feats_t, t_col, p, adel, seq, group_major):
    tm = min(seq, 1024)
    const = lambda r, c: pl.BlockSpec((r, c), lambda i: (0, 0))
    if group_major:
        ng = FFT_R // FFT_G
        out = pl.BlockSpec((2, ng, tm // FFT_R, FFT_G, HY_WIDTH), lambda i: (0, 0, i, 0, 0))
        shp = jax.ShapeDtypeStruct((2, ng, seq // FFT_R, FFT_G, HY_WIDTH), F32)
    else:
        out = pl.BlockSpec((2, tm, HY_WIDTH), lambda i: (0, i, 0))
        shp = jax.ShapeDtypeStruct((2, seq, HY_WIDTH), F32)
    return pl.pallas_call(
        functools.partial(_taps_kernel, group_major=group_major),
        grid=(seq // tm,),
        in_specs=[pl.BlockSpec((LANE, tm), lambda i: (0, i)),
                  pl.BlockSpec((tm, 1), lambda i: (i, 0)),
                  const(HY_HIDDEN, LANE), const(HY_HIDDEN, 1), const(HY_HIDDEN, 1),
                  const(HY_HIDDEN, HY_HIDDEN), const(HY_HIDDEN, 1),
                  const(HY_HIDDEN, 2 * HY_WIDTH), const(1, HY_WIDTH)],
        out_specs=out,
        out_shape=shp,
        compiler_params=_cp("parallel"),
        name="hy_taps",
    )(feats_t, t_col, p["hy_w1p"].T, p["hy_b1"][:, None], p["hy_freq"][:, None], p["hy_w2"].T,
      p["hy_b2"][:, None], p["hy_w3"], adel)


def _fft1_kernel(x_ref, t_ref, are_ref, aim_ref):
    a = _dot1_table(t_ref[0], x_ref[0])
    rows = FFT_K1 * FFT_G
    are_ref[0] = a[:rows].astype(are_ref.dtype)
    aim_ref[0] = a[rows:].astype(aim_ref.dtype)


def _fft1(x, table):
    nb, ng, rows, c = x.shape
    spec_o = pl.BlockSpec((1, None, FFT_K1 * FFT_G, c), lambda g, bi: (bi, g, 0, 0))
    shp = jax.ShapeDtypeStruct((nb, ng, FFT_K1 * FFT_G, c), BF16)
    return pl.pallas_call(
        _fft1_kernel,
        grid=(ng, nb),
        in_specs=[pl.BlockSpec((1, None, rows, c), lambda g, bi: (bi, g, 0, 0)),
                  pl.BlockSpec((1,) + table.shape[1:], lambda g, bi: (g, 0, 0))],
        out_specs=[spec_o, spec_o],
        out_shape=[shp, shp],
        compiler_params=_cp("parallel", "parallel"),
        name="hy_fft1",
    )(x, table)


def _pair_rows(blk, par):
    half = blk[:, par * FFT_G:(par + 1) * FFT_G]
    return half.reshape(half.shape[0] * half.shape[1], half.shape[2])


def _fft2_filter_kernel(are_ref, aim_ref, fs_ref, kre_ref, kim_ref):
    r_ = FFT_R
    for kp in range(SUBLANE // 2):
        a = [(are_ref[e, :, kp].astype(F32), aim_ref[e, :, kp].astype(F32)) for e in range(2)]
        for par in range(2):
            xc, xa = [_dot1_table(fs_ref[...], jnp.concatenate([_pair_rows(re, par), _pair_rows(im, par)], axis=0))
                      for re, im in a]
            kre_ref[2 * kp + par] = xc[:r_] + xa[:r_]
            kim_ref[2 * kp + par] = xc[r_:] - xa[r_:]


def _fft2_filter(are, aim, fs):
    r_ = FFT_R
    nb, ng, _, c = are.shape
    view = lambda a: a.reshape(nb, ng, FFT_K1 // 2, 2 * FFT_G, c)
    spec_i = pl.BlockSpec((nb, ng, SUBLANE // 2, 2 * FFT_G, c), lambda kg: (0, 0, kg, 0, 0))
    spec_o = pl.BlockSpec((SUBLANE, r_, c), lambda kg: (kg, 0, 0))
    shp = jax.ShapeDtypeStruct((FFT_K1, r_, c), F32)
    return pl.pallas_call(
        _fft2_filter_kernel,
        grid=(FFT_K1 // SUBLANE,),
        in_specs=[spec_i, spec_i, pl.BlockSpec((2 * r_, 2 * r_), lambda kg: (0, 0))],
        out_specs=[spec_o, spec_o],
        out_shape=[shp, shp],
        compiler_params=_cp("parallel"),
        name="hy_fft2_filter",
    )(view(are), view(aim), fs)


def _fft2_kernel(are_ref, aim_ref, kre_ref, kim_ref, fs_ref, fc_ref, zre_ref, zim_ref):
    r_ = FFT_R
    nb, ng, c = are_ref.shape[0], are_ref.shape[1], are_ref.shape[4]
    for b in range(nb):
        for kp in range(SUBLANE // 2):
            a_re = are_ref[b, :, kp].astype(F32)
            a_im = aim_ref[b, :, kp].astype(F32)
            z_re, z_im = [], []
            for par in range(2):
                kk = 2 * kp + par
                a = jnp.concatenate([_pair_rows(a_re, par), _pair_rows(a_im, par)], axis=0)
                bb = _dot1_table(fs_ref[...], a)
                br, bi = bb[:r_], bb[r_:]
                kr, ki = kre_ref[kk], kim_ref[kk]
                y = jnp.concatenate([br * kr - bi * ki, br * ki + bi * kr], axis=0)
                z = _dot1_table(fc_ref[...], y)
                z_re.append(z[:r_].reshape(ng, FFT_G, c))
                z_im.append(z[r_:].reshape(ng, FFT_G, c))
            zre_ref[b, :, kp] = jnp.concatenate(z_re, axis=1).astype(zre_ref.dtype)
            zim_ref[b, :, kp] = jnp.concatenate(z_im, axis=1).astype(zim_ref.dtype)


def _fft2(are, aim, kre, kim, fs, fc):
    nb, ng, rows, c = are.shape
    r_ = FFT_R
    view = lambda a: a.reshape(nb, ng, FFT_K1 // 2, 2 * FFT_G, c)
    spec_a = pl.BlockSpec((nb, ng, SUBLANE // 2, 2 * FFT_G, c), lambda kg: (0, 0, kg, 0, 0))
    spec_k = pl.BlockSpec((SUBLANE, r_, c), lambda kg: (kg, 0, 0))
    spec_f = pl.BlockSpec((2 * r_, 2 * r_), lambda kg: (0, 0))
    shp = jax.ShapeDtypeStruct((nb, ng, FFT_K1 // 2, 2 * FFT_G, c), BF16)
    zre, zim = pl.pallas_call(
        _fft2_kernel,
        grid=(FFT_K1 // SUBLANE,),
        in_specs=[spec_a, spec_a, spec_k, spec_k, spec_f, spec_f],
        out_specs=[spec_a, spec_a],
        out_shape=[shp, shp],
        compiler_params=_cp("parallel"),
        name="hy_fft2",
    )(view(are), view(aim), kre, kim, fs, fc)
    return zre.reshape(are.shape), zim.reshape(are.shape)


def _ifft1_kernel(zre_ref, zim_ref, t_ref, vx_ref, x0_ref, bias_ref, o_ref):
    z = jnp.concatenate([zre_ref[0], zim_ref[0]], axis=0)
    conv = _dot1_table(t_ref[0], z)
    o_ref[0] = (x0_ref[0].astype(F32) * (conv + bias_ref[...] * vx_ref[0].astype(F32))).astype(o_ref.dtype)


def _ifft1(zre, zim, table, vx, x0, bias):
    nb, ng, rows, c = vx.shape
    spec_z = pl.BlockSpec((1, None, FFT_K1 * FFT_G, c), lambda g, bi: (bi, g, 0, 0))
    spec_x = pl.BlockSpec((1, None, rows, c), lambda g, bi: (bi, g, 0, 0))
    return pl.pallas_call(
        _ifft1_kernel,
        grid=(ng, nb),
        in_specs=[spec_z, spec_z,
                  pl.BlockSpec((1,) + table.shape[1:], lambda g, bi: (g, 0, 0)),
                  spec_x, spec_x,
                  pl.BlockSpec((1, c), lambda g, bi: (0, 0))],
        out_specs=spec_x,
        out_shape=jax.ShapeDtypeStruct(vx.shape, BF16),
        compiler_params=_cp("parallel", "parallel"),
        name="hy_ifft1",
    )(zre, zim, table, vx, x0, bias)


def _hy_ctx_kernel(vx_ref, x0_ref, taps_ref, ff_ref, fi_ref, bias_ref, o_ref):
    vx = vx_ref[0]
    seq = vx.shape[0]
    n = 2 * seq
    u = _dot_hi(ff_ref[:, :seq], vx)
    k = _dot_hi(ff_ref[...], taps_ref[...])
    ur, ui = u[:n], u[n:]
    kr, ki = k[:n], k[n:]
    y = jnp.concatenate([ur * kr - ui * ki, ur * ki + ui * kr], axis=0)
    conv = _dot_hi(fi_ref[...], y)
    o_ref[0] = x0_ref[0] * (conv + bias_ref[...] * vx)


def _hy_ctx(vx, x0, taps, ff, fi, bias):
    b, seq, c = vx.shape
    n = 2 * seq
    spec_x = pl.BlockSpec((1, seq, c), lambda bi: (bi, 0, 0))
    const = lambda r, cc: pl.BlockSpec((r, cc), lambda bi: (0, 0))
    return pl.pallas_call(
        _hy_ctx_kernel,
        grid=(b,),
        in_specs=[spec_x, spec_x, const(n, c), const(2 * n, n), const(seq, 2 * n), const(1, c)],
        out_specs=spec_x,
        out_shape=jax.ShapeDtypeStruct((b, seq, c), F32),
        compiler_params=_cp("parallel"),
        name="hy_ctx",
    )(vx, x0, taps, ff, fi, bias)


def _mix_kernel(x_ref, yf_ref, yb_ref, z_ref, at_ref, hy_ref, g1_ref, gs_ref, ga_ref, gh_ref, w_ref, o_ref):
    ys = _rms((yf_ref[0].astype(F32) + yb_ref[0].astype(F32)) * _silu(z_ref[0].astype(F32)), gs_ref[...]).astype(BF16)
    ya = _rms(at_ref[0].astype(F32), ga_ref[...]).astype(BF16)
    hy = hy_ref[0].astype(F32)
    if hy.ndim == 4:
        hy = _from_group_major(hy.reshape(hy.shape[0], -1, FFT_G, hy.shape[3]))
    yh = _rms(hy, gh_ref[...]).astype(BF16)
    r = (_dot(ys, w_ref[0:SSD_INNER, :])
         + _dot(ya, w_ref[SSD_INNER:SSD_INNER + ATTN_INNER, :])
         + _dot(yh, w_ref[SSD_INNER + ATTN_INNER:, :]))
    o_ref[0] = x_ref[0] + g1_ref[0] * r


def _mix_out(x, yf, yb, proj, at, hy, g1, gs, ga, gh, w):
    b, l, _ = x.shape
    tm = min(l, 512)
    c = SSD_INNER
    t512 = lambda col: pl.BlockSpec((1, tm, c), lambda bi, i: (bi, i, col))
    const = lambda r, cc: pl.BlockSpec((r, cc), lambda bi, i: (0, 0))
    xs = pl.BlockSpec((1, tm, D_MODEL), lambda bi, i: (bi, i, 0))
    if hy.ndim == 5:
        hy_spec = pl.BlockSpec((1, hy.shape[1], tm // FFT_R // 2, hy.shape[3], c), lambda bi, i: (bi, 0, i, 0, 0))
    else:
        hy_spec = t512(0)
    return pl.pallas_call(
        _mix_kernel,
        grid=(b, l // tm),
        in_specs=[xs, t512(0), t512(0), t512(P_Z // c), t512(0), hy_spec,
                  pl.BlockSpec((1, 1, D_MODEL), lambda bi, i: (bi, 0, 0)),
                  const(1, c), const(1, c), const(1, c), const(D_MIX, D_MODEL)],
        out_specs=xs,
        out_shape=jax.ShapeDtypeStruct(x.shape, F32),
        compiler_params=_cp("parallel", "parallel"),
        name="mix_out",
    )(x, yf, yb, proj, at, hy, g1, gs, ga, gh, w)


def _ffn_kernel(x_ref, sh_ref, sc_ref, g2_ref, ng_ref, wg_ref, wu_ref, wd_ref, fg_ref, o_ref, h_ref, acc_ref,
                *, final):
    j = pl.program_id(2)

    @pl.when(j == 0)
    def _():
        y = _rms(x_ref[0], ng_ref[...])
        h_ref[...] = (y * (1.0 + sc_ref[0]) + sh_ref[0]).astype(BF16)
        acc_ref[...] = jnp.zeros_like(acc_ref)

    h = h_ref[...]
    act = (_silu(_dot(h, wg_ref[...])) * _dot(h, wu_ref[...])).astype(BF16)
    acc_ref[...] += _dot(act, wd_ref[...])

    @pl.when(j == pl.num_programs(2) - 1)
    def _():
        y = x_ref[0] + g2_ref[0] * acc_ref[...]
        if final:
            y = _rms(y, fg_ref[...])
        o_ref[0] = y


def _ffn(x, sh, sc, g2, ng, w_gu, w_down, fg, final):
    b, l, _ = x.shape
    tm = min(l, 1024)
    tf = 256
    nf = D_FF // tf
    xs = pl.BlockSpec((1, tm, D_MODEL), lambda bi, i, j: (bi, i, 0))
    mod = pl.BlockSpec((1, 1, D_MODEL), lambda bi, i, j: (bi, 0, 0))
    row = pl.BlockSpec((1, D_MODEL), lambda bi, i, j: (0, 0))
    return pl.pallas_call(
        functools.partial(_ffn_kernel, final=final),
        grid=(b, l // tm, nf),
        in_specs=[xs, mod, mod, mod, row,
                  pl.BlockSpec((D_MODEL, tf), lambda bi, i, j: (0, j)),
                  pl.BlockSpec((D_MODEL, tf), lambda bi, i, j: (0, nf + j)),
                  pl.BlockSpec((tf, D_MODEL), lambda bi, i, j: (j, 0)),
                  row],
        out_specs=xs,
        out_shape=jax.ShapeDtypeStruct(x.shape, F32),
        scratch_shapes=[pltpu.VMEM((tm, D_MODEL), BF16), pltpu.VMEM((tm, D_MODEL), F32)],
        compiler_params=_cp("parallel", "parallel", "arbitrary"),
        name="ffn",
    )(x, sh, sc, g2, ng, w_gu, w_gu, w_down, fg)


def _rope_tables(seq):
    t = np.arange(seq)
    inv = ROPE_THETA ** (-np.arange(0, ROPE_AXIS_DIM, 2, dtype=np.float64) / ROPE_AXIS_DIM)
    ang_r = (t // GRID_W)[:, None] * inv
    ang_c = (t % GRID_W)[:, None] * inv
    cos = np.concatenate([np.cos(ang_r), np.cos(ang_r), np.cos(ang_c), np.cos(ang_c)], axis=1)
    sin = np.concatenate([-np.sin(ang_r), np.sin(ang_r), -np.sin(ang_c), np.sin(ang_c)], axis=1)
    return (jnp.asarray(np.tile(cos, (1, 2)), F32), jnp.asarray(np.tile(sin, (1, 2)), F32))


def _hy_feats(seq):
    t = np.linspace(0.0, 1.0, seq)[:, None]
    w = 2.0 * math.pi * np.arange(seq)[:, None] / seq
    f = np.linspace(1e-4, HY_BANDS - 1, HY_BANDS)
    feats = np.concatenate([t, np.cos(f * w), -np.sin(f * w)], axis=1)
    feats_t = np.pad(feats, ((0, 0), (0, LANE - HY_EMB))).T
    return jnp.asarray(feats_t, F32), jnp.asarray(t, F32)


def _hy_abs_deltas():
    lo = math.log(HY_TARGET) / HY_SLOW_DECAY_PCT
    hi = math.log(HY_TARGET) / HY_FAST_DECAY_PCT
    return jnp.asarray(np.abs(np.linspace(lo, hi, HY_WIDTH))[None], F32)


def _fft_tables():
    r_ = FFT_R
    n = r_ * r_
    n2 = np.arange(r_)[:, None, None]
    k1 = np.arange(FFT_K1)[None, :, None]
    n1 = np.arange(r_ // 2)[None, None, :]
    live = (k1 <= r_ // 2).astype(np.float64)
    th = 2.0 * math.pi * ((k1 * (r_ * n1 + n2)) % n) / n
    cos, msin = np.cos(th) * live, -np.sin(th) * live
    wgt = np.where((k1 == 0) | (k1 == r_ // 2), 1.0, 2.0) / n
    fwd = np.concatenate([_group_blocks(cos), _group_blocks(msin)], axis=1)
    inv = np.transpose(np.concatenate([_group_blocks(cos * wgt), _group_blocks(msin * wgt)], axis=1), (0, 2, 1))
    kk = np.arange(r_)
    ph = 2.0 * math.pi * ((kk[:, None] * kk[None, :]) % r_) / r_
    fr, fi = np.cos(ph), -np.sin(ph)
    fs = np.block([[fr, -fi], [fi, fr]])
    fc = np.block([[fr, fi], [-fi, fr]])
    return tuple(jnp.asarray(a, F32) for a in (fwd, inv, fs, fc))


def _group_blocks(t):
    n2, k1, n1 = t.shape
    ng = n2 // FFT_G
    out = np.zeros((ng, k1, FFT_G, n1, FFT_G))
    for r in range(FFT_G):
        out[:, :, r, :, r] = t.reshape(ng, FFT_G, k1, n1)[:, r]
    return out.reshape(ng, k1 * FFT_G, n1 * FFT_G)


def _dense_dft_tables(seq):
    n = 2 * seq
    kk = np.arange(n)
    ph = 2.0 * math.pi * ((kk[:, None] * kk[None, :]) % n) / n
    ff = np.concatenate([np.cos(ph), -np.sin(ph)], axis=0)
    fi = np.concatenate([np.cos(ph[:seq]), -np.sin(ph[:seq])], axis=1) / n
    return jnp.asarray(ff, F32), jnp.asarray(fi, F32)


def _head_expand():
    e = np.zeros((LANE, 2 * SSD_INNER), np.float32)
    for h in range(2 * SSD_HEADS):
        e[h, h * SSD_HEAD_DIM:(h + 1) * SSD_HEAD_DIM] = 1.0
    return jnp.asarray(e, BF16)


def _group_mean():
    g = np.kron(np.eye(LANE // ATTN_HEAD_DIM), np.ones((ATTN_HEAD_DIM, ATTN_HEAD_DIM))) / ATTN_HEAD_DIM
    return jnp.asarray(g, BF16)


def _relayout_w_in(w):
    cols = [w[:, OFF_HY:OFF_HY + 3 * HY_WIDTH], w[:, OFF_Q:OFF_Q + ATTN_INNER],
            w[:, OFF_XB:OFF_XB + SSD_XB], w[:, OFF_C:OFF_C + SSD_GN], w[:, OFF_Z:OFF_Z + SSD_INNER],
            w[:, OFF_K:OFF_K + ATTN_KV_INNER], w[:, OFF_V:OFF_V + ATTN_KV_INNER],
            w[:, OFF_DT:OFF_DT + 2 * SSD_HEADS]]
    wr = jnp.concatenate(cols, axis=1)
    return jnp.pad(wr, ((0, 0), (0, NP - wr.shape[1]))).astype(BF16)


def _pad_row(v):
    v = v.reshape(1, -1)
    return jnp.pad(v, ((0, 0), (0, LANE - v.shape[1])))


def _layer_params(l, raw, tables):
    p = {k: v[l] for k, v in raw.items()}
    p["w_in_r"] = _relayout_w_in(p["w_in"])
    p["dt_bias_row"] = _pad_row(p["ssd_dt_bias"])
    p["a_log_row"] = _pad_row(p["ssd_a_log"])
    p["d_row"] = jnp.repeat(p["ssd_d"], SSD_HEAD_DIM)[None]
    p["head_expand"] = tables["head_expand"]
    p["gq"] = jnp.tile(p["q_norm_g"], LANE // ATTN_HEAD_DIM)[None]
    p["gk"] = jnp.tile(p["k_norm_g"], LANE // ATTN_HEAD_DIM)[None]
    p["hy_w1p"] = jnp.pad(p["hy_w1"], ((0, LANE - HY_EMB), (0, 0)))
    p["w_out_b"] = p["w_out"].astype(BF16)
    p["w_gu_b"] = p["w_gu"].astype(BF16)
    p["w_down_b"] = p["w_down"].astype(BF16)
    return p


def _mixers(proj, dt_raw, p, tables, is_ctx, ssd_init):
    yf, yb, sf, sb = _ssd_scan(proj, dt_raw, ssd_init[0], ssd_init[1], p)
    rope = tables["rope_ctx"] if is_ctx else tables["rope"]
    q, k, v = _attn_prep(proj, rope[0], rope[1], p["gq"], p["gk"], tables["group_mean"])
    return (yf, yb, sf, sb), (q, k, v)


def _hyena(proj, p, tables, is_ctx):
    seq = proj.shape[1]
    bias = p["hy_bias"][None]
    vx, x0 = _hy_conv(proj, p["hy_conv_w"], p["hy_conv_b"], not is_ctx)
    if is_ctx:
        f = _hy_taps(*tables["feats_ctx"], p, tables["abs_deltas"], seq, False)
        taps = jnp.concatenate([f[0], jnp.zeros((1, HY_WIDTH), F32), jnp.flip(f[1, 1:], axis=0)], axis=0)
        return _hy_ctx(vx, x0, taps, tables["dft_ctx"][0], tables["dft_ctx"][1], bias)
    b, ng, _, _, c = vx.shape
    rows = seq // ng
    t_fwd, t_inv, fs, fc = tables["fft"]
    f = _hy_taps(*tables["feats"], p, tables["abs_deltas"], seq, True)
    kre, kim = _fft2_filter(*_fft1(f.reshape(2, ng, rows, c), t_fwd), fs)
    vx = vx.reshape(b, ng, rows, c)
    are, aim = _fft1(vx, t_fwd)
    zre, zim = _fft2(are, aim, kre, kim, fs, fc)
    y = _ifft1(zre, zim, t_inv, vx, x0.reshape(b, ng, rows, c), bias)
    return y.reshape(b, ng, rows // (2 * FFT_G), 2 * FFT_G, c)


def _tail(x, proj, ssd, at, hy, mod, p, final_g, final):
    x = _mix_out(x, ssd[0], ssd[1], proj, at, hy, mod[2], p["ssd_norm_g"][None], p["attn_norm_g"][None],
                 p["hy_norm_g"][None], p["w_out_b"])
    return _ffn(x, mod[3], mod[4], mod[5], p["norm2_g"][None], p["w_gu_b"], p["w_down_b"], final_g[None], final)


def _layer(x, xc, mod_rows, p, tables, final_g, last):
    b = x.shape[0]
    mod_x = [mod_rows[:b, i * D_MODEL:(i + 1) * D_MODEL][:, None, :] for i in range(N_MOD)]
    mod_c = [jnp.broadcast_to(mod_rows[b:b + 1, i * D_MODEL:(i + 1) * D_MODEL][:, None, :], (b, 1, D_MODEL))
             for i in range(N_MOD)]
    g1 = p["norm1_g"][None]
    proj, dt_raw = _proj_in(x, mod_x[0], mod_x[1], g1, p["w_in_r"])
    lc = xc.shape[1]
    rows = lambda a: a.reshape(1, b * lc, a.shape[-1])
    mod_c1 = [m[:1] for m in mod_c]
    projc, dt_raw_c = _proj_in(rows(xc), mod_c1[0], mod_c1[1], g1, p["w_in_r"])
    projc, dt_raw_c = projc.reshape(b, lc, -1), dt_raw_c.reshape(b, lc, -1)
    zeros = jnp.zeros((b, SSD_STATE, SSD_INNER), F32)
    ssd_c, qkv_c = _mixers(projc, dt_raw_c, p, tables, True, (zeros, zeros))
    ssd_x, qkv_x = _mixers(proj, dt_raw, p, tables, False, (ssd_c[2], ssd_c[3]))
    k_all = jnp.concatenate([qkv_c[1], qkv_x[1]], axis=2)
    v_all = jnp.concatenate([qkv_c[2], qkv_x[2]], axis=3)
    at = _flash(qkv_x[0], k_all, v_all)
    hy = _hyena(proj, p, tables, False)
    x = _tail(x, proj, ssd_x, at, hy, mod_x, p, final_g, last)
    if last:
        return x, None
    at_c = _flash(qkv_c[0], qkv_c[1], qkv_c[2])
    hy_c = _hyena(projc, p, tables, True)
    xc = _tail(rows(xc), rows(projc), (rows(ssd_c[0]), rows(ssd_c[1])), rows(at_c), rows(hy_c), mod_c1, p,
               final_g, False)
    return x, xc.reshape(b, lc, -1)


def kernel(x, c, ctx, c_ctx, w_mod, b_mod, norm1_g, w_in, ssd_conv_w, ssd_conv_b, ssd_a_log, ssd_dt_bias, ssd_d, ssd_norm_g, q_norm_g, k_norm_g, attn_norm_g, hy_conv_w, hy_conv_b, hy_w1, hy_b1, hy_freq, hy_w2, hy_b2, hy_w3, hy_bias, hy_norm_g, w_out, norm2_g, w_gu, w_down, final_g):
    b, seq, _ = x.shape
    ctx_len = ctx.shape[1]
    depth = w_mod.shape[0]
    assert 2 * seq == FFT_R * FFT_R and b + 1 <= SUBLANE
    raw = dict(w_in=w_in, ssd_conv_w=ssd_conv_w, ssd_conv_b=ssd_conv_b, ssd_a_log=ssd_a_log,
               ssd_dt_bias=ssd_dt_bias, ssd_d=ssd_d, ssd_norm_g=ssd_norm_g, q_norm_g=q_norm_g,
               k_norm_g=k_norm_g, attn_norm_g=attn_norm_g, hy_conv_w=hy_conv_w, hy_conv_b=hy_conv_b,
               hy_w1=hy_w1, hy_b1=hy_b1, hy_freq=hy_freq, hy_w2=hy_w2, hy_b2=hy_b2, hy_w3=hy_w3,
               hy_bias=hy_bias, hy_norm_g=hy_norm_g, w_out=w_out, norm2_g=norm2_g, w_gu=w_gu, w_down=w_down,
               norm1_g=norm1_g)
    ones = jnp.ones((ctx_len, LANE), F32)
    tables = dict(rope=_rope_tables(seq), rope_ctx=(ones, jnp.zeros_like(ones)),
                  feats=_hy_feats(seq), feats_ctx=_hy_feats(ctx_len), abs_deltas=_hy_abs_deltas(),
                  fft=_fft_tables(), dft_ctx=_dense_dft_tables(ctx_len),
                  head_expand=_head_expand(), group_mean=_group_mean())
    c_rows = jnp.concatenate([c, c_ctx[None], jnp.zeros((SUBLANE - b - 1, D_MODEL), F32)], axis=0)
    xc = ctx
    for l in range(depth):
        p = _layer_params(l, raw, tables)
        mod_rows = _mod_call(c_rows, w_mod[l], b_mod[l][None])
        x, xc = _layer(x, xc, mod_rows, p, tables, final_g, l == depth - 1)
    return x
```
